```python
import math
import jax, jax.numpy as jnp
from jax import lax
import numpy as np

D_MODEL = 1024
BATCH = 8
SEQ = 4096
DEPTH = 1

D_FF = 2816
DN_HEADS = 8
DN_DK = 128
DN_DV = 128
CONV_K = 4
CHUNK = 64
SA_HEADS = 8
SA_DH = 128
Q_RANK = 256
KV_RANK = 256
IDX_HEADS = 8
IDX_DIM = 64
TOPK_MAX = 256
Q_BLOCK = 128
REL_BUCKETS = 32
REL_MAX_DIST = 128
EPS = 1e-6

DN_QK_W = DN_HEADS * DN_DK
DN_V_W = DN_HEADS * DN_DV
SA_W = SA_HEADS * SA_DH
IN_SPLITS = (2 * DN_QK_W + DN_V_W,
             DN_V_W,
             DN_HEADS,
             DN_HEADS,
             Q_RANK,
             KV_RANK,
             IDX_DIM,
             IDX_HEADS,
             D_MODEL,
             D_MODEL)
IN_WIDTH = sum(IN_SPLITS)

kernel_name = "hybrid_gdn_dsa_macaron_block"


def rmsnorm(x, g):
    xf = x.astype(jnp.float32)
    y = xf * lax.rsqrt(jnp.mean(xf * xf, axis=-1, keepdims=True) + EPS)
    return (y * g.astype(jnp.float32)).astype(x.dtype)


def layernorm(x, g, b):
    xf = x.astype(jnp.float32)
    mu = jnp.mean(xf, axis=-1, keepdims=True)
    var = jnp.mean(jnp.square(xf - mu), axis=-1, keepdims=True)
    y = (xf - mu) * lax.rsqrt(var + EPS)
    return (y * g.astype(jnp.float32) + b.astype(jnp.float32)).astype(x.dtype)


def l2norm(t):
    return t * lax.rsqrt(jnp.sum(t * t, axis=-1, keepdims=True) + EPS)


def swiglu(h, wg, wu, wd):
    return (jax.nn.silu(h @ wg) * (h @ wu)) @ wd


def split_cols(t, sizes):
    out, o = [], 0
    for s in sizes:
        out.append(t[..., o:o + s])
        o += s
    return out


def causal_dwconv(x, w):
    return lax.conv_general_dilated(
        x, w[:, None, :].astype(x.dtype), window_strides=(1,),
        padding=[(CONV_K - 1, 0)], dimension_numbers=('NWC', 'WIO', 'NWC'),
        feature_group_count=x.shape[-1])


def t5_bucket(n):
    max_exact = REL_BUCKETS // 2
    nf = jnp.maximum(n, 1).astype(jnp.float32)
    large = max_exact + (jnp.log(nf / max_exact) / math.log(REL_MAX_DIST / max_exact)
                         * (REL_BUCKETS - max_exact)).astype(jnp.int32)
    large = jnp.minimum(large, REL_BUCKETS - 1)
    return jnp.where(n < max_exact, n, large)


def gated_delta_rule(q, k, v, g, beta):
    B, S, H, dk = q.shape
    dv = v.shape[-1]
    nc = S // CHUNK

    def chunk4(t):
        return t.reshape(B, nc, CHUNK, H, t.shape[-1]).transpose(1, 0, 3, 2, 4)

    def chunk3(t):
        return t.reshape(B, nc, CHUNK, H).transpose(1, 0, 3, 2)

    q_c, k_c, v_c = chunk4(q), chunk4(k), chunk4(v)
    b_c = chunk3(beta)
    g_cum = jnp.cumsum(chunk3(g), axis=-1)
    pos = jnp.arange(CHUNK)
    tril = pos[:, None] >= pos[None, :]
    strict = pos[:, None] > pos[None, :]
    decay = jnp.exp(jnp.where(tril, g_cum[..., :, None] - g_cum[..., None, :], -jnp.inf))
    k_beta = k_c * b_c[..., None]
    v_beta = v_c * b_c[..., None]
    L = jnp.where(strict, jnp.einsum('nbhid,nbhjd->nbhij', k_beta, k_c) * decay, 0.0)
    eye = jnp.eye(CHUNK, dtype=L.dtype)
    T = lax.linalg.triangular_solve(eye + L, jnp.broadcast_to(eye, L.shape),
                                    left_side=True, lower=True)
    u = T @ v_beta
    w = T @ (k_beta * jnp.exp(g_cum)[..., None])
    intra = jnp.einsum('nbhid,nbhjd->nbhij', q_c, k_c) * decay

    def step(state, inp):
        q_i, k_i, u_i, w_i, a_i, gc_i = inp
        v_new = u_i - w_i @ state
        o_i = (q_i * jnp.exp(gc_i)[..., None]) @ state + a_i @ v_new
        g_last = gc_i[..., -1]
        state = state * jnp.exp(g_last)[..., None, None] + jnp.einsum(
            'bhcd,bhce->bhde', k_i * jnp.exp(g_last[..., None] - gc_i)[..., None], v_new)
        return state, o_i

    s0 = jnp.zeros((B, H, dk, dv), jnp.float32)
    _, o = lax.scan(step, s0, (q_c, k_c, u, w, intra, g_cum))
    return o.transpose(1, 0, 3, 2, 4).reshape(B, S, H, dv)


def dsa_attention(cq, ckv, k_idx, w_idx, w_uq, w_uk, w_uv, w_iq, rel_bias):
    B, S, _ = cq.shape
    nb = S // Q_BLOCK
    topk = min(TOPK_MAX, S // 4)
    key_pos = jnp.arange(S, dtype=jnp.int32)

    def blocks(t):
        return t.reshape(B, nb, Q_BLOCK, t.shape[-1]).transpose(1, 0, 2, 3)

    def attend_block(inp):
        cq_b, w_b, t0 = inp
        t = t0 + jnp.arange(Q_BLOCK, dtype=jnp.int32)
        q_idx = (cq_b @ w_iq).reshape(B, Q_BLOCK, IDX_HEADS, IDX_DIM)
        dots = jax.nn.relu(jnp.einsum('bqhd,bsd->bqhs', q_idx, k_idx))
        score = jnp.einsum('bqhs,bqh->bqs', dots, w_b).astype(jnp.float32)
        score = jnp.where(key_pos[None, None, :] <= t[None, :, None], score, -jnp.inf)
        _, sel = lax.top_k(score, topk)
        valid = sel <= t[None, :, None]
        c_sel = jax.vmap(lambda c, i: c[i])(ckv, sel)
        q = (cq_b @ w_uq).reshape(B, Q_BLOCK, SA_HEADS, SA_DH)
        q_lat = jnp.einsum('bqhd,rhd->bqhr', q, w_uk)
        logits = jnp.einsum('bqhr,bqkr->bqhk', q_lat, c_sel).astype(jnp.float32) * (SA_DH ** -0.5)
        bias = rel_bias[t5_bucket(jnp.maximum(t[None, :, None] - sel, 0))]
        logits = logits + jnp.moveaxis(bias, -1, 2).astype(jnp.float32)
        logits = jnp.where(valid[:, :, None, :], logits, -jnp.inf)
        p = jax.nn.softmax(logits, axis=-1).astype(c_sel.dtype)
        o_lat = jnp.einsum('bqhk,bqkr->bqhr', p, c_sel)
        o = jnp.einsum('bqhr,rhd->bqhd', o_lat, w_uv)
        return o.reshape(B, Q_BLOCK, SA_W)

    starts = jnp.arange(nb, dtype=jnp.int32) * Q_BLOCK
    out = lax.map(attend_block, (blocks(cq), blocks(w_idx), starts))
    return out.transpose(1, 0, 2, 3).reshape(B, S, SA_W)


def setup_inputs(seed: int = 0) -> dict:
    key = jax.random.key(seed)
    ks = iter(jax.random.split(key, 32))
    f32 = jnp.float32

    def nrm(shape, scale):
        return jax.random.normal(next(ks), shape, f32) * scale

    def gain(shape):
        return 1.0 + 0.01 * jax.random.normal(next(ks), shape, f32)

    x = jax.random.normal(next(ks), (BATCH, SEQ, D_MODEL), f32)
    a_log = jnp.log(jax.random.uniform(next(ks), (DEPTH, DN_HEADS), f32, 1.0, 16.0))
    dt = jnp.exp(jax.random.uniform(next(ks), (DEPTH, DN_HEADS), f32,
                                    math.log(1e-3), math.log(1e-1)))
    dt_bias = dt + jnp.log(-jnp.expm1(-dt))
    return {
        "x": x,
        "ffn1_norm": gain((DEPTH, D_MODEL)),
        "ffn1_wg": nrm((DEPTH, D_MODEL, D_FF), D_MODEL ** -0.5),
        "ffn1_wu": nrm((DEPTH, D_MODEL, D_FF), D_MODEL ** -0.5),
        "ffn1_wd": nrm((DEPTH, D_FF, D_MODEL), D_FF ** -0.5),
        "mix_norm": gain((DEPTH, D_MODEL)),
        "w_in": nrm((DEPTH, D_MODEL, IN_WIDTH), D_MODEL ** -0.5),
        "conv_w": nrm((DEPTH, CONV_K, 2 * DN_QK_W + DN_V_W), CONV_K ** -0.5),
        "a_log": a_log,
        "dt_bias": dt_bias,
        "dn_out_norm": gain((DEPTH, DN_DV)),
        "q_norm": gain((DEPTH, Q_RANK)),
        "kv_norm": gain((DEPTH, KV_RANK)),
        "w_uq": nrm((DEPTH, Q_RANK, SA_W), Q_RANK ** -0.5),
        "w_uk": nrm((DEPTH, KV_RANK, SA_HEADS, SA_DH), KV_RANK ** -0.5),
        "w_uv": nrm((DEPTH, KV_RANK, SA_HEADS, SA_DH), KV_RANK ** -0.5),
        "w_iq": nrm((DEPTH, Q_RANK, IDX_HEADS * IDX_DIM), Q_RANK ** -0.5),
        "idx_k_g": gain((DEPTH, IDX_DIM)),
        "idx_k_b": nrm((DEPTH, IDX_DIM), 0.01),
        "rel_bias": nrm((REL_BUCKETS, SA_HEADS), 0.2),
        "w_o": nrm((DEPTH, D_MODEL, D_MODEL), D_MODEL ** -0.5),
        "ffn2_norm": gain((DEPTH, D_MODEL)),
        "ffn2_wg": nrm((DEPTH, D_MODEL, D_FF), D_MODEL ** -0.5),
        "ffn2_wu": nrm((DEPTH, D_MODEL, D_FF), D_MODEL ** -0.5),
        "ffn2_wd": nrm((DEPTH, D_FF, D_MODEL), D_FF ** -0.5),
        "final_norm": gain((D_MODEL,)),
    }


def reference(x, ffn1_norm, ffn1_wg, ffn1_wu, ffn1_wd, mix_norm, w_in, conv_w, a_log,
              dt_bias, dn_out_norm, q_norm, kv_norm, w_uq, w_uk, w_uv, w_iq, idx_k_g,
              idx_k_b, rel_bias, w_o, ffn2_norm, ffn2_wg, ffn2_wu, ffn2_wd, final_norm):
    B, S, _ = x.shape
    for l in range(DEPTH):
        x = x + 0.5 * swiglu(rmsnorm(x, ffn1_norm[l]), ffn1_wg[l], ffn1_wu[l], ffn1_wd[l])

        h = rmsnorm(x, mix_norm[l])
        proj = h @ w_in[l]
        (dn_qkv, dn_z, dn_b, dn_a, c_q, c_kv, idx_k_raw, idx_w,
         gate_a, gate_b) = split_cols(proj, IN_SPLITS)

        qkv = jax.nn.silu(causal_dwconv(dn_qkv, conv_w[l])).astype(jnp.float32)
        dq, dk_, dvv = split_cols(qkv, (DN_QK_W, DN_QK_W, DN_V_W))
        dq = l2norm(dq.reshape(B, S, DN_HEADS, DN_DK)) * (DN_DK ** -0.5)
        dk_ = l2norm(dk_.reshape(B, S, DN_HEADS, DN_DK))
        dvv = dvv.reshape(B, S, DN_HEADS, DN_DV)
        beta = jax.nn.sigmoid(dn_b.astype(jnp.float32))
        g = -jnp.exp(a_log[l].astype(jnp.float32)) * jax.nn.softplus(
            dn_a.astype(jnp.float32) + dt_bias[l].astype(jnp.float32))
        o_dn = gated_delta_rule(dq, dk_, dvv, g, beta)
        o_dn = rmsnorm(o_dn, dn_out_norm[l]) * jax.nn.silu(
            dn_z.reshape(B, S, DN_HEADS, DN_DV).astype(jnp.float32))
        o_dn = o_dn.reshape(B, S, DN_V_W).astype(x.dtype)

        cq = rmsnorm(c_q, q_norm[l])
        ckv = rmsnorm(c_kv, kv_norm[l])
        k_idx = layernorm(idx_k_raw, idx_k_g[l], idx_k_b[l])
        w_idx = idx_w * ((IDX_HEADS ** -0.5) * (IDX_DIM ** -0.5))
        o_sa = dsa_attention(cq, ckv, k_idx, w_idx, w_uq[l], w_uk[l], w_uv[l], w_iq[l],
                             rel_bias).astype(x.dtype)

        merged = jax.nn.sigmoid(gate_a) * o_dn + jax.nn.sigmoid(gate_b) * o_sa
        x = x + merged @ w_o[l]

        x = x + 0.5 * swiglu(rmsnorm(x, ffn2_norm[l]), ffn2_wg[l], ffn2_wu[l], ffn2_wd[l])
    return rmsnorm(x, final_norm)
```

```python
import functools
import math

import numpy as np
import jax
import jax.numpy as jnp
from jax import lax
from jax.experimental import pallas as pl
from jax.experimental.pallas import tpu as pltpu

F32 = jnp.float32
BF16 = jnp.bfloat16
EPS = 1e-6
TOPK_MAX = 256
REL_MAX_DIST = 128
DN_CHUNK = 64
LANE = 128
INT_MIN = -(2 ** 31)
NEG_BIG = -1e30
VMEM_LIMIT = 48 * 1024 * 1024


def _cparams(sem):
    return pltpu.CompilerParams(dimension_semantics=sem, vmem_limit_bytes=VMEM_LIMIT)


def _rms(x, g):
    return x * lax.rsqrt(jnp.mean(x * x, axis=-1, keepdims=True) + EPS) * g


def _silu(x):
    return x * jax.nn.sigmoid(x)


def _dot(a, b):
    return jnp.dot(a, b, preferred_element_type=F32)


def _dot_nt(a, b):
    return lax.dot_general(a, b, (((1,), (1,)), ((), ())), preferred_element_type=F32)


def _dot_tn(a, b):
    return lax.dot_general(a, b, (((0,), (0,)), ((), ())), preferred_element_type=F32)


def _ffn_kernel(x_ref, g_ref, wg_ref, wu_ref, wd_ref, *rest, final):
    if final:
        fg_ref, o_ref, h_ref, acc_ref = rest
    else:
        o_ref, h_ref, acc_ref = rest
    j = pl.program_id(1)

    @pl.when(j == 0)
    def _():
        h_ref[...] = _rms(x_ref[...], g_ref[...]).astype(BF16)
        acc_ref[...] = jnp.zeros_like(acc_ref)

    h = h_ref[...]
    a = _dot(h, wg_ref[...])
    u = _dot(h, wu_ref[...])
    act = (_silu(a) * u).astype(BF16)
    acc_ref[...] += _dot(act, wd_ref[...])

    @pl.when(j == pl.num_programs(1) - 1)
    def _():
        y = x_ref[...] + 0.5 * acc_ref[...]
        if final:
            y = _rms(y, fg_ref[...])
        o_ref[...] = y


def _ffn(x2d, g, wg, wu, wd, final_g=None, *, tm=512, tf=256):
    n, d = x2d.shape
    ff = wg.shape[1]
    final = final_g is not None
    in_specs = [
        pl.BlockSpec((tm, d), lambda i, j: (i, 0)),
        pl.BlockSpec((1, d), lambda i, j: (0, 0)),
        pl.BlockSpec((d, tf), lambda i, j: (0, j)),
        pl.BlockSpec((d, tf), lambda i, j: (0, j)),
        pl.BlockSpec((tf, d), lambda i, j: (j, 0)),
    ]
    args = [x2d, g.reshape(1, d), wg.astype(BF16), wu.astype(BF16), wd.astype(BF16)]
    if final:
        in_specs.append(pl.BlockSpec((1, d), lambda i, j: (0, 0)))
        args.append(final_g.reshape(1, d))
    return pl.pallas_call(
        functools.partial(_ffn_kernel, final=final),
        grid=(n // tm, ff // tf),
        in_specs=in_specs,
        out_specs=pl.BlockSpec((tm, d), lambda i, j: (i, 0)),
        out_shape=jax.ShapeDtypeStruct((n, d), F32),
        scratch_shapes=[pltpu.VMEM((tm, d), BF16), pltpu.VMEM((tm, d), F32)],
        compiler_params=_cparams(("parallel", "arbitrary")),
        name="ffn_final" if final else "ffn",
    )(*args)


def _nmm_kernel(x_ref, g_ref, w_ref, o_ref, h_ref):
    @pl.when(pl.program_id(1) == 0)
    def _():
        h_ref[...] = _rms(x_ref[...], g_ref[...]).astype(BF16)

    o_ref[...] = _dot(h_ref[...], w_ref[...])


def _norm_matmul(x2d, g, w, *, tm=512, tn=512):
    n, d = x2d.shape
    cols = w.shape[1]
    return pl.pallas_call(
        _nmm_kernel,
        grid=(n // tm, cols // tn),
        in_specs=[
            pl.BlockSpec((tm, d), lambda i, j: (i, 0)),
            pl.BlockSpec((1, d), lambda i, j: (0, 0)),
            pl.BlockSpec((d, tn), lambda i, j: (0, j)),
        ],
        out_specs=pl.BlockSpec((tm, tn), lambda i, j: (i, j)),
        out_shape=jax.ShapeDtypeStruct((n, cols), F32),
        scratch_shapes=[pltpu.VMEM((tm, d), BF16)],
        compiler_params=_cparams(("parallel", "arbitrary")),
        name="in_proj_wide",
    )(x2d, g.reshape(1, d), w.astype(BF16))


def _small_layout(q_rank, kv_rank, idx_dim):
    o_cq = 0
    o_ckv = o_cq + q_rank
    o_ik = o_ckv + kv_rank
    o_misc = o_ik + LANE * ((idx_dim + LANE - 1) // LANE)
    return o_cq, o_ckv, o_ik, o_misc, o_misc + LANE


def _small_kernel(x_ref, g_ref, w_ref, qn_ref, kvn_ref, ikg_ref, ikb_ref, negA_ref, dtb_ref,
                  cq_ref, ckv_ref, ik_ref, misc_ref, *, q_rank, kv_rank, idx_dim, nh, w_scale):
    o_cq, o_ckv, o_ik, o_misc, _ = _small_layout(q_rank, kv_rank, idx_dim)
    h = _rms(x_ref[...], g_ref[...]).astype(BF16)
    p = _dot(h, w_ref[...])
    cq_ref[...] = _rms(p[:, o_cq:o_cq + q_rank], qn_ref[...]).astype(cq_ref.dtype)
    ckv_ref[...] = _rms(p[:, o_ckv:o_ckv + kv_rank], kvn_ref[...]).astype(ckv_ref.dtype)
    ik = p[:, o_ik:o_ik + idx_dim]
    mu = jnp.mean(ik, axis=-1, keepdims=True)
    var = jnp.mean(jnp.square(ik - mu), axis=-1, keepdims=True)
    ik_ref[...] = ((ik - mu) * lax.rsqrt(var + EPS) * ikg_ref[...] + ikb_ref[...]).astype(ik_ref.dtype)
    m = p[:, o_misc:o_misc + LANE]
    lane = lax.broadcasted_iota(jnp.int32, m.shape, 1)
    beta = jax.nn.sigmoid(m)
    sp_in = m + dtb_ref[...]
    softplus = jnp.maximum(sp_in, 0.0) + jnp.log(1.0 + jnp.exp(-jnp.abs(sp_in)))
    gdec = negA_ref[...] * softplus
    misc_ref[...] = jnp.where(lane < nh, m * w_scale,
                              jnp.where(lane < 2 * nh, beta,
                                        jnp.where(lane < 3 * nh, gdec, 0.0)))


def _small_proj(x2d, g, w_small, q_norm, kv_norm, ik_g, ik_b, neg_a_pad, dtb_pad, *,
                q_rank, kv_rank, idx_dim, nh, w_scale, tm=512):
    n, d = x2d.shape
    cols = w_small.shape[1]
    full = lambda shape: pl.BlockSpec(shape, lambda i: (0,) * len(shape))
    return pl.pallas_call(
        functools.partial(_small_kernel, q_rank=q_rank, kv_rank=kv_rank, idx_dim=idx_dim, nh=nh,
                          w_scale=w_scale),
        grid=(n // tm,),
        in_specs=[
            pl.BlockSpec((tm, d), lambda i: (i, 0)),
            full((1, d)), full((d, cols)), full((1, q_rank)), full((1, kv_rank)),
            full((1, idx_dim)), full((1, idx_dim)), full((1, LANE)), full((1, LANE)),
        ],
        out_specs=[
            pl.BlockSpec((tm, q_rank), lambda i: (i, 0)),
            pl.BlockSpec((tm, kv_rank), lambda i: (i, 0)),
            pl.BlockSpec((tm, idx_dim), lambda i: (i, 0)),
            pl.BlockSpec((tm, LANE), lambda i: (i, 0)),
        ],
        out_shape=[
            jax.ShapeDtypeStruct((n, q_rank), BF16),
            jax.ShapeDtypeStruct((n, kv_rank), BF16),
            jax.ShapeDtypeStruct((n, idx_dim), BF16),
            jax.ShapeDtypeStruct((n, LANE), F32),
        ],
        compiler_params=_cparams(("parallel",)),
        name="in_proj_small",
    )(x2d, g.reshape(1, d), w_small, q_norm.reshape(1, -1), kv_norm.reshape(1, -1),
      ik_g.reshape(1, -1), ik_b.reshape(1, -1), neg_a_pad, dtb_pad)


def _tri_inverse(lmat, masks, eye):
    x = eye - lmat * masks[0]
    for mk in masks[1:]:
        lm = (lmat * mk).astype(BF16)
        t = _dot(x.astype(BF16), lm)
        x = x - _dot(t.astype(BF16), x.astype(BF16))
    return x


def _deltanet_kernel(q_ref, k_ref, v_ref, z_ref, cw_ref, bg_ref, gr_ref, on_ref, o_ref,
                     xpad_ref, act_ref, s_ref, *, nh, dk, ts, conv_k):
    c = DN_CHUNK
    i = pl.program_id(1)
    qk_w = nh * dk
    halo = 8

    @pl.when(i == 0)
    def _():
        xpad_ref[0:halo, :] = jnp.zeros((halo, 3 * qk_w), F32)
        s_ref[...] = jnp.zeros_like(s_ref)

    xpad_ref[halo:halo + ts, 0:qk_w] = q_ref[0]
    xpad_ref[halo:halo + ts, qk_w:2 * qk_w] = k_ref[0]
    xpad_ref[halo:halo + ts, 2 * qk_w:3 * qk_w] = v_ref[0]
    win = xpad_ref[...]
    y = win[halo:, :] * cw_ref[conv_k - 1:conv_k, :]
    for dlt in range(1, conv_k):
        y = y + pltpu.roll(win, dlt, 0)[halo:, :] * cw_ref[conv_k - 1 - dlt:conv_k - dlt, :]
    act_ref[...] = _silu(y)
    xpad_ref[0:halo, :] = win[ts:ts + halo, :]

    row = lax.broadcasted_iota(jnp.int32, (c, c), 0)
    col = lax.broadcasted_iota(jnp.int32, (c, c), 1)
    tril = row >= col
    strict = row > col
    eye = jnp.where(row == col, 1.0, 0.0).astype(F32)
    tril_f = jnp.where(tril, 1.0, 0.0).astype(F32)
    masks = []
    m = 1
    while m < c:
        mk = strict & ((row // (2 * m)) == (col // (2 * m))) & ((row // m) != (col // m))
        masks.append(jnp.where(mk, 1.0, 0.0).astype(F32))
        m *= 2
    onorm = on_ref[...]

    def chunk_body(ci, carry):
        r0 = pl.multiple_of(ci * c, c)
        bg = bg_ref[0, pl.ds(r0, c), :]
        grow = gr_ref[0, ci]
        gc_col = jnp.dot(tril_f, bg, precision=lax.Precision.HIGHEST,
                         preferred_element_type=F32)
        gc_row = lax.dot_general(grow, tril_f, (((1,), (1,)), ((), ())),
                                 precision=lax.Precision.HIGHEST,
                                 preferred_element_type=F32)
        for h in range(nh):
            q = act_ref[pl.ds(r0, c), h * dk:(h + 1) * dk]
            k = act_ref[pl.ds(r0, c), qk_w + h * dk:qk_w + (h + 1) * dk]
            v = act_ref[pl.ds(r0, c), 2 * qk_w + h * dk:2 * qk_w + (h + 1) * dk]
            q = q * (lax.rsqrt(jnp.sum(q * q, axis=-1, keepdims=True) + EPS) * (dk ** -0.5))
            k = k * lax.rsqrt(jnp.sum(k * k, axis=-1, keepdims=True) + EPS)
            beta = bg[:, nh + h:nh + h + 1]
            gcc = gc_col[:, 2 * nh + h:2 * nh + h + 1]
            gcr = gc_row[h:h + 1, :]
            g_last = gc_row[h:h + 1, c - 1:c]
            decay = jnp.where(tril, jnp.exp(jnp.minimum(gcc - gcr, 0.0)), 0.0)
            eg = jnp.exp(gcc)
            kb = k * beta
            vb = v * beta
            kbf = k.astype(BF16)
            aq = _dot_nt(jnp.concatenate([kb, q], axis=0).astype(BF16), kbf)
            lmat = jnp.where(strict, aq[:c] * decay, 0.0)
            intra = aq[c:] * decay
            tinv = _tri_inverse(lmat, masks, eye)
            uw = _dot(tinv.astype(BF16), jnp.concatenate([vb, kb * eg], axis=1).astype(BF16))
            u = uw[:, :dk]
            w = uw[:, dk:]
            s = s_ref[h]
            m1 = _dot(jnp.concatenate([w, q * eg], axis=0).astype(BF16), s.astype(BF16))
            v_new = u - m1[:c]
            vnb = v_new.astype(BF16)
            o = m1[c:] + _dot(intra.astype(BF16), vnb)
            kd = k * jnp.exp(g_last - gcc)
            s_ref[h] = s * jnp.exp(g_last) + _dot_tn(kd.astype(BF16), vnb)
            zz = z_ref[0, pl.ds(r0, c), h * dk:(h + 1) * dk]
            o_ref[0, pl.ds(r0, c), h * dk:(h + 1) * dk] = _rms(o, onorm) * _silu(zz)
        return carry

    lax.fori_loop(0, ts // c, chunk_body, 0)


def _deltanet(big, conv_w, bg, g_row, out_norm, *, nh, dk, ts=256):
    b, s, _ = big.shape
    qk_w = nh * dk
    conv_k = conv_w.shape[0]
    return pl.pallas_call(
        functools.partial(_deltanet_kernel, nh=nh, dk=dk, ts=ts, conv_k=conv_k),
        grid=(b, s // ts),
        in_specs=[
            pl.BlockSpec((1, ts, qk_w), lambda bi, i: (bi, i, 0)),
            pl.BlockSpec((1, ts, qk_w), lambda bi, i: (bi, i, 1)),
            pl.BlockSpec((1, ts, qk_w), lambda bi, i: (bi, i, 2)),
            pl.BlockSpec((1, ts, qk_w), lambda bi, i: (bi, i, 3)),
            pl.BlockSpec((conv_k, 3 * qk_w), lambda bi, i: (0, 0)),
            pl.BlockSpec((1, ts, LANE), lambda bi, i: (bi, i, 0)),
            pl.BlockSpec((1, ts // DN_CHUNK, nh, DN_CHUNK), lambda bi, i: (bi, i, 0, 0)),
            pl.BlockSpec((1, dk), lambda bi, i: (0, 0)),
        ],
        out_specs=pl.BlockSpec((1, ts, qk_w), lambda bi, i: (bi, i, 0)),
        out_shape=jax.ShapeDtypeStruct((b, s, qk_w), F32),
        scratch_shapes=[
            pltpu.VMEM((ts + 8, 3 * qk_w), F32),
            pltpu.VMEM((ts, 3 * qk_w), F32),
            pltpu.VMEM((nh, dk, dk), F32),
        ],
        compiler_params=_cparams(("arbitrary", "arbitrary")),
        name="deltanet",
    )(big, big, big, big, conv_w, bg, g_row, out_norm.reshape(1, dk))


def _sortable(x):
    i = pltpu.bitcast(x, jnp.int32)
    return i ^ ((i >> 31) & jnp.int32(0x7FFFFFFF))


def _dsa_kernel(cq_ref, wr_ref, ki_ref, ckv_ref, ckvt_ref, wiq_ref, wuq_ref, wuk_ref, wuv_ref,
                bias_ref, o_ref, key_ref, acc_ref, *, nh_idx, idx_dim, nh, dh, topk, scale):
    tq = LANE
    tk = LANE
    j = pl.program_id(1)
    nkt = j + 1
    cqb = cq_ref[0]
    rows = lax.broadcasted_iota(jnp.int32, (tk, tq), 0)
    cols = lax.broadcasted_iota(jnp.int32, (tk, tq), 1)

    qidx_t = _dot_nt(wiq_ref[...], cqb).astype(BF16)
    wrow = wr_ref[0]

    def score_body(kt, carry):
        s0 = pl.multiple_of(kt * tk, tk)
        kk = ki_ref[0, pl.ds(s0, tk), :]
        sc = jnp.zeros((tk, tq), F32)
        for h in range(nh_idx):
            d = _dot(kk, qidx_t[h * idx_dim:(h + 1) * idx_dim, :])
            sc = sc + wrow[h:h + 1, :] * jnp.maximum(d, 0.0)
        causal = (s0 + rows) <= (j * tq + cols)
        key_ref[pl.ds(s0, tk), :] = jnp.where(causal, _sortable(sc), INT_MIN)
        return carry

    lax.fori_loop(0, nkt, score_body, 0)

    def count_ge(cand):
        def body(kt, acc):
            s0 = pl.multiple_of(kt * tk, tk)
            kv = key_ref[pl.ds(s0, tk), :]
            hit = jnp.where(kv >= cand, 1, 0).astype(jnp.int32)
            return acc + jnp.sum(hit.reshape(tk // 8, 8, tq), axis=0)
        acc = lax.fori_loop(0, nkt, body, jnp.zeros((8, tq), jnp.int32))
        return jnp.sum(acc, axis=0, keepdims=True)

    cnt0 = count_ge(jnp.zeros((1, tq), jnp.int32))
    ans0 = jnp.where(cnt0 >= topk, 0, INT_MIN).astype(jnp.int32)

    def bit_body(bi, ans):
        cand = ans | jnp.left_shift(jnp.int32(1), 30 - bi)
        return jnp.where(count_ge(cand) >= topk, cand, ans)

    ans = lax.fori_loop(0, 31, bit_body, ans0)
    thr = jnp.maximum(ans, INT_MIN + 1)

    cnt_ge = count_ge(thr)
    any_tie = jnp.max(jnp.where(cnt_ge > topk, 1, 0)) > 0

    @pl.when(any_tie)
    def _():
        need = topk - count_ge(thr + 1)

        def count_eq_below(lim):
            def body(kt, acc):
                s0 = pl.multiple_of(kt * tk, tk)
                kv = key_ref[pl.ds(s0, tk), :]
                hit = jnp.where((kv == thr) & ((s0 + rows) < lim), 1, 0).astype(jnp.int32)
                return acc + jnp.sum(hit.reshape(tk // 8, 8, tq), axis=0)
            acc = lax.fori_loop(0, nkt, body, jnp.zeros((8, tq), jnp.int32))
            return jnp.sum(acc, axis=0, keepdims=True)

        nbits = int(math.ceil(math.log2(key_ref.shape[0]))) + 1

        def jbit_body(bi, lim):
            cand = lim | jnp.left_shift(jnp.int32(1), nbits - 1 - bi)
            return jnp.where(count_eq_below(cand) <= need, cand, lim)

        lim = lax.fori_loop(0, nbits, jbit_body, jnp.zeros((1, tq), jnp.int32))

        def demote_body(kt, carry):
            s0 = pl.multiple_of(kt * tk, tk)
            kv = key_ref[pl.ds(s0, tk), :]
            drop = (kv == thr) & ((s0 + rows) >= lim)
            key_ref[pl.ds(s0, tk), :] = jnp.where(drop, INT_MIN, kv)
            return carry

        lax.fori_loop(0, nkt, demote_body, 0)

    q_t = _dot_nt(wuq_ref[...], cqb)
    qlat = []
    for h in range(nh):
        ql = _dot(wuk_ref[h], q_t[h * dh:(h + 1) * dh, :].astype(BF16)) * scale
        qlat.append(ql.astype(BF16))
    qlat_t = jnp.concatenate(qlat, axis=1)
    thr_h = jnp.concatenate([thr] * nh, axis=1)
    acc_ref[...] = jnp.zeros_like(acc_ref)

    def attn_body(kt, carry):
        m_run, l_run = carry
        s0 = pl.multiple_of(kt * tk, tk)
        cblk = ckv_ref[0, pl.ds(s0, tk), :]
        lg = _dot(cblk, qlat_t)
        bias = bias_ref[jnp.minimum(j - kt, 2)]
        kv = key_ref[pl.ds(s0, tk), :]
        keys = jnp.concatenate([kv] * nh, axis=1)
        lg = jnp.where(keys >= thr_h, lg + bias, NEG_BIG)
        m_new = jnp.maximum(m_run, jnp.max(lg, axis=0, keepdims=True))
        p = jnp.exp(lg - m_new)
        alpha = jnp.exp(m_run - m_new)
        l_new = l_run * alpha + jnp.sum(p, axis=0, keepdims=True)
        ct = ckvt_ref[0, :, pl.ds(s0, tk)]
        acc_ref[...] = acc_ref[...] * alpha + _dot(ct, p.astype(BF16))
        return m_new, l_new

    m0 = jnp.full((1, nh * tq), NEG_BIG, F32)
    l0 = jnp.zeros((1, nh * tq), F32)
    _, l_fin = lax.fori_loop(0, nkt, attn_body, (m0, l0))
    olat = (acc_ref[...] * (1.0 / l_fin)).astype(BF16)
    for h in range(nh):
        o_t = _dot(wuv_ref[h], olat[:, h * tq:(h + 1) * tq])
        o_ref[0, :, h * dh:(h + 1) * dh] = o_t.T


def _dsa(cq, widx_t, kidx, ckv, ckv_t, wiq_t, wuq_t, wuk_h, wuv_h, bias_tab, *, topk, scale):
    b, s, q_rank = cq.shape
    kv_rank = ckv.shape[-1]
    nh_idx = widx_t.shape[1]
    idx_dim = kidx.shape[-1]
    nh, dh, _ = wuv_h.shape
    tq = LANE
    full = lambda shape: pl.BlockSpec(shape, lambda bi, j: (0,) * len(shape))
    return pl.pallas_call(
        functools.partial(_dsa_kernel, nh_idx=nh_idx, idx_dim=idx_dim, nh=nh, dh=dh, topk=topk,
                          scale=scale),
        grid=(b, s // tq),
        in_specs=[
            pl.BlockSpec((1, tq, q_rank), lambda bi, j: (bi, j, 0)),
            pl.BlockSpec((1, nh_idx, tq), lambda bi, j: (bi, 0, j)),
            pl.BlockSpec((1, s, idx_dim), lambda bi, j: (bi, 0, 0)),
            pl.BlockSpec((1, s, kv_rank), lambda bi, j: (bi, 0, 0)),
            pl.BlockSpec((1, kv_rank, s), lambda bi, j: (bi, 0, 0)),
            full(wiq_t.shape), full(wuq_t.shape), full(wuk_h.shape), full(wuv_h.shape),
            full(bias_tab.shape),
        ],
        out_specs=pl.BlockSpec((1, tq, nh * dh), lambda bi, j: (bi, j, 0)),
        out_shape=jax.ShapeDtypeStruct((b, s, nh * dh), F32),
        scratch_shapes=[
            pltpu.VMEM((s, tq), jnp.int32),
            pltpu.VMEM((kv_rank, nh * tq), F32),
        ],
        compiler_params=_cparams(("parallel", "arbitrary")),
        name="dsa",
    )(cq, widx_t, kidx, ckv, ckv_t, wiq_t, wuq_t, wuk_h, wuv_h, bias_tab)


def _t5_bucket_np(n, n_buckets):
    max_exact = n_buckets // 2
    nf = np.maximum(n, 1).astype(np.float32)
    large = max_exact + (np.log(nf / max_exact) / math.log(REL_MAX_DIST / max_exact)
                         * (n_buckets - max_exact)).astype(np.int32)
    large = np.minimum(large, n_buckets - 1)
    return np.where(n < max_exact, n, large)


def _bias_table(rel_bias, tq):
    n_buckets, nh = rel_bias.shape
    assert _t5_bucket_np(np.array([tq + 1]), n_buckets)[0] == n_buckets - 1
    s_loc = np.arange(tq)[:, None]
    t_loc = np.arange(tq)[None, :]
    tabs = []
    for d in range(3):
        dist = np.maximum(d * tq + t_loc - s_loc, 0)
        bucket = _t5_bucket_np(dist, n_buckets)
        tb = rel_bias[bucket]
        tabs.append(jnp.transpose(tb, (0, 2, 1)).reshape(tq, nh * tq))
    return jnp.stack(tabs).astype(F32)


def _merge_kernel(x_ref, ga_ref, gb_ref, oa_ref, ob_ref, wo_ref, o_ref):
    merged = jax.nn.sigmoid(ga_ref[...]) * oa_ref[...] + jax.nn.sigmoid(gb_ref[...]) * ob_ref[...]
    o_ref[...] = x_ref[...] + _dot(merged.astype(BF16), wo_ref[...])


def _merge(x2d, big2d, o_dn, o_sa, w_o, *, gate_blk, tm=512):
    n, d = x2d.shape
    return pl.pallas_call(
        _merge_kernel,
        grid=(n // tm,),
        in_specs=[
            pl.BlockSpec((tm, d), lambda i: (i, 0)),
            pl.BlockSpec((tm, d), lambda i: (i, gate_blk)),
            pl.BlockSpec((tm, d), lambda i: (i, gate_blk + 1)),
            pl.BlockSpec((tm, d), lambda i: (i, 0)),
            pl.BlockSpec((tm, d), lambda i: (i, 0)),
            pl.BlockSpec((d, d), lambda i: (0, 0)),
        ],
        out_specs=pl.BlockSpec((tm, d), lambda i: (i, 0)),
        out_shape=jax.ShapeDtypeStruct((n, d), F32),
        compiler_params=_cparams(("parallel",)),
        name="merge",
    )(x2d, big2d, big2d, o_dn, o_sa, w_o.astype(BF16))


def kernel(x, ffn1_norm, ffn1_wg, ffn1_wu, ffn1_wd, mix_norm, w_in, conv_w, a_log, dt_bias, dn_out_norm, q_norm, kv_norm, w_uq, w_uk, w_uv, w_iq, idx_k_g, idx_k_b, rel_bias, w_o, ffn2_norm, ffn2_wg, ffn2_wu, ffn2_wd, final_norm):
    b, s, d = x.shape
    depth = ffn1_norm.shape[0]
    nh_dn = a_log.shape[1]
    dk = dn_out_norm.shape[1]
    qk_w = nh_dn * dk
    q_rank = q_norm.shape[1]
    kv_rank = kv_norm.shape[1]
    idx_dim = idx_k_g.shape[1]
    nh_idx = w_iq.shape[2] // idx_dim
    nh_sa, dh_sa = w_uk.shape[2], w_uk.shape[3]
    assert qk_w == d and nh_sa * dh_sa == d and nh_idx == nh_dn
    topk = min(TOPK_MAX, s // 4)
    n = b * s

    sizes = (3 * qk_w, qk_w, nh_dn, nh_dn, q_rank, kv_rank, idx_dim, nh_idx, d, d)
    offs = np.concatenate([[0], np.cumsum(sizes)])
    (o_qkv, o_z, o_b, o_a, o_cq, o_ckv, o_ik, o_iw, o_ga, o_gb) = [int(v) for v in offs[:-1]]
    l_cq, l_ckv, l_ik, l_misc, small_cols = _small_layout(q_rank, kv_rank, idx_dim)
    bias_tab = _bias_table(rel_bias, LANE)

    x2d = x.reshape(n, d)
    for l in range(depth):
        x1 = _ffn(x2d, ffn1_norm[l], ffn1_wg[l], ffn1_wu[l], ffn1_wd[l])

        wl = w_in[l]
        w_big = jnp.concatenate([wl[:, o_qkv:o_qkv + 4 * qk_w], wl[:, o_ga:o_ga + 2 * d]], axis=1)
        big2d = _norm_matmul(x1, mix_norm[l], w_big)

        w_small = jnp.zeros((d, small_cols), F32)
        w_small = w_small.at[:, l_cq:l_cq + q_rank].set(wl[:, o_cq:o_cq + q_rank])
        w_small = w_small.at[:, l_ckv:l_ckv + kv_rank].set(wl[:, o_ckv:o_ckv + kv_rank])
        w_small = w_small.at[:, l_ik:l_ik + idx_dim].set(wl[:, o_ik:o_ik + idx_dim])
        w_small = w_small.at[:, l_misc:l_misc + nh_idx].set(wl[:, o_iw:o_iw + nh_idx])
        w_small = w_small.at[:, l_misc + nh_dn:l_misc + 2 * nh_dn].set(wl[:, o_b:o_b + nh_dn])
        w_small = w_small.at[:, l_misc + 2 * nh_dn:l_misc + 3 * nh_dn].set(wl[:, o_a:o_a + nh_dn])
        neg_a_pad = jnp.zeros((1, LANE), F32).at[0, 2 * nh_dn:3 * nh_dn].set(-jnp.exp(a_log[l].astype(F32)))
        dtb_pad = jnp.zeros((1, LANE), F32).at[0, 2 * nh_dn:3 * nh_dn].set(dt_bias[l].astype(F32))
        cq, ckv, kidx, misc = _small_proj(
            x1, mix_norm[l], w_small.astype(BF16), q_norm[l], kv_norm[l], idx_k_g[l], idx_k_b[l],
            neg_a_pad, dtb_pad, q_rank=q_rank, kv_rank=kv_rank, idx_dim=idx_dim, nh=nh_dn,
            w_scale=(nh_idx ** -0.5) * (idx_dim ** -0.5))

        misc3 = misc.reshape(b, s, LANE)
        g_row = jnp.transpose(misc3[:, :, 2 * nh_dn:3 * nh_dn].reshape(b, s // DN_CHUNK, DN_CHUNK, nh_dn),
                              (0, 1, 3, 2))
        o_dn = _deltanet(big2d.reshape(b, s, 6 * d), conv_w[l], misc3, g_row, dn_out_norm[l],
                         nh=nh_dn, dk=dk)

        widx_t = jnp.transpose(misc3[:, :, 0:nh_idx], (0, 2, 1))
        ckv3 = ckv.reshape(b, s, kv_rank)
        o_sa = _dsa(cq.reshape(b, s, q_rank), widx_t, kidx.reshape(b, s, idx_dim), ckv3,
                    jnp.transpose(ckv3, (0, 2, 1)),
                    w_iq[l].T.astype(BF16), w_uq[l].T.astype(BF16),
                    jnp.transpose(w_uk[l], (1, 0, 2)).astype(BF16),
                    jnp.transpose(w_uv[l], (1, 2, 0)).astype(BF16),
                    bias_tab, topk=topk, scale=dh_sa ** -0.5)

        x2 = _merge(x1, big2d, o_dn.reshape(n, d), o_sa.reshape(n, d), w_o[l], gate_blk=4)
        last = l == depth - 1
        x2d = _ffn(x2, ffn2_norm[l], ffn2_wg[l], ffn2_wu[l], ffn2_wd[l],
                   final_norm if last else None)
    if depth == 0:
        raise ValueError("depth must be positive")
    return x2d.reshape(b, s, d)
```

```python
import functools
import math

import numpy as np
import jax
import jax.numpy as jnp
from jax import lax
from jax.experimental import pallas as pl
from jax.experimental.pallas import tpu as pltpu

F32 = jnp.float32
BF16 = jnp.bfloat16
EPS = 1e-6
TOPK_MAX = 256
REL_MAX_DIST = 128
DN_CHUNK = 64
LANE = 128
INT_MIN = -(2 ** 31)
NEG_BIG = -1e30
VMEM_LIMIT = 48 * 1024 * 1024


def _cparams(sem):
    return pltpu.CompilerParams(dimension_semantics=sem, vmem_limit_bytes=VMEM_LIMIT)


def _rms(x, g):
    return x * lax.rsqrt(jnp.mean(x * x, axis=-1, keepdims=True) + EPS) * g


def _silu(x):
    return x * jax.nn.sigmoid(x)


def _dot(a, b):
    return jnp.dot(a, b, preferred_element_type=F32)


def _dot_nt(a, b):
    return lax.dot_general(a, b, (((1,), (1,)), ((), ())), preferred_element_type=F32)


def _dot_tn(a, b):
    return lax.dot_general(a, b, (((0,), (0,)), ((), ())), preferred_element_type=F32)


def _bdot(a, b):
    return lax.dot_general(a, b, (((2,), (1,)), ((0,), (0,))), preferred_element_type=F32)


def _bdot_nt(a, b):
    return lax.dot_general(a, b, (((2,), (2,)), ((0,), (0,))), preferred_element_type=F32)


def _bdot_tn(a, b):
    return lax.dot_general(a, b, (((1,), (1,)), ((0,), (0,))), preferred_element_type=F32)


def _ffn_kernel(x_ref, g_ref, wg_ref, wu_ref, wd_ref, *rest, final):
    if final:
        fg_ref, o_ref, h_ref, acc_ref = rest
    else:
        o_ref, h_ref, acc_ref = rest
    j = pl.program_id(1)

    @pl.when(j == 0)
    def _():
        h_ref[...] = _rms(x_ref[...], g_ref[...]).astype(BF16)
        acc_ref[...] = jnp.zeros_like(acc_ref)

    h = h_ref[...]
    a = _dot(h, wg_ref[...])
    u = _dot(h, wu_ref[...])
    act = (_silu(a) * u).astype(BF16)
    acc_ref[...] += _dot(act, wd_ref[...])

    @pl.when(j == pl.num_programs(1) - 1)
    def _():
        y = x_ref[...] + 0.5 * acc_ref[...]
        if final:
            y = _rms(y, fg_ref[...])
        o_ref[...] = y


def _ffn(x2d, g, wg, wu, wd, final_g=None, *, tm=512, tf=256):
    n, d = x2d.shape
    ff = wg.shape[1]
    final = final_g is not None
    in_specs = [
        pl.BlockSpec((tm, d), lambda i, j: (i, 0)),
        pl.BlockSpec((1, d), lambda i, j: (0, 0)),
        pl.BlockSpec((d, tf), lambda i, j: (0, j)),
        pl.BlockSpec((d, tf), lambda i, j: (0, j)),
        pl.BlockSpec((tf, d), lambda i, j: (j, 0)),
    ]
    args = [x2d, g.reshape(1, d), wg.astype(BF16), wu.astype(BF16), wd.astype(BF16)]
    if final:
        in_specs.append(pl.BlockSpec((1, d), lambda i, j: (0, 0)))
        args.append(final_g.reshape(1, d))
    return pl.pallas_call(
        functools.partial(_ffn_kernel, final=final),
        grid=(n // tm, ff // tf),
        in_specs=in_specs,
        out_specs=pl.BlockSpec((tm, d), lambda i, j: (i, 0)),
        out_shape=jax.ShapeDtypeStruct((n, d), F32),
        scratch_shapes=[pltpu.VMEM((tm, d), BF16), pltpu.VMEM((tm, d), F32)],
        compiler_params=_cparams(("parallel", "arbitrary")),
        name="ffn_final" if final else "ffn",
    )(*args)


def _nmm_kernel(x_ref, g_ref, w_ref, o_ref, h_ref):
    @pl.when(pl.program_id(1) == 0)
    def _():
        h_ref[...] = _rms(x_ref[...], g_ref[...]).astype(BF16)

    o_ref[...] = _dot(h_ref[...], w_ref[...])


def _norm_matmul(x2d, g, w, *, tm=512, tn=512):
    n, d = x2d.shape
    cols = w.shape[1]
    return pl.pallas_call(
        _nmm_kernel,
        grid=(n // tm, cols // tn),
        in_specs=[
            pl.BlockSpec((tm, d), lambda i, j: (i, 0)),
            pl.BlockSpec((1, d), lambda i, j: (0, 0)),
            pl.BlockSpec((d, tn), lambda i, j: (0, j)),
        ],
        out_specs=pl.BlockSpec((tm, tn), lambda i, j: (i, j)),
        out_shape=jax.ShapeDtypeStruct((n, cols), F32),
        scratch_shapes=[pltpu.VMEM((tm, d), BF16)],
        compiler_params=_cparams(("parallel", "arbitrary")),
        name="in_proj_wide",
    )(x2d, g.reshape(1, d), w.astype(BF16))


def _small_layout(q_rank, kv_rank, idx_dim):
    o_cq = 0
    o_ckv = o_cq + q_rank
    o_ik = o_ckv + kv_rank
    o_misc = o_ik + LANE * ((idx_dim + LANE - 1) // LANE)
    return o_cq, o_ckv, o_ik, o_misc, o_misc + LANE


def _small_kernel(x_ref, g_ref, w_ref, qn_ref, kvn_ref, ikg_ref, ikb_ref, negA_ref, dtb_ref,
                  cq_ref, ckv_ref, ik_ref, misc_ref, *, q_rank, kv_rank, idx_dim, nh, w_scale):
    o_cq, o_ckv, o_ik, o_misc, _ = _small_layout(q_rank, kv_rank, idx_dim)
    h = _rms(x_ref[...], g_ref[...]).astype(BF16)
    p = _dot(h, w_ref[...])
    cq_ref[...] = _rms(p[:, o_cq:o_cq + q_rank], qn_ref[...]).astype(cq_ref.dtype)
    ckv_ref[...] = _rms(p[:, o_ckv:o_ckv + kv_rank], kvn_ref[...]).astype(ckv_ref.dtype)
    ik = p[:, o_ik:o_ik + idx_dim]
    mu = jnp.mean(ik, axis=-1, keepdims=True)
    var = jnp.mean(jnp.square(ik - mu), axis=-1, keepdims=True)
    ik_ref[...] = ((ik - mu) * lax.rsqrt(var + EPS) * ikg_ref[...] + ikb_ref[...]).astype(ik_ref.dtype)
    m = p[:, o_misc:o_misc + LANE]
    lane = lax.broadcasted_iota(jnp.int32, m.shape, 1)
    beta = jax.nn.sigmoid(m)
    sp_in = m + dtb_ref[...]
    softplus = jnp.maximum(sp_in, 0.0) + jnp.log(1.0 + jnp.exp(-jnp.abs(sp_in)))
    gdec = negA_ref[...] * softplus
    misc_ref[...] = jnp.where(lane < nh, m * w_scale,
                              jnp.where(lane < 2 * nh, beta,
                                        jnp.where(lane < 3 * nh, gdec, 0.0)))


def _small_proj(x2d, g, w_small, q_norm, kv_norm, ik_g, ik_b, neg_a_pad, dtb_pad, *,
                q_rank, kv_rank, idx_dim, nh, w_scale, tm=512):
    n, d = x2d.shape
    cols = w_small.shape[1]
    full = lambda shape: pl.BlockSpec(shape, lambda i: (0,) * len(shape))
    return pl.pallas_call(
        functools.partial(_small_kernel, q_rank=q_rank, kv_rank=kv_rank, idx_dim=idx_dim, nh=nh,
                          w_scale=w_scale),
        grid=(n // tm,),
        in_specs=[
            pl.BlockSpec((tm, d), lambda i: (i, 0)),
            full((1, d)), full((d, cols)), full((1, q_rank)), full((1, kv_rank)),
            full((1, idx_dim)), full((1, idx_dim)), full((1, LANE)), full((1, LANE)),
        ],
        out_specs=[
            pl.BlockSpec((tm, q_rank), lambda i: (i, 0)),
            pl.BlockSpec((tm, kv_rank), lambda i: (i, 0)),
            pl.BlockSpec((tm, idx_dim), lambda i: (i, 0)),
            pl.BlockSpec((tm, LANE), lambda i: (i, 0)),
        ],
        out_shape=[
            jax.ShapeDtypeStruct((n, q_rank), BF16),
            jax.ShapeDtypeStruct((n, kv_rank), BF16),
            jax.ShapeDtypeStruct((n, idx_dim), BF16),
            jax.ShapeDtypeStruct((n, LANE), F32),
        ],
        compiler_params=_cparams(("parallel",)),
        name="in_proj_small",
    )(x2d, g.reshape(1, d), w_small, q_norm.reshape(1, -1), kv_norm.reshape(1, -1),
      ik_g.reshape(1, -1), ik_b.reshape(1, -1), neg_a_pad, dtb_pad)


def _tri_inverse(lmat, masks, eye):
    x = eye - lmat * masks[0]
    for mk in masks[1:]:
        lm = (lmat * mk).astype(BF16)
        t = _bdot(x.astype(BF16), lm)
        x = x - _bdot(t.astype(BF16), x.astype(BF16))
    return x


def _deltanet_kernel(q_ref, k_ref, v_ref, z_ref, cw_ref, bg_ref, gr_ref, on_ref, o_ref,
                     xpad_ref, act_ref, s_ref, *, nh, dk, ts, conv_k, cpi):
    c = DN_CHUNK
    i = pl.program_id(1)
    qk_w = nh * dk
    halo = 8

    @pl.when(i == 0)
    def _():
        xpad_ref[0:halo, :] = jnp.zeros((halo, 3 * qk_w), F32)
        s_ref[...] = jnp.zeros_like(s_ref)

    xpad_ref[halo:halo + ts, 0:qk_w] = q_ref[0]
    xpad_ref[halo:halo + ts, qk_w:2 * qk_w] = k_ref[0]
    xpad_ref[halo:halo + ts, 2 * qk_w:3 * qk_w] = v_ref[0]
    y = xpad_ref[halo:halo + ts, :] * cw_ref[conv_k - 1:conv_k, :]
    for dlt in range(1, conv_k):
        y = y + xpad_ref[halo - dlt:halo - dlt + ts, :] * cw_ref[conv_k - 1 - dlt:conv_k - dlt, :]
    act_ref[...] = _silu(y)
    xpad_ref[0:halo, :] = xpad_ref[ts:ts + halo, :]

    row = lax.broadcasted_iota(jnp.int32, (c, c), 0)
    col = lax.broadcasted_iota(jnp.int32, (c, c), 1)
    tril = row >= col
    strict = row > col
    eye = jnp.where(row == col, 1.0, 0.0).astype(F32)
    tril_f = jnp.where(tril, 1.0, 0.0).astype(F32)
    masks = []
    m = 1
    while m < c:
        mk = strict & ((row // (2 * m)) == (col // (2 * m))) & ((row // m) != (col // m))
        masks.append(jnp.where(mk, 1.0, 0.0).astype(F32))
        m *= 2
    onorm = on_ref[...]

    def heads(ref, r0, base):
        return [ref[pl.ds(r0, c), base + h * dk:base + (h + 1) * dk] for h in range(nh)]

    def group_body(it, carry):
        qs, ks, vs, betas, gccs, gcrs, glasts, r0s = [], [], [], [], [], [], [], []
        for cc in range(cpi):
            r0 = pl.multiple_of((it * cpi + cc) * c, c)
            r0s.append(r0)
            bg = bg_ref[0, pl.ds(r0, c), :]
            grow = gr_ref[0, it * cpi + cc]
            gc_col = jnp.dot(tril_f, bg, precision=lax.Precision.HIGHEST,
                             preferred_element_type=F32)
            gc_row = lax.dot_general(grow, tril_f, (((1,), (1,)), ((), ())),
                                     precision=lax.Precision.HIGHEST,
                                     preferred_element_type=F32)
            qs += heads(act_ref, r0, 0)
            ks += heads(act_ref, r0, qk_w)
            vs += heads(act_ref, r0, 2 * qk_w)
            betas += [bg[:, nh + h:nh + h + 1] for h in range(nh)]
            gccs += [gc_col[:, 2 * nh + h:2 * nh + h + 1] for h in range(nh)]
            gcrs += [gc_row[h:h + 1, :] for h in range(nh)]
            glasts += [gc_row[h:h + 1, c - 1:c] for h in range(nh)]
        q = jnp.stack(qs)
        k = jnp.stack(ks)
        v = jnp.stack(vs)
        beta = jnp.stack(betas)
        gcc = jnp.stack(gccs)
        gcr = jnp.stack(gcrs)
        g_last = jnp.stack(glasts)
        q = q * (lax.rsqrt(jnp.sum(q * q, axis=-1, keepdims=True) + EPS) * (dk ** -0.5))
        k = k * lax.rsqrt(jnp.sum(k * k, axis=-1, keepdims=True) + EPS)
        decay = jnp.where(tril, jnp.exp(jnp.minimum(gcc - gcr, 0.0)), 0.0)
        eg = jnp.exp(gcc)
        kb = k * beta
        vb = v * beta
        aq = _bdot_nt(jnp.concatenate([kb, q], axis=1).astype(BF16), k.astype(BF16))
        lmat = jnp.where(strict, aq[:, :c] * decay, 0.0)
        intra = (aq[:, c:] * decay).astype(BF16)
        tinv = _tri_inverse(lmat, masks, eye)
        uw = _bdot(tinv.astype(BF16), jnp.concatenate([vb, kb * eg], axis=2).astype(BF16))
        u = uw[:, :, :dk]
        wq = jnp.concatenate([uw[:, :, dk:], q * eg], axis=1).astype(BF16)
        kd = (k * jnp.exp(g_last - gcc)).astype(BF16)
        sdec = jnp.exp(g_last)
        s = s_ref[...]
        for cc in range(cpi):
            sl = slice(cc * nh, (cc + 1) * nh)
            m1 = _bdot(wq[sl], s.astype(BF16))
            vnb = (u[sl] - m1[:, :c]).astype(BF16)
            o = _rms(m1[:, c:] + _bdot(intra[sl], vnb), onorm)
            s = s * sdec[sl] + _bdot_tn(kd[sl], vnb)
            for h in range(nh):
                zz = z_ref[0, pl.ds(r0s[cc], c), h * dk:(h + 1) * dk]
                o_ref[0, pl.ds(r0s[cc], c), h * dk:(h + 1) * dk] = o[h] * _silu(zz)
        s_ref[...] = s
        return carry

    lax.fori_loop(0, ts // (c * cpi), group_body, 0)


def _deltanet(big, conv_w, bg, g_row, out_norm, *, nh, dk, ts=256, cpi=4):
    b, s, _ = big.shape
    qk_w = nh * dk
    conv_k = conv_w.shape[0]
    return pl.pallas_call(
        functools.partial(_deltanet_kernel, nh=nh, dk=dk, ts=ts, conv_k=conv_k, cpi=cpi),
        grid=(b, s // ts),
        in_specs=[
            pl.BlockSpec((1, ts, qk_w), lambda bi, i: (bi, i, 0)),
            pl.BlockSpec((1, ts, qk_w), lambda bi, i: (bi, i, 1)),
            pl.BlockSpec((1, ts, qk_w), lambda bi, i: (bi, i, 2)),
            pl.BlockSpec((1, ts, qk_w), lambda bi, i: (bi, i, 3)),
            pl.BlockSpec((conv_k, 3 * qk_w), lambda bi, i: (0, 0)),
            pl.BlockSpec((1, ts, LANE), lambda bi, i: (bi, i, 0)),
            pl.BlockSpec((1, ts // DN_CHUNK, nh, DN_CHUNK), lambda bi, i: (bi, i, 0, 0)),
            pl.BlockSpec((1, dk), lambda bi, i: (0, 0)),
        ],
        out_specs=pl.BlockSpec((1, ts, qk_w), lambda bi, i: (bi, i, 0)),
        out_shape=jax.ShapeDtypeStruct((b, s, qk_w), F32),
        scratch_shapes=[
            pltpu.VMEM((ts + 8, 3 * qk_w), F32),
            pltpu.VMEM((ts, 3 * qk_w), F32),
            pltpu.VMEM((nh, dk, dk), F32),
        ],
        compiler_params=_cparams(("arbitrary", "arbitrary")),
        name="deltanet",
    )(big, big, big, big, conv_w, bg, g_row, out_norm.reshape(1, dk))


def _sortable(x):
    i = pltpu.bitcast(x, jnp.int32)
    return i ^ ((i >> 31) & jnp.int32(0x7FFFFFFF))


def _dsa_kernel(cq_ref, wr_ref, ki_ref, ckv_ref, ckvt_ref, wiq_ref, wuq_ref, wuk_ref, wuv_ref,
                bias_ref, o_ref, key_ref, acc_ref, *, nh_idx, idx_dim, nh, dh, topk, scale):
    tq = LANE
    tk = LANE
    j = pl.program_id(1)
    nkt = j + 1
    cqb = cq_ref[0]
    rows = lax.broadcasted_iota(jnp.int32, (tk, tq), 0)
    cols = lax.broadcasted_iota(jnp.int32, (tk, tq), 1)

    qidx_t = _dot_nt(wiq_ref[...], cqb).astype(BF16)
    wrow = wr_ref[0]

    def score_body(kt, carry):
        s0 = pl.multiple_of(kt * tk, tk)
        kk = ki_ref[0, pl.ds(s0, tk), :]
        sc = jnp.zeros((tk, tq), F32)
        for h in range(nh_idx):
            d = _dot(kk, qidx_t[h * idx_dim:(h + 1) * idx_dim, :])
            sc = sc + wrow[h:h + 1, :] * jnp.maximum(d, 0.0)
        causal = (s0 + rows) <= (j * tq + cols)
        key_ref[pl.ds(s0, tk), :] = jnp.where(causal, _sortable(sc), INT_MIN)
        return carry

    lax.fori_loop(0, nkt, score_body, 0)

    def count_ge(cand):
        def body(kt, acc):
            s0 = pl.multiple_of(kt * tk, tk)
            kv = key_ref[pl.ds(s0, tk), :]
            hit = jnp.where(kv >= cand, 1, 0).astype(jnp.int32)
            return acc + jnp.sum(hit.reshape(tk // 8, 8, tq), axis=0)
        acc = lax.fori_loop(0, nkt, body, jnp.zeros((8, tq), jnp.int32))
        return jnp.sum(acc, axis=0, keepdims=True)

    cnt0 = count_ge(jnp.zeros((1, tq), jnp.int32))
    ans0 = jnp.where(cnt0 >= topk, 0, INT_MIN).astype(jnp.int32)

    def bit_body(bi, ans):
        cand = ans | jnp.left_shift(jnp.int32(1), 30 - bi)
        return jnp.where(count_ge(cand) >= topk, cand, ans)

    ans = lax.fori_loop(0, 31, bit_body, ans0)
    thr = jnp.maximum(ans, INT_MIN + 1)

    cnt_ge = count_ge(thr)
    any_tie = jnp.max(jnp.where(cnt_ge > topk, 1, 0)) > 0

    @pl.when(any_tie)
    def _():
        need = topk - count_ge(thr + 1)

        def count_eq_below(lim):
            def body(kt, acc):
                s0 = pl.multiple_of(kt * tk, tk)
                kv = key_ref[pl.ds(s0, tk), :]
                hit = jnp.where((kv == thr) & ((s0 + rows) < lim), 1, 0).astype(jnp.int32)
                return acc + jnp.sum(hit.reshape(tk // 8, 8, tq), axis=0)
            acc = lax.fori_loop(0, nkt, body, jnp.zeros((8, tq), jnp.int32))
            return jnp.sum(acc, axis=0, keepdims=True)

        nbits = int(math.ceil(math.log2(key_ref.shape[0]))) + 1

        def jbit_body(bi, lim):
            cand = lim | jnp.left_shift(jnp.int32(1), nbits - 1 - bi)
            return jnp.where(count_eq_below(cand) <= need, cand, lim)

        lim = lax.fori_loop(0, nbits, jbit_body, jnp.zeros((1, tq), jnp.int32))

        def demote_body(kt, carry):
            s0 = pl.multiple_of(kt * tk, tk)
            kv = key_ref[pl.ds(s0, tk), :]
            drop = (kv == thr) & ((s0 + rows) >= lim)
            key_ref[pl.ds(s0, tk), :] = jnp.where(drop, INT_MIN, kv)
            return carry

        lax.fori_loop(0, nkt, demote_body, 0)

    q_t = _dot_nt(wuq_ref[...], cqb)
    qlat = []
    for h in range(nh):
        ql = _dot(wuk_ref[h], q_t[h * dh:(h + 1) * dh, :].astype(BF16)) * scale
        qlat.append(ql.astype(BF16))
    qlat_t = jnp.concatenate(qlat, axis=1)
    thr_h = jnp.concatenate([thr] * nh, axis=1)
    acc_ref[...] = jnp.zeros_like(acc_ref)

    def attn_body(kt, carry):
        m_run, l_run = carry
        s0 = pl.multiple_of(kt * tk, tk)
        cblk = ckv_ref[0, pl.ds(s0, tk), :]
        lg = _dot(cblk, qlat_t)
        bias = bias_ref[jnp.minimum(j - kt, 2)]
        kv = key_ref[pl.ds(s0, tk), :]
        keys = jnp.concatenate([kv] * nh, axis=1)
        lg = jnp.where(keys >= thr_h, lg + bias, NEG_BIG)
        m_new = jnp.maximum(m_run, jnp.max(lg, axis=0, keepdims=True))
        p = jnp.exp(lg - m_new)
        alpha = jnp.exp(m_run - m_new)
        l_new = l_run * alpha + jnp.sum(p, axis=0, keepdims=True)
        ct = ckvt_ref[0, :, pl.ds(s0, tk)]
        acc_ref[...] = acc_ref[...] * alpha + _dot(ct, p.astype(BF16))
        return m_new, l_new

    m0 = jnp.full((1, nh * tq), NEG_BIG, F32)
    l0 = jnp.zeros((1, nh * tq), F32)
    _, l_fin = lax.fori_loop(0, nkt, attn_body, (m0, l0))
    olat = (acc_ref[...] * (1.0 / l_fin)).astype(BF16)
    for h in range(nh):
        o_t = _dot(wuv_ref[h], olat[:, h * tq:(h + 1) * tq])
        o_ref[0, :, h * dh:(h + 1) * dh] = o_t.T


def _dsa(cq, widx_t, kidx, ckv, ckv_t, wiq_t, wuq_t, wuk_h, wuv_h, bias_tab, *, topk, scale):
    b, s, q_rank = cq.shape
    kv_rank = ckv.shape[-1]
    nh_idx = widx_t.shape[1]
    idx_dim = kidx.shape[-1]
    nh, dh, _ = wuv_h.shape
    tq = LANE
    full = lambda shape: pl.BlockSpec(shape, lambda bi, j: (0,) * len(shape))
    return pl.pallas_call(
        functools.partial(_dsa_kernel, nh_idx=nh_idx, idx_dim=idx_dim, nh=nh, dh=dh, topk=topk,
                          scale=scale),
        grid=(b, s // tq),
        in_specs=[
            pl.BlockSpec((1, tq, q_rank), lambda bi, j: (bi, j, 0)),
            pl.BlockSpec((1, nh_idx, tq), lambda bi, j: (bi, 0, j)),
            pl.BlockSpec((1, s, idx_dim), lambda bi, j: (bi, 0, 0)),
            pl.BlockSpec((1, s, kv_rank), lambda bi, j: (bi, 0, 0)),
            pl.BlockSpec((1, kv_rank, s), lambda bi, j: (bi, 0, 0)),
            full(wiq_t.shape), full(wuq_t.shape), full(wuk_h.shape), full(wuv_h.shape),
            full(bias_tab.shape),
        ],
        out_specs=pl.BlockSpec((1, tq, nh * dh), lambda bi, j: (bi, j, 0)),
        out_shape=jax.ShapeDtypeStruct((b, s, nh * dh), F32),
        scratch_shapes=[
            pltpu.VMEM((s, tq), jnp.int32),
            pltpu.VMEM((kv_rank, nh * tq), F32),
        ],
        compiler_params=_cparams(("parallel", "arbitrary")),
        name="dsa",
    )(cq, widx_t, kidx, ckv, ckv_t, wiq_t, wuq_t, wuk_h, wuv_h, bias_tab)


def _t5_bucket_np(n, n_buckets):
    max_exact = n_buckets // 2
    nf = np.maximum(n, 1).astype(np.float32)
    large = max_exact + (np.log(nf / max_exact) / math.log(REL_MAX_DIST / max_exact)
                         * (n_buckets - max_exact)).astype(np.int32)
    large = np.minimum(large, n_buckets - 1)
    return np.where(n < max_exact, n, large)


def _bias_table(rel_bias, tq):
    n_buckets, nh = rel_bias.shape
    assert _t5_bucket_np(np.array([tq + 1]), n_buckets)[0] == n_buckets - 1
    s_loc = np.arange(tq)[:, None]
    t_loc = np.arange(tq)[None, :]
    tabs = []
    for d in range(3):
        dist = np.maximum(d * tq + t_loc - s_loc, 0)
        bucket = _t5_bucket_np(dist, n_buckets)
        tb = rel_bias[bucket]
        tabs.append(jnp.transpose(tb, (0, 2, 1)).reshape(tq, nh * tq))
    return jnp.stack(tabs).astype(F32)


def _merge_kernel(x_ref, ga_ref, gb_ref, oa_ref, ob_ref, wo_ref, o_ref):
    merged = jax.nn.sigmoid(ga_ref[...]) * oa_ref[...] + jax.nn.sigmoid(gb_ref[...]) * ob_ref[...]
    o_ref[...] = x_ref[...] + _dot(merged.astype(BF16), wo_ref[...])


def _merge(x2d, big2d, o_dn, o_sa, w_o, *, gate_blk, tm=512):
    n, d = x2d.shape
    return pl.pallas_call(
        _merge_kernel,
        grid=(n // tm,),
        in_specs=[
            pl.BlockSpec((tm, d), lambda i: (i, 0)),
            pl.BlockSpec((tm, d), lambda i: (i, gate_blk)),
            pl.BlockSpec((tm, d), lambda i: (i, gate_blk + 1)),
            pl.BlockSpec((tm, d), lambda i: (i, 0)),
            pl.BlockSpec((tm, d), lambda i: (i, 0)),
            pl.BlockSpec((d, d), lambda i: (0, 0)),
        ],
        out_specs=pl.BlockSpec((tm, d), lambda i: (i, 0)),
        out_shape=jax.ShapeDtypeStruct((n, d), F32),
        compiler_params=_cparams(("parallel",)),
        name="merge",
    )(x2d, big2d, big2d, o_dn, o_sa, w_o.astype(BF16))


def kernel(x, ffn1_norm, ffn1_wg, ffn1_wu, ffn1_wd, mix_norm, w_in, conv_w, a_log, dt_bias, dn_out_norm, q_norm, kv_norm, w_uq, w_uk, w_uv, w_iq, idx_k_g, idx_k_b, rel_bias, w_o, ffn2_norm, ffn2_wg, ffn2_wu, ffn2_wd, final_norm):
    b, s, d = x.shape
    depth = ffn1_norm.shape[0]
    nh_dn = a_log.shape[1]
    dk = dn_out_norm.shape[1]
    qk_w = nh_dn * dk
    q_rank = q_norm.shape[1]
    kv_rank = kv_norm.shape[1]
    idx_dim = idx_k_g.shape[1]
    nh_idx = w_iq.shape[2] // idx_dim
    nh_sa, dh_sa = w_uk.shape[2], w_uk.shape[3]
    assert qk_w == d and nh_sa * dh_sa == d and nh_idx == nh_dn
    topk = min(TOPK_MAX, s // 4)
    n = b * s

    sizes = (3 * qk_w, qk_w, nh_dn, nh_dn, q_rank, kv_rank, idx_dim, nh_idx, d, d)
    offs = np.concatenate([[0], np.cumsum(sizes)])
    (o_qkv, o_z, o_b, o_a, o_cq, o_ckv, o_ik, o_iw, o_ga, o_gb) = [int(v) for v in offs[:-1]]
    l_cq, l_ckv, l_ik, l_misc, small_cols = _small_layout(q_rank, kv_rank, idx_dim)
    bias_tab = _bias_table(rel_bias, LANE)

    x2d = x.reshape(n, d)
    for l in range(depth):
        x1 = _ffn(x2d, ffn1_norm[l], ffn1_wg[l], ffn1_wu[l], ffn1_wd[l])

        wl = w_in[l]
        w_big = jnp.concatenate([wl[:, o_qkv:o_qkv + 4 * qk_w], wl[:, o_ga:o_ga + 2 * d]], axis=1)
        big2d = _norm_matmul(x1, mix_norm[l], w_big)

        w_small = jnp.zeros((d, small_cols), F32)
        w_small = w_small.at[:, l_cq:l_cq + q_rank].set(wl[:, o_cq:o_cq + q_rank])
        w_small = w_small.at[:, l_ckv:l_ckv + kv_rank].set(wl[:, o_ckv:o_ckv + kv_rank])
        w_small = w_small.at[:, l_ik:l_ik + idx_dim].set(wl[:, o_ik:o_ik + idx_dim])
        w_small = w_small.at[:, l_misc:l_misc + nh_idx].set(wl[:, o_iw:o_iw + nh_idx])
        w_small = w_small.at[:, l_misc + nh_dn:l_misc + 2 * nh_dn].set(wl[:, o_b:o_b + nh_dn])
        w_small = w_small.at[:, l_misc + 2 * nh_dn:l_misc + 3 * nh_dn].set(wl[:, o_a:o_a + nh_dn])
        neg_a_pad = jnp.zeros((1, LANE), F32).at[0, 2 * nh_dn:3 * nh_dn].set(-jnp.exp(a_log[l].astype(F32)))
        dtb_pad = jnp.zeros((1, LANE), F32).at[0, 2 * nh_dn:3 * nh_dn].set(dt_bias[l].astype(F32))
        cq, ckv, kidx, misc = _small_proj(
            x1, mix_norm[l], w_small.astype(BF16), q_norm[l], kv_norm[l], idx_k_g[l], idx_k_b[l],
            neg_a_pad, dtb_pad, q_rank=q_rank, kv_rank=kv_rank, idx_dim=idx_dim, nh=nh_dn,
            w_scale=(nh_idx ** -0.5) * (idx_dim ** -0.5))

        misc3 = misc.reshape(b, s, LANE)
        g_row = jnp.transpose(misc3[:, :, 2 * nh_dn:3 * nh_dn].reshape(b, s // DN_CHUNK, DN_CHUNK, nh_dn),
                              (0, 1, 3, 2))
        o_dn = _deltanet(big2d.reshape(b, s, 6 * d), conv_w[l], misc3, g_row, dn_out_norm[l],
                         nh=nh_dn, dk=dk)

        widx_t = jnp.transpose(misc3[:, :, 0:nh_idx], (0, 2, 1))
        ckv3 = ckv.reshape(b, s, kv_rank)
        o_sa = _dsa(cq.reshape(b, s, q_rank), widx_t, kidx.reshape(b, s, idx_dim), ckv3,
                    jnp.transpose(ckv3, (0, 2, 1)),
                    w_iq[l].T.astype(BF16), w_uq[l].T.astype(BF16),
                    jnp.transpose(w_uk[l], (1, 0, 2)).astype(BF16),
                    jnp.transpose(w_uv[l], (1, 2, 0)).astype(BF16),
                    bias_tab, topk=topk, scale=dh_sa ** -0.5)

        x2 = _merge(x1, big2d, o_dn.reshape(n, d), o_sa.reshape(n, d), w_o[l], gate_blk=4)
        last = l == depth - 1
        x2d = _ffn(x2, ffn2_norm[l], ffn2_wg[l], ffn2_wu[l], ffn2_wd[l],
                   final_norm if last else None)
    if depth == 0:
        raise ValueError("depth must be positive")
    return x2d.reshape(b, s, d)
```

```python
import functools
import math

import numpy as np
import jax
import jax.numpy as jnp
from jax import lax
from jax.experimental import pallas as pl
from jax.experimental.pallas import tpu as pltpu

F32 = jnp.float32
BF16 = jnp.bfloat16
EPS = 1e-6
TOPK_MAX = 256
REL_MAX_DIST = 128
DN_CHUNK = 64
LANE = 128
MISC_ROWS = 32
INT_MIN = -(2 ** 31)
NEG_BIG = -1e30
LOG2E = math.log2(math.e)
VMEM_LIMIT = 56 * 1024 * 1024


def _cparams(sem):
    return pltpu.CompilerParams(dimension_semantics=sem, vmem_limit_bytes=VMEM_LIMIT)


def _resident(shape):
    return pl.BlockSpec(shape, lambda *_: (0,) * len(shape), pipeline_mode=pl.Buffered(1))


def _rms(x, g):
    return x * lax.rsqrt(jnp.mean(x * x, axis=-1, keepdims=True) + EPS) * g


def _silu(x):
    return x * jax.nn.sigmoid(x)


def _dot(a, b):
    return jnp.dot(a, b, preferred_element_type=F32)


def _dot_nt(a, b):
    return lax.dot_general(a, b, (((1,), (1,)), ((), ())), preferred_element_type=F32)


def _bdot(a, b):
    return lax.dot_general(a, b, (((2,), (1,)), ((0,), (0,))), preferred_element_type=F32)


def _bdot_nt(a, b):
    return lax.dot_general(a, b, (((2,), (2,)), ((0,), (0,))), preferred_element_type=F32)


def _bdot_tn(a, b):
    return lax.dot_general(a, b, (((1,), (1,)), ((0,), (0,))), preferred_element_type=F32)


def _ffn_kernel(*refs, final, merge, ff_chunk):
    refs = list(refs)
    x_ref = refs.pop(0)
    if merge:
        ga_ref, gb_ref, oa_ref, ob_ref, wo_ref = refs[:5]
        refs = refs[5:]
    g_ref, wg_ref, wu_ref, wd_ref = refs[:4]
    refs = refs[4:]
    if final:
        fg_ref = refs.pop(0)
    o_ref, act_ref = refs

    x = x_ref[...]
    if merge:
        merged = (jax.nn.sigmoid(ga_ref[...].astype(F32)) * oa_ref[...].astype(F32)
                  + jax.nn.sigmoid(gb_ref[...].astype(F32)) * ob_ref[...].astype(F32))
        x = x + _dot(merged.astype(BF16), wo_ref[...])
    h = _rms(x, g_ref[...]).astype(BF16)
    ff = wg_ref.shape[1]
    for c0 in range(0, ff, ff_chunk):
        c1 = min(c0 + ff_chunk, ff)
        a = _dot(h, wg_ref[:, c0:c1])
        u = _dot(h, wu_ref[:, c0:c1])
        act_ref[:, c0:c1] = (_silu(a) * u).astype(BF16)
    y = x + 0.5 * _dot(act_ref[...], wd_ref[...])
    if final:
        y = _rms(y, fg_ref[...])
    o_ref[...] = y


def _ffn(x2d, g, wg, wu, wd, *, final_g=None, merge=None, tm=512, ff_chunk=256):
    n, d = x2d.shape
    ff = wg.shape[1]
    row = lambda cols, blk=0: pl.BlockSpec((tm, cols), lambda i, blk=blk: (i, blk))
    in_specs = [row(d)]
    args = [x2d]
    if merge is not None:
        big2d, gate_blk, o_dn, o_sa, w_o = merge
        in_specs += [row(d, gate_blk), row(d, gate_blk + 1), row(d), row(d), _resident((d, d))]
        args += [big2d, big2d, o_dn, o_sa, w_o.astype(BF16)]
    in_specs += [_resident((1, d)), _resident((d, ff)), _resident((d, ff)), _resident((ff, d))]
    args += [g.reshape(1, d), wg.astype(BF16), wu.astype(BF16), wd.astype(BF16)]
    if final_g is not None:
        in_specs.append(_resident((1, d)))
        args.append(final_g.reshape(1, d))
    return pl.pallas_call(
        functools.partial(_ffn_kernel, final=final_g is not None, merge=merge is not None,
                          ff_chunk=ff_chunk),
        grid=(n // tm,),
        in_specs=in_specs,
        out_specs=row(d),
        out_shape=jax.ShapeDtypeStruct((n, d), F32),
        scratch_shapes=[pltpu.VMEM((tm, ff), BF16)],
        compiler_params=_cparams(("parallel",)),
        name="ffn_merge" if merge is not None else "ffn",
    )(*args)


def _small_layout(q_rank, kv_rank, idx_dim):
    o_cq = 0
    o_ckv = o_cq + q_rank
    o_ik = o_ckv + kv_rank
    o_misc = o_ik + LANE * ((idx_dim + LANE - 1) // LANE)
    return o_cq, o_ckv, o_ik, o_misc, o_misc + LANE


def _inproj_kernel(x_ref, g_ref, wb_ref, ws_ref, qn_ref, kvn_ref, ikg_ref, ikb_ref, negA_ref, dtb_ref,
                   big_ref, cq_ref, ckv_ref, ckvt_ref, ik_ref, misc_ref, misct_ref,
                   *, q_rank, kv_rank, idx_dim, nh, w_scale, col_chunk):
    o_cq, o_ckv, o_ik, o_misc, _ = _small_layout(q_rank, kv_rank, idx_dim)
    h = _rms(x_ref[...], g_ref[...]).astype(BF16)
    cols = wb_ref.shape[1]
    for c0 in range(0, cols, col_chunk):
        c1 = min(c0 + col_chunk, cols)
        big_ref[:, c0:c1] = _dot(h, wb_ref[:, c0:c1]).astype(big_ref.dtype)
    p = _dot(h, ws_ref[...])
    cq_ref[...] = _rms(p[:, o_cq:o_cq + q_rank], qn_ref[...]).astype(cq_ref.dtype)
    ckv = _rms(p[:, o_ckv:o_ckv + kv_rank], kvn_ref[...])
    ckv_ref[...] = ckv.astype(ckv_ref.dtype)
    ckvt_ref[...] = ckv.T.astype(ckvt_ref.dtype)
    ik = p[:, o_ik:o_ik + idx_dim]
    mu = jnp.mean(ik, axis=-1, keepdims=True)
    var = jnp.mean(jnp.square(ik - mu), axis=-1, keepdims=True)
    ik_ref[...] = ((ik - mu) * lax.rsqrt(var + EPS) * ikg_ref[...] + ikb_ref[...]).astype(ik_ref.dtype)
    m = p[:, o_misc:o_misc + LANE]
    lane = lax.broadcasted_iota(jnp.int32, m.shape, 1)
    beta = jax.nn.sigmoid(m)
    sp_in = m + dtb_ref[...]
    softplus = jnp.maximum(sp_in, 0.0) + jnp.log(1.0 + jnp.exp(-jnp.abs(sp_in)))
    gdec = negA_ref[...] * softplus
    misc = jnp.where(lane < nh, m * w_scale,
                     jnp.where(lane < 2 * nh, beta, jnp.where(lane < 3 * nh, gdec, 0.0)))
    misc_ref[...] = misc
    misct_ref[...] = misc.T[0:MISC_ROWS, :]


def _in_proj(x2d, g, w_big, w_small, q_norm, kv_norm, ik_g, ik_b, neg_a_pad, dtb_pad, *,
             q_rank, kv_rank, idx_dim, nh, w_scale, tm=512, col_chunk=512):
    n, d = x2d.shape
    cols = w_big.shape[1]
    row = lambda c: pl.BlockSpec((tm, c), lambda i: (i, 0))
    colblk = lambda r: pl.BlockSpec((r, tm), lambda i: (0, i))
    return pl.pallas_call(
        functools.partial(_inproj_kernel, q_rank=q_rank, kv_rank=kv_rank, idx_dim=idx_dim, nh=nh,
                          w_scale=w_scale, col_chunk=col_chunk),
        grid=(n // tm,),
        in_specs=[
            row(d), _resident((1, d)), _resident((d, cols)), _resident((d, w_small.shape[1])),
            _resident((1, q_rank)), _resident((1, kv_rank)), _resident((1, idx_dim)),
            _resident((1, idx_dim)), _resident((1, LANE)), _resident((1, LANE)),
        ],
        out_specs=[row(cols), row(q_rank), row(kv_rank), colblk(kv_rank), row(idx_dim), row(LANE),
                   colblk(MISC_ROWS)],
        out_shape=[
            jax.ShapeDtypeStruct((n, cols), BF16),
            jax.ShapeDtypeStruct((n, q_rank), BF16),
            jax.ShapeDtypeStruct((n, kv_rank), BF16),
            jax.ShapeDtypeStruct((kv_rank, n), BF16),
            jax.ShapeDtypeStruct((n, idx_dim), BF16),
            jax.ShapeDtypeStruct((n, LANE), F32),
            jax.ShapeDtypeStruct((MISC_ROWS, n), F32),
        ],
        compiler_params=_cparams(("parallel",)),
        name="in_proj",
    )(x2d, g.reshape(1, d), w_big, w_small, q_norm.reshape(1, -1), kv_norm.reshape(1, -1),
      ik_g.reshape(1, -1), ik_b.reshape(1, -1), neg_a_pad, dtb_pad)


def _tri_inverse(lmat, masks, eye):
    x = eye - lmat * masks[0]
    for mk in masks[1:]:
        lm = (lmat * mk).astype(BF16)
        t = _bdot(x.astype(BF16), lm)
        x = x - _bdot(t.astype(BF16), x.astype(BF16))
    return x


def _deltanet_kernel(q_ref, k_ref, v_ref, z_ref, cw_ref, bg_ref, mt_ref, on_ref, o_ref,
                     xpad_ref, act_ref, s_ref, *, nh, dk, ts, conv_k):
    c = DN_CHUNK
    cpi = ts // c
    i = pl.program_id(1)
    qk_w = nh * dk
    halo = 8

    @pl.when(i == 0)
    def _():
        xpad_ref[0:halo, :] = jnp.zeros((halo, 3 * qk_w), F32)
        s_ref[...] = jnp.zeros_like(s_ref)

    xpad_ref[halo:halo + ts, 0:qk_w] = q_ref[...].astype(F32)
    xpad_ref[halo:halo + ts, qk_w:2 * qk_w] = k_ref[...].astype(F32)
    xpad_ref[halo:halo + ts, 2 * qk_w:3 * qk_w] = v_ref[...].astype(F32)
    y = xpad_ref[halo:halo + ts, :] * cw_ref[conv_k - 1:conv_k, :]
    for dlt in range(1, conv_k):
        y = y + xpad_ref[halo - dlt:halo - dlt + ts, :] * cw_ref[conv_k - 1 - dlt:conv_k - dlt, :]
    act_ref[...] = _silu(y)
    xpad_ref[0:halo, :] = xpad_ref[ts:ts + halo, :]

    row = lax.broadcasted_iota(jnp.int32, (c, c), 0)
    col = lax.broadcasted_iota(jnp.int32, (c, c), 1)
    tril = row >= col
    strict = row > col
    eye = jnp.where(row == col, 1.0, 0.0).astype(F32)
    tril_f = jnp.where(tril, 1.0, 0.0).astype(F32)
    masks = []
    m = 1
    while m < c:
        mk = strict & ((row // (2 * m)) == (col // (2 * m))) & ((row // m) != (col // m))
        masks.append(jnp.where(mk, 1.0, 0.0).astype(F32))
        m *= 2
    onorm = on_ref[...]

    def heads(ref, r0, base):
        return [ref[r0:r0 + c, base + h * dk:base + (h + 1) * dk] for h in range(nh)]

    qs, ks, vs, betas, gccs, gcrs, glasts = [], [], [], [], [], [], []
    for cc in range(cpi):
        r0 = cc * c
        bg = bg_ref[r0:r0 + c, :]
        grow = mt_ref[2 * nh:3 * nh, r0:r0 + c]
        gc_col = jnp.dot(tril_f, bg, precision=lax.Precision.HIGHEST,
                         preferred_element_type=F32)
        gc_row = lax.dot_general(grow, tril_f, (((1,), (1,)), ((), ())),
                                 precision=lax.Precision.HIGHEST,
                                 preferred_element_type=F32)
        qs += heads(act_ref, r0, 0)
        ks += heads(act_ref, r0, qk_w)
        vs += heads(act_ref, r0, 2 * qk_w)
        betas += [bg[:, nh + h:nh + h + 1] for h in range(nh)]
        gccs += [gc_col[:, 2 * nh + h:2 * nh + h + 1] for h in range(nh)]
        gcrs += [gc_row[h:h + 1, :] for h in range(nh)]
        glasts += [gc_row[h:h + 1, c - 1:c] for h in range(nh)]
    q = jnp.stack(qs)
    k = jnp.stack(ks)
    v = jnp.stack(vs)
    beta = jnp.stack(betas)
    gcc = jnp.stack(gccs)
    gcr = jnp.stack(gcrs)
    g_last = jnp.stack(glasts)
    q = q * (lax.rsqrt(jnp.sum(q * q, axis=-1, keepdims=True) + EPS) * (dk ** -0.5))
    k = k * lax.rsqrt(jnp.sum(k * k, axis=-1, keepdims=True) + EPS)
    decay = jnp.where(tril, jnp.exp(jnp.minimum(gcc - gcr, 0.0)), 0.0)
    eg = jnp.exp(gcc)
    kb = k * beta
    vb = v * beta
    aq = _bdot_nt(jnp.concatenate([kb, q], axis=1).astype(BF16), k.astype(BF16))
    lmat = jnp.where(strict, aq[:, :c] * decay, 0.0)
    intra = (aq[:, c:] * decay).astype(BF16)
    tinv = _tri_inverse(lmat, masks, eye)
    uw = _bdot(tinv.astype(BF16), jnp.concatenate([vb, kb * eg], axis=2).astype(BF16))
    u = uw[:, :, :dk]
    wq = jnp.concatenate([uw[:, :, dk:], q * eg], axis=1).astype(BF16)
    kd = (k * jnp.exp(g_last - gcc)).astype(BF16)
    sdec = jnp.exp(g_last)
    s = s_ref[...]
    for cc in range(cpi):
        sl = slice(cc * nh, (cc + 1) * nh)
        m1 = _bdot(wq[sl], s.astype(BF16))
        vnb = (u[sl] - m1[:, :c]).astype(BF16)
        o = _rms(m1[:, c:] + _bdot(intra[sl], vnb), onorm)
        s = s * sdec[sl] + _bdot_tn(kd[sl], vnb)
        for h in range(nh):
            zz = z_ref[cc * c:(cc + 1) * c, h * dk:(h + 1) * dk].astype(F32)
            o_ref[cc * c:(cc + 1) * c, h * dk:(h + 1) * dk] = (o[h] * _silu(zz)).astype(o_ref.dtype)
    s_ref[...] = s


def _deltanet(big2d, conv_w, misc, misc_t, out_norm, *, batch, nh, dk, ts=256):
    n = big2d.shape[0]
    steps = n // batch // ts
    qk_w = nh * dk
    conv_k = conv_w.shape[0]
    tok = lambda cols, blk: pl.BlockSpec((ts, cols), lambda bi, i, blk=blk: (bi * steps + i, blk))
    return pl.pallas_call(
        functools.partial(_deltanet_kernel, nh=nh, dk=dk, ts=ts, conv_k=conv_k),
        grid=(batch, steps),
        in_specs=[
            tok(qk_w, 0), tok(qk_w, 1), tok(qk_w, 2), tok(qk_w, 3),
            _resident((conv_k, 3 * qk_w)),
            tok(LANE, 0),
            pl.BlockSpec((MISC_ROWS, ts), lambda bi, i: (0, bi * steps + i)),
            _resident((1, dk)),
        ],
        out_specs=tok(qk_w, 0),
        out_shape=jax.ShapeDtypeStruct((n, qk_w), BF16),
        scratch_shapes=[
            pltpu.VMEM((ts + 8, 3 * qk_w), F32),
            pltpu.VMEM((ts, 3 * qk_w), F32),
            pltpu.VMEM((nh, dk, dk), F32),
        ],
        compiler_params=_cparams(("arbitrary", "arbitrary")),
        name="deltanet",
    )(big2d, big2d, big2d, big2d, conv_w, misc, misc_t, out_norm.reshape(1, dk))


def _sortable(x):
    i = pltpu.bitcast(x, jnp.int32)
    return i ^ ((i >> 31) & jnp.int32(0x7FFFFFFF))


def _dsa_kernel(cq_ref, mt_ref, ki_ref, ckv_ref, ckvt_ref, wiq_ref, wuq_ref, wuk_ref, wuv_ref,
                bias_ref, o_ref, key_ref, acc_ref, lg0_ref, lg1_ref,
                *, nh_idx, idx_dim, nh, dh, topk, scale, tks, tka):
    tq = LANE
    sub = LANE
    j = pl.program_id(1)
    n_s = (j * tq + tq + tks - 1) // tks
    n_a = (j * tq + tq + tka - 1) // tka
    cqb = cq_ref[...]
    rows = lax.broadcasted_iota(jnp.int32, (sub, tq), 0)
    qpos = j * tq + lax.broadcasted_iota(jnp.int32, (sub, tq), 1)
    rows_s = lax.broadcasted_iota(jnp.int32, (tks, tq), 0)

    qidx_t = _dot_nt(wiq_ref[...], cqb).astype(BF16)
    qidx_all = jnp.concatenate([qidx_t[h * idx_dim:(h + 1) * idx_dim, :] for h in range(nh_idx)],
                               axis=1)
    wrow_all = jnp.concatenate([mt_ref[h:h + 1, :] for h in range(nh_idx)], axis=1)

    def score_body(kt, carry):
        for r in range(tks // sub):
            s0 = pl.multiple_of(kt * tks + r * sub, sub)
            kk = ki_ref[pl.ds(s0, sub), :]
            d = jnp.maximum(_dot(kk, qidx_all), 0.0) * wrow_all
            parts = [d[:, h * tq:(h + 1) * tq] for h in range(nh_idx)]
            while len(parts) > 1:
                parts = [parts[a] + parts[a + 1] for a in range(0, len(parts), 2)]
            causal = (s0 + rows) <= qpos
            key_ref[pl.ds(s0, sub), :] = jnp.where(causal, _sortable(parts[0]), INT_MIN)
        return carry

    lax.fori_loop(0, n_s, score_body, 0)

    def count_tiles(hit_fn):
        def body(kt, acc):
            s0 = pl.multiple_of(kt * tks, tks)
            hit = hit_fn(key_ref[pl.ds(s0, tks), :], s0)
            return acc + jnp.sum(hit.reshape(tks // 32, 4, 8, tq), axis=0)
        acc = lax.fori_loop(0, n_s, body, jnp.zeros((4, 8, tq), jnp.int32))
        return jnp.sum(acc.reshape(32, tq), axis=0, keepdims=True)

    def count_ge(cand):
        return count_tiles(lambda kv, s0: jnp.where(kv >= cand, 1, 0).astype(jnp.int32))

    cnt0 = count_ge(jnp.zeros((1, tq), jnp.int32))
    ans0 = jnp.where(cnt0 >= topk, 0, INT_MIN).astype(jnp.int32)

    def bit_body(bi, ans):
        cand = ans | jnp.left_shift(jnp.int32(1), 30 - bi)
        return jnp.where(count_ge(cand) >= topk, cand, ans)

    ans = lax.fori_loop(0, 31, bit_body, ans0)
    thr = jnp.maximum(ans, INT_MIN + 1)

    cnt_ge = count_ge(thr)
    any_tie = jnp.max(jnp.where(cnt_ge > topk, 1, 0)) > 0

    @pl.when(any_tie)
    def _():
        need = topk - count_ge(thr + 1)

        def count_eq_below(lim):
            return count_tiles(lambda kv, s0: jnp.where((kv == thr) & ((s0 + rows_s) < lim), 1, 0)
                               .astype(jnp.int32))

        nbits = int(math.ceil(math.log2(key_ref.shape[0]))) + 1

        def jbit_body(bi, lim):
            cand = lim | jnp.left_shift(jnp.int32(1), nbits - 1 - bi)
            return jnp.where(count_eq_below(cand) <= need, cand, lim)

        lim = lax.fori_loop(0, nbits, jbit_body, jnp.zeros((1, tq), jnp.int32))

        def demote_body(kt, carry):
            s0 = pl.multiple_of(kt * tks, tks)
            kv = key_ref[pl.ds(s0, tks), :]
            drop = (kv == thr) & ((s0 + rows_s) >= lim)
            key_ref[pl.ds(s0, tks), :] = jnp.where(drop, INT_MIN, kv)
            return carry

        lax.fori_loop(0, n_s, demote_body, 0)

    q_t = _dot_nt(wuq_ref[...], cqb)
    qlat = []
    for h in range(nh):
        ql = _dot(wuk_ref[h], q_t[h * dh:(h + 1) * dh, :].astype(BF16)) * (scale * LOG2E)
        qlat.append(ql.astype(BF16))
    qlat_t = jnp.concatenate(qlat, axis=1)
    acc_ref[...] = jnp.zeros_like(acc_ref)
    nsub = tka // sub

    last_tile = n_a - 1
    int_max = jnp.int32(2 ** 31 - 1)

    def logits_to(buf_ref, kt):
        s0 = pl.multiple_of(jnp.minimum(kt, last_tile) * tka, tka)
        buf_ref[...] = _dot(ckv_ref[pl.ds(s0, tka), :], qlat_t)

    def consume(buf_ref, kt, carry):
        m_run, l_run = carry
        ktc = jnp.minimum(kt, last_tile)
        s0 = pl.multiple_of(ktc * tka, tka)
        thr_v = jnp.where(kt <= last_tile, thr, int_max)
        parts = []
        for r in range(nsub):
            kv = key_ref[pl.ds(pl.multiple_of(s0 + r * sub, sub), sub), :]
            madd = jnp.where(kv >= thr_v, 0.0, NEG_BIG)
            madd = jnp.concatenate([madd] * nh, axis=1) + bias_ref[jnp.clip(j - (ktc * nsub + r), 0, 2)]
            parts.append(buf_ref[r * sub:(r + 1) * sub, :] + madd)
        lg = jnp.concatenate(parts, axis=0)
        m_new = jnp.maximum(m_run, jnp.max(lg, axis=0, keepdims=True))
        p = jnp.exp2(lg - m_new)
        alpha = jnp.exp2(m_run - m_new)
        l_new = l_run * alpha + jnp.sum(p, axis=0, keepdims=True)
        ct = ckvt_ref[:, pl.ds(s0, tka)]
        acc_ref[...] = acc_ref[...] * alpha + _dot(ct, p.astype(BF16))
        return m_new, l_new

    def pair_body(t, carry):
        kt = 2 * t
        logits_to(lg1_ref, kt + 1)
        carry = consume(lg0_ref, kt, carry)
        logits_to(lg0_ref, kt + 2)
        return consume(lg1_ref, kt + 1, carry)

    m0 = jnp.full((1, nh * tq), NEG_BIG, F32)
    l0 = jnp.zeros((1, nh * tq), F32)
    logits_to(lg0_ref, 0)
    _, l_fin = lax.fori_loop(0, (n_a + 1) // 2, pair_body, (m0, l0))
    olat = (acc_ref[...] * (1.0 / l_fin)).astype(BF16)
    for h in range(nh):
        o_t = _dot(wuv_ref[h], olat[:, h * tq:(h + 1) * tq])
        o_ref[:, h * dh:(h + 1) * dh] = o_t.T.astype(o_ref.dtype)


def _dsa(cq, misc_t, kidx, ckv, ckv_t, wiq_t, wuq_t, wuk_h, wuv_h, bias_tab, *, batch, topk, scale):
    n, q_rank = cq.shape
    s = n // batch
    kv_rank = ckv.shape[-1]
    idx_dim = kidx.shape[-1]
    nh_idx = wiq_t.shape[0] // idx_dim
    nh, dh, _ = wuv_h.shape
    tq = LANE
    nb = s // tq
    tks = 512 if s % 512 == 0 else tq
    tka = 256 if s % 256 == 0 else tq
    return pl.pallas_call(
        functools.partial(_dsa_kernel, nh_idx=nh_idx, idx_dim=idx_dim, nh=nh, dh=dh, topk=topk,
                          scale=scale, tks=tks, tka=tka),
        grid=(batch, nb),
        in_specs=[
            pl.BlockSpec((tq, q_rank), lambda bi, j: (bi * nb + j, 0)),
            pl.BlockSpec((MISC_ROWS, tq), lambda bi, j: (0, bi * nb + j)),
            pl.BlockSpec((s, idx_dim), lambda bi, j: (bi, 0)),
            pl.BlockSpec((s, kv_rank), lambda bi, j: (bi, 0)),
            pl.BlockSpec((kv_rank, s), lambda bi, j: (0, bi)),
            _resident(wiq_t.shape), _resident(wuq_t.shape), _resident(wuk_h.shape),
            _resident(wuv_h.shape), _resident(bias_tab.shape),
        ],
        out_specs=pl.BlockSpec((tq, nh * dh), lambda bi, j: (bi * nb + j, 0)),
        out_shape=jax.ShapeDtypeStruct((n, nh * dh), BF16),
        scratch_shapes=[
            pltpu.VMEM((s, tq), jnp.int32),
            pltpu.VMEM((kv_rank, nh * tq), F32),
            pltpu.VMEM((tka, nh * tq), F32),
            pltpu.VMEM((tka, nh * tq), F32),
        ],
        compiler_params=_cparams(("parallel", "arbitrary")),
        name="dsa",
    )(cq, misc_t, kidx, ckv, ckv_t, wiq_t, wuq_t, wuk_h, wuv_h, bias_tab)


def _t5_bucket_np(n, n_buckets):
    max_exact = n_buckets // 2
    nf = np.maximum(n, 1).astype(np.float32)
    large = max_exact + (np.log(nf / max_exact) / math.log(REL_MAX_DIST / max_exact)
                         * (n_buckets - max_exact)).astype(np.int32)
    large = np.minimum(large, n_buckets - 1)
    return np.where(n < max_exact, n, large)


def _bias_table(rel_bias, tq):
    n_buckets, nh = rel_bias.shape
    assert _t5_bucket_np(np.array([tq + 1]), n_buckets)[0] == n_buckets - 1
    s_loc = np.arange(tq)[:, None]
    t_loc = np.arange(tq)[None, :]
    dist = np.stack([np.maximum(d * tq + t_loc - s_loc, 0) for d in range(3)])
    onehot = np.eye(n_buckets, dtype=np.float32)[_t5_bucket_np(dist, n_buckets)]
    tb = jnp.einsum('dstb,bh->dsht', jnp.asarray(onehot), rel_bias.astype(F32),
                    precision=lax.Precision.HIGHEST)
    far = rel_bias[n_buckets - 1].astype(F32)[None, None, :, None]
    return ((tb - far) * LOG2E).reshape(3, tq, nh * tq)


def kernel(x, ffn1_norm, ffn1_wg, ffn1_wu, ffn1_wd, mix_norm, w_in, conv_w, a_log, dt_bias, dn_out_norm, q_norm, kv_norm, w_uq, w_uk, w_uv, w_iq, idx_k_g, idx_k_b, rel_bias, w_o, ffn2_norm, ffn2_wg, ffn2_wu, ffn2_wd, final_norm):
    b, s, d = x.shape
    depth = ffn1_norm.shape[0]
    nh_dn = a_log.shape[1]
    dk = dn_out_norm.shape[1]
    qk_w = nh_dn * dk
    q_rank = q_norm.shape[1]
    kv_rank = kv_norm.shape[1]
    idx_dim = idx_k_g.shape[1]
    nh_idx = w_iq.shape[2] // idx_dim
    nh_sa, dh_sa = w_uk.shape[2], w_uk.shape[3]
    assert qk_w == d and nh_sa * dh_sa == d and nh_idx == nh_dn and 3 * nh_dn <= MISC_ROWS
    topk = min(TOPK_MAX, s // 4)
    n = b * s

    sizes = (3 * qk_w, qk_w, nh_dn, nh_dn, q_rank, kv_rank, idx_dim, nh_idx, d, d)
    offs = np.concatenate([[0], np.cumsum(sizes)])
    (o_qkv, o_z, o_b, o_a, o_cq, o_ckv, o_ik, o_iw, o_ga, o_gb) = [int(v) for v in offs[:-1]]
    l_cq, l_ckv, l_ik, l_misc, small_cols = _small_layout(q_rank, kv_rank, idx_dim)
    bias_tab = _bias_table(rel_bias, LANE)

    x2d = x.reshape(n, d)
    for l in range(depth):
        x1 = _ffn(x2d, ffn1_norm[l], ffn1_wg[l], ffn1_wu[l], ffn1_wd[l])

        wl = w_in[l]
        w_big = jnp.concatenate([wl[:, o_qkv:o_qkv + 4 * qk_w], wl[:, o_ga:o_ga + 2 * d]],
                                axis=1).astype(BF16)
        w_small = jnp.zeros((d, small_cols), F32)
        w_small = w_small.at[:, l_cq:l_cq + q_rank].set(wl[:, o_cq:o_cq + q_rank])
        w_small = w_small.at[:, l_ckv:l_ckv + kv_rank].set(wl[:, o_ckv:o_ckv + kv_rank])
        w_small = w_small.at[:, l_ik:l_ik + idx_dim].set(wl[:, o_ik:o_ik + idx_dim])
        w_small = w_small.at[:, l_misc:l_misc + nh_idx].set(wl[:, o_iw:o_iw + nh_idx])
        w_small = w_small.at[:, l_misc + nh_dn:l_misc + 2 * nh_dn].set(wl[:, o_b:o_b + nh_dn])
        w_small = w_small.at[:, l_misc + 2 * nh_dn:l_misc + 3 * nh_dn].set(wl[:, o_a:o_a + nh_dn])
        neg_a_pad = jnp.zeros((1, LANE), F32).at[0, 2 * nh_dn:3 * nh_dn].set(-jnp.exp(a_log[l].astype(F32)))
        dtb_pad = jnp.zeros((1, LANE), F32).at[0, 2 * nh_dn:3 * nh_dn].set(dt_bias[l].astype(F32))
        big2d, cq, ckv, ckv_t, kidx, misc, misc_t = _in_proj(
            x1, mix_norm[l], w_big, w_small.astype(BF16), q_norm[l], kv_norm[l], idx_k_g[l],
            idx_k_b[l], neg_a_pad, dtb_pad, q_rank=q_rank, kv_rank=kv_rank, idx_dim=idx_dim,
            nh=nh_dn, w_scale=(nh_idx ** -0.5) * (idx_dim ** -0.5))

        o_dn = _deltanet(big2d, conv_w[l], misc, misc_t, dn_out_norm[l], batch=b, nh=nh_dn, dk=dk)

        o_sa = _dsa(cq, misc_t, kidx, ckv, ckv_t,
                    w_iq[l].T.astype(BF16), w_uq[l].T.astype(BF16),
                    jnp.transpose(w_uk[l], (1, 0, 2)).astype(BF16),
                    jnp.transpose(w_uv[l], (1, 2, 0)).astype(BF16),
                    bias_tab, batch=b, topk=topk, scale=dh_sa ** -0.5)

        x2d = _ffn(x1, ffn2_norm[l], ffn2_wg[l], ffn2_wu[l], ffn2_wd[l],
                   final_g=final_norm if l == depth - 1 else None,
                   merge=(big2d, 4, o_dn, o_sa, w_o[l]))
    return x2d.reshape(b, s, d)
```

```python
import functools
import math

import numpy as np
import jax
import jax.numpy as jnp
from jax import lax
from jax.experimental import pallas as pl
from jax.experimental.pallas import tpu as pltpu

F32 = jnp.float32
BF16 = jnp.bfloat16
EPS = 1e-6
TOPK_MAX = 256
REL_MAX_DIST = 128
DN_CHUNK = 64
LANE = 128
MISC_ROWS = 32
INT_MIN = -(2 ** 31)
NEG_BIG = -1e30
LOG2E = math.log2(math.e)
VMEM_LIMIT = 56 * 1024 * 1024


def _cparams(sem):
    return pltpu.CompilerParams(dimension_semantics=sem, vmem_limit_bytes=VMEM_LIMIT)


def _resident(shape):
    return pl.BlockSpec(shape, lambda *_: (0,) * len(shape), pipeline_mode=pl.Buffered(1))


def _rms(x, g):
    return x * lax.rsqrt(jnp.mean(x * x, axis=-1, keepdims=True) + EPS) * g


def _silu(x):
    return x * jax.nn.sigmoid(x)


def _dot(a, b):
    return jnp.dot(a, b, preferred_element_type=F32)


def _dot_nt(a, b):
    return lax.dot_general(a, b, (((1,), (1,)), ((), ())), preferred_element_type=F32)


def _bdot(a, b):
    return lax.dot_general(a, b, (((2,), (1,)), ((0,), (0,))), preferred_element_type=F32)


def _bdot_nt(a, b):
    return lax.dot_general(a, b, (((2,), (2,)), ((0,), (0,))), preferred_element_type=F32)


def _bdot_tn(a, b):
    return lax.dot_general(a, b, (((1,), (1,)), ((0,), (0,))), preferred_element_type=F32)


def _ffn_kernel(*refs, final, merge, ff_chunk):
    refs = list(refs)
    x_ref = refs.pop(0)
    if merge:
        ga_ref, gb_ref, oa_ref, ob_ref, wo_ref = refs[:5]
        refs = refs[5:]
    g_ref, wg_ref, wu_ref, wd_ref = refs[:4]
    refs = refs[4:]
    if final:
        fg_ref = refs.pop(0)
    o_ref, act_ref = refs

    x = x_ref[...]
    if merge:
        merged = (jax.nn.sigmoid(ga_ref[...].astype(F32)) * oa_ref[...].astype(F32)
                  + jax.nn.sigmoid(gb_ref[...].astype(F32)) * ob_ref[...].astype(F32))
        x = x + _dot(merged.astype(BF16), wo_ref[...])
    h = _rms(x, g_ref[...]).astype(BF16)
    ff = wg_ref.shape[1]
    for c0 in range(0, ff, ff_chunk):
        c1 = min(c0 + ff_chunk, ff)
        a = _dot(h, wg_ref[:, c0:c1])
        u = _dot(h, wu_ref[:, c0:c1])
        act_ref[:, c0:c1] = (_silu(a) * u).astype(BF16)
    y = x + 0.5 * _dot(act_ref[...], wd_ref[...])
    if final:
        y = _rms(y, fg_ref[...])
    o_ref[...] = y


def _ffn(x2d, g, wg, wu, wd, *, final_g=None, merge=None, tm=512, ff_chunk=256):
    n, d = x2d.shape
    ff = wg.shape[1]
    row = lambda cols, blk=0: pl.BlockSpec((tm, cols), lambda i, blk=blk: (i, blk))
    in_specs = [row(d)]
    args = [x2d]
    if merge is not None:
        big2d, gate_blk, o_dn, o_sa, w_o = merge
        in_specs += [row(d, gate_blk), row(d, gate_blk + 1), row(d), row(d), _resident((d, d))]
        args += [big2d, big2d, o_dn, o_sa, w_o.astype(BF16)]
    in_specs += [_resident((1, d)), _resident((d, ff)), _resident((d, ff)), _resident((ff, d))]
    args += [g.reshape(1, d), wg.astype(BF16), wu.astype(BF16), wd.astype(BF16)]
    if final_g is not None:
        in_specs.append(_resident((1, d)))
        args.append(final_g.reshape(1, d))
    return pl.pallas_call(
        functools.partial(_ffn_kernel, final=final_g is not None, merge=merge is not None,
                          ff_chunk=ff_chunk),
        grid=(n // tm,),
        in_specs=in_specs,
        out_specs=row(d),
        out_shape=jax.ShapeDtypeStruct((n, d), F32),
        scratch_shapes=[pltpu.VMEM((tm, ff), BF16)],
        compiler_params=_cparams(("parallel",)),
        name="ffn_merge" if merge is not None else "ffn",
    )(*args)


def _small_layout(q_rank, kv_rank, idx_dim):
    o_cq = 0
    o_ckv = o_cq + q_rank
    o_ik = o_ckv + kv_rank
    o_misc = o_ik + LANE * ((idx_dim + LANE - 1) // LANE)
    return o_cq, o_ckv, o_ik, o_misc, o_misc + LANE


def _inproj_kernel(x_ref, g_ref, wb_ref, ws_ref, qn_ref, kvn_ref, ikg_ref, ikb_ref, negA_ref, dtb_ref,
                   big_ref, cq_ref, ckv_ref, ckvt_ref, ik_ref, misc_ref, misct_ref,
                   *, q_rank, kv_rank, idx_dim, nh, w_scale, col_chunk):
    o_cq, o_ckv, o_ik, o_misc, _ = _small_layout(q_rank, kv_rank, idx_dim)
    h = _rms(x_ref[...], g_ref[...]).astype(BF16)
    cols = wb_ref.shape[1]
    for c0 in range(0, cols, col_chunk):
        c1 = min(c0 + col_chunk, cols)
        big_ref[:, c0:c1] = _dot(h, wb_ref[:, c0:c1]).astype(big_ref.dtype)
    p = _dot(h, ws_ref[...])
    cq_ref[...] = _rms(p[:, o_cq:o_cq + q_rank], qn_ref[...]).astype(cq_ref.dtype)
    ckv = _rms(p[:, o_ckv:o_ckv + kv_rank], kvn_ref[...])
    ckv_ref[...] = ckv.astype(ckv_ref.dtype)
    ckvt_ref[...] = ckv.T.astype(ckvt_ref.dtype)
    ik = p[:, o_ik:o_ik + idx_dim]
    mu = jnp.mean(ik, axis=-1, keepdims=True)
    var = jnp.mean(jnp.square(ik - mu), axis=-1, keepdims=True)
    ik_ref[...] = ((ik - mu) * lax.rsqrt(var + EPS) * ikg_ref[...] + ikb_ref[...]).astype(ik_ref.dtype)
    m = p[:, o_misc:o_misc + LANE]
    lane = lax.broadcasted_iota(jnp.int32, m.shape, 1)
    beta = jax.nn.sigmoid(m)
    sp_in = m + dtb_ref[...]
    softplus = jnp.maximum(sp_in, 0.0) + jnp.log(1.0 + jnp.exp(-jnp.abs(sp_in)))
    gdec = negA_ref[...] * softplus
    misc = jnp.where(lane < nh, m * w_scale,
                     jnp.where(lane < 2 * nh, beta, jnp.where(lane < 3 * nh, gdec, 0.0)))
    misc_ref[...] = misc
    misct_ref[...] = misc.T[0:MISC_ROWS, :]


def _in_proj(x2d, g, w_big, w_small, q_norm, kv_norm, ik_g, ik_b, neg_a_pad, dtb_pad, *,
             q_rank, kv_rank, idx_dim, nh, w_scale, tm=512, col_chunk=512):
    n, d = x2d.shape
    cols = w_big.shape[1]
    row = lambda c: pl.BlockSpec((tm, c), lambda i: (i, 0))
    colblk = lambda r: pl.BlockSpec((r, tm), lambda i: (0, i))
    return pl.pallas_call(
        functools.partial(_inproj_kernel, q_rank=q_rank, kv_rank=kv_rank, idx_dim=idx_dim, nh=nh,
                          w_scale=w_scale, col_chunk=col_chunk),
        grid=(n // tm,),
        in_specs=[
            row(d), _resident((1, d)), _resident((d, cols)), _resident((d, w_small.shape[1])),
            _resident((1, q_rank)), _resident((1, kv_rank)), _resident((1, idx_dim)),
            _resident((1, idx_dim)), _resident((1, LANE)), _resident((1, LANE)),
        ],
        out_specs=[row(cols), row(q_rank), row(kv_rank), colblk(kv_rank), row(idx_dim), row(LANE),
                   colblk(MISC_ROWS)],
        out_shape=[
            jax.ShapeDtypeStruct((n, cols), BF16),
            jax.ShapeDtypeStruct((n, q_rank), BF16),
            jax.ShapeDtypeStruct((n, kv_rank), BF16),
            jax.ShapeDtypeStruct((kv_rank, n), BF16),
            jax.ShapeDtypeStruct((n, idx_dim), BF16),
            jax.ShapeDtypeStruct((n, LANE), F32),
            jax.ShapeDtypeStruct((MISC_ROWS, n), F32),
        ],
        compiler_params=_cparams(("parallel",)),
        name="in_proj",
    )(x2d, g.reshape(1, d), w_big, w_small, q_norm.reshape(1, -1), kv_norm.reshape(1, -1),
      ik_g.reshape(1, -1), ik_b.reshape(1, -1), neg_a_pad, dtb_pad)


def _tri_inverse(lmat, masks, eye):
    x = eye - lmat * masks[0]
    for mk in masks[1:]:
        lm = (lmat * mk).astype(BF16)
        t = _bdot(x.astype(BF16), lm)
        x = x - _bdot(t.astype(BF16), x.astype(BF16))
    return x


def _deltanet_kernel(q_ref, k_ref, v_ref, z_ref, cw_ref, bg_ref, mt_ref, on_ref, o_ref,
                     xpad_ref, act_ref, s_ref, *, nh, dk, ts, conv_k):
    c = DN_CHUNK
    cpi = ts // c
    i = pl.program_id(1)
    qk_w = nh * dk
    halo = 8

    @pl.when(i == 0)
    def _():
        xpad_ref[0:halo, :] = jnp.zeros((halo, 3 * qk_w), F32)
        s_ref[...] = jnp.zeros_like(s_ref)

    xpad_ref[halo:halo + ts, 0:qk_w] = q_ref[...].astype(F32)
    xpad_ref[halo:halo + ts, qk_w:2 * qk_w] = k_ref[...].astype(F32)
    xpad_ref[halo:halo + ts, 2 * qk_w:3 * qk_w] = v_ref[...].astype(F32)
    y = xpad_ref[halo:halo + ts, :] * cw_ref[conv_k - 1:conv_k, :]
    for dlt in range(1, conv_k):
        y = y + xpad_ref[halo - dlt:halo - dlt + ts, :] * cw_ref[conv_k - 1 - dlt:conv_k - dlt, :]
    act_ref[...] = _silu(y)
    xpad_ref[0:halo, :] = xpad_ref[ts:ts + halo, :]

    row = lax.broadcasted_iota(jnp.int32, (c, c), 0)
    col = lax.broadcasted_iota(jnp.int32, (c, c), 1)
    tril = row >= col
    strict = row > col
    eye = jnp.where(row == col, 1.0, 0.0).astype(F32)
    tril_f = jnp.where(tril, 1.0, 0.0).astype(F32)
    masks = []
    m = 1
    while m < c:
        mk = strict & ((row // (2 * m)) == (col // (2 * m))) & ((row // m) != (col // m))
        masks.append(jnp.where(mk, 1.0, 0.0).astype(F32))
        m *= 2
    onorm = on_ref[...]

    def heads(ref, r0, base):
        return [ref[r0:r0 + c, base + h * dk:base + (h + 1) * dk] for h in range(nh)]

    qs, ks, vs, betas, gccs, gcrs, glasts = [], [], [], [], [], [], []
    for cc in range(cpi):
        r0 = cc * c
        bg = bg_ref[r0:r0 + c, :]
        grow = mt_ref[2 * nh:3 * nh, r0:r0 + c]
        gc_col = jnp.dot(tril_f, bg, precision=lax.Precision.HIGHEST,
                         preferred_element_type=F32)
        gc_row = lax.dot_general(grow, tril_f, (((1,), (1,)), ((), ())),
                                 precision=lax.Precision.HIGHEST,
                                 preferred_element_type=F32)
        qs += heads(act_ref, r0, 0)
        ks += heads(act_ref, r0, qk_w)
        vs += heads(act_ref, r0, 2 * qk_w)
        betas += [bg[:, nh + h:nh + h + 1] for h in range(nh)]
        gccs += [gc_col[:, 2 * nh + h:2 * nh + h + 1] for h in range(nh)]
        gcrs += [gc_row[h:h + 1, :] for h in range(nh)]
        glasts += [gc_row[h:h + 1, c - 1:c] for h in range(nh)]
    q = jnp.stack(qs)
    k = jnp.stack(ks)
    v = jnp.stack(vs)
    beta = jnp.stack(betas)
    gcc = jnp.stack(gccs)
    gcr = jnp.stack(gcrs)
    g_last = jnp.stack(glasts)
    q = q * (lax.rsqrt(jnp.sum(q * q, axis=-1, keepdims=True) + EPS) * (dk ** -0.5))
    k = k * lax.rsqrt(jnp.sum(k * k, axis=-1, keepdims=True) + EPS)
    decay = jnp.where(tril, jnp.exp(jnp.minimum(gcc - gcr, 0.0)), 0.0)
    eg = jnp.exp(gcc)
    kb = k * beta
    vb = v * beta
    aq = _bdot_nt(jnp.concatenate([kb, q], axis=1).astype(BF16), k.astype(BF16))
    lmat = jnp.where(strict, aq[:, :c] * decay, 0.0)
    intra = (aq[:, c:] * decay).astype(BF16)
    tinv = _tri_inverse(lmat, masks, eye)
    uw = _bdot(tinv.astype(BF16), jnp.concatenate([vb, kb * eg], axis=2).astype(BF16))
    u = uw[:, :, :dk]
    wq = jnp.concatenate([uw[:, :, dk:], q * eg], axis=1).astype(BF16)
    kd = (k * jnp.exp(g_last - gcc)).astype(BF16)
    sdec = jnp.exp(g_last)
    s = s_ref[...]
    for cc in range(cpi):
        sl = slice(cc * nh, (cc + 1) * nh)
        m1 = _bdot(wq[sl], s.astype(BF16))
        vnb = (u[sl] - m1[:, :c]).astype(BF16)
        o = _rms(m1[:, c:] + _bdot(intra[sl], vnb), onorm)
        s = s * sdec[sl] + _bdot_tn(kd[sl], vnb)
        for h in range(nh):
            zz = z_ref[cc * c:(cc + 1) * c, h * dk:(h + 1) * dk].astype(F32)
            o_ref[cc * c:(cc + 1) * c, h * dk:(h + 1) * dk] = (o[h] * _silu(zz)).astype(o_ref.dtype)
    s_ref[...] = s


def _deltanet(big2d, conv_w, misc, misc_t, out_norm, *, batch, nh, dk, ts=256):
    n = big2d.shape[0]
    steps = n // batch // ts
    qk_w = nh * dk
    conv_k = conv_w.shape[0]
    tok = lambda cols, blk: pl.BlockSpec((ts, cols), lambda bi, i, blk=blk: (bi * steps + i, blk))
    return pl.pallas_call(
        functools.partial(_deltanet_kernel, nh=nh, dk=dk, ts=ts, conv_k=conv_k),
        grid=(batch, steps),
        in_specs=[
            tok(qk_w, 0), tok(qk_w, 1), tok(qk_w, 2), tok(qk_w, 3),
            _resident((conv_k, 3 * qk_w)),
            tok(LANE, 0),
            pl.BlockSpec((MISC_ROWS, ts), lambda bi, i: (0, bi * steps + i)),
            _resident((1, dk)),
        ],
        out_specs=tok(qk_w, 0),
        out_shape=jax.ShapeDtypeStruct((n, qk_w), BF16),
        scratch_shapes=[
            pltpu.VMEM((ts + 8, 3 * qk_w), F32),
            pltpu.VMEM((ts, 3 * qk_w), F32),
            pltpu.VMEM((nh, dk, dk), F32),
        ],
        compiler_params=_cparams(("arbitrary", "arbitrary")),
        name="deltanet",
    )(big2d, big2d, big2d, big2d, conv_w, misc, misc_t, out_norm.reshape(1, dk))


def _sortable(x):
    i = pltpu.bitcast(x, jnp.int32)
    return i ^ ((i >> 31) & jnp.int32(0x7FFFFFFF))


def _bit_transpose32(words):
    a = list(words)
    assert len(a) == 32
    j, m = 16, 0x0000FFFF
    while j:
        for k in range(32):
            if not k & j:
                t = (lax.shift_right_logical(a[k], jnp.int32(j)) ^ a[k + j]) & jnp.int32(m)
                a[k + j] = a[k + j] ^ t
                a[k] = a[k] ^ (t << j)
        j >>= 1
        m = (m ^ (m << j)) & 0xFFFFFFFF if j else m
    return a


def _dsa_kernel(cq_ref, mt_ref, ki_ref, ckv_ref, ckvt_ref, wiq_ref, wuq_ref, wuk_ref, wuv_ref,
                bias_ref, o_ref, key_ref, planes_ref, acc_ref, lg0_ref, lg1_ref,
                *, nh_idx, idx_dim, nh, dh, topk, scale, tks, tka):
    tq = LANE
    sub = LANE
    j = pl.program_id(1)
    n_s = (j * tq + tq + tks - 1) // tks
    n_a = (j * tq + tq + tka - 1) // tka
    cqb = cq_ref[...]
    rows = lax.broadcasted_iota(jnp.int32, (sub, tq), 0)
    qpos = j * tq + lax.broadcasted_iota(jnp.int32, (sub, tq), 1)
    rows_s = lax.broadcasted_iota(jnp.int32, (tks, tq), 0)

    qidx_t = _dot_nt(wiq_ref[...], cqb).astype(BF16)
    qidx_all = jnp.concatenate([qidx_t[h * idx_dim:(h + 1) * idx_dim, :] for h in range(nh_idx)],
                               axis=1)
    wrow_all = jnp.concatenate([mt_ref[h:h + 1, :] for h in range(nh_idx)], axis=1)

    @pl.when((pl.program_id(0) == 0) & (j == 0))
    def _():
        planes_ref[...] = jnp.zeros_like(planes_ref)

    def score_tile(kt):
        words = []
        for r in range(tks // sub):
            s0 = pl.multiple_of(kt * tks + r * sub, sub)
            kk = ki_ref[pl.ds(s0, sub), :]
            d = jnp.maximum(_dot(kk, qidx_all), 0.0) * wrow_all
            parts = [d[:, h * tq:(h + 1) * tq] for h in range(nh_idx)]
            while len(parts) > 1:
                parts = [parts[a] + parts[a + 1] for a in range(0, len(parts), 2)]
            causal = (s0 + rows) <= qpos
            key = jnp.where(causal, _sortable(parts[0]), INT_MIN)
            key_ref[pl.ds(s0, sub), :] = key
            ukey = key ^ INT_MIN
            words += [ukey[8 * w:8 * (w + 1), :] for w in range(sub // 8)]
        for b, plane in enumerate(_bit_transpose32(words)):
            planes_ref[b, kt] = plane

    def score_body(t, carry):
        score_tile(2 * t)
        score_tile(2 * t + 1)
        return carry

    assert planes_ref.shape[1] % 2 == 0
    lax.fori_loop(0, (n_s + 1) // 2, score_body, 0)

    nt = planes_ref.shape[1]
    tile_ids = lax.broadcasted_iota(jnp.int32, (nt, 8, tq), 0)
    eq0 = jnp.where(tile_ids < n_s, -1, 0).astype(jnp.int32)

    def bit_body(bi, state):
        ans, cnt_gt, eq = state
        b = 31 - bi
        t = eq & planes_ref[b]
        c = jnp.sum(jnp.sum(lax.population_count(t), axis=0), axis=0, keepdims=True)
        accept = (cnt_gt + c) >= topk
        ans = jnp.where(accept, ans | jnp.left_shift(jnp.int32(1), b), ans)
        cnt_gt = jnp.where(accept, cnt_gt, cnt_gt + c)
        eq = jnp.where(accept, t, eq ^ t)
        return ans, cnt_gt, eq

    zero_row = jnp.zeros((1, tq), jnp.int32)
    ans_u, cnt_gt, eq = lax.fori_loop(0, 32, bit_body, (zero_row, zero_row, eq0))
    thr = jnp.maximum(ans_u ^ INT_MIN, INT_MIN + 1)

    n_eq = jnp.sum(jnp.sum(lax.population_count(eq), axis=0), axis=0, keepdims=True)
    any_tie = jnp.max(jnp.where((cnt_gt + n_eq > topk) & (ans_u != 0), 1, 0)) > 0

    def count_tiles(hit_fn):
        def body(kt, acc):
            s0 = pl.multiple_of(kt * tks, tks)
            hit = hit_fn(key_ref[pl.ds(s0, tks), :], s0)
            return acc + jnp.sum(hit.reshape(tks // 32, 4, 8, tq), axis=0)
        acc = lax.fori_loop(0, n_s, body, jnp.zeros((4, 8, tq), jnp.int32))
        return jnp.sum(acc.reshape(32, tq), axis=0, keepdims=True)

    def count_ge(cand):
        return count_tiles(lambda kv, s0: jnp.where(kv >= cand, 1, 0).astype(jnp.int32))

    @pl.when(any_tie)
    def _():
        need = topk - count_ge(thr + 1)

        def count_eq_below(lim):
            return count_tiles(lambda kv, s0: jnp.where((kv == thr) & ((s0 + rows_s) < lim), 1, 0)
                               .astype(jnp.int32))

        nbits = int(math.ceil(math.log2(key_ref.shape[0]))) + 1

        def jbit_body(bi, lim):
            cand = lim | jnp.left_shift(jnp.int32(1), nbits - 1 - bi)
            return jnp.where(count_eq_below(cand) <= need, cand, lim)

        lim = lax.fori_loop(0, nbits, jbit_body, jnp.zeros((1, tq), jnp.int32))

        def demote_body(kt, carry):
            s0 = pl.multiple_of(kt * tks, tks)
            kv = key_ref[pl.ds(s0, tks), :]
            drop = (kv == thr) & ((s0 + rows_s) >= lim)
            key_ref[pl.ds(s0, tks), :] = jnp.where(drop, INT_MIN, kv)
            return carry

        lax.fori_loop(0, n_s, demote_body, 0)

    q_t = _dot_nt(wuq_ref[...], cqb)
    qlat = []
    for h in range(nh):
        ql = _dot(wuk_ref[h], q_t[h * dh:(h + 1) * dh, :].astype(BF16)) * (scale * LOG2E)
        qlat.append(ql.astype(BF16))
    qlat_t = jnp.concatenate(qlat, axis=1)
    acc_ref[...] = jnp.zeros_like(acc_ref)
    nsub = tka // sub

    last_tile = n_a - 1
    int_max = jnp.int32(2 ** 31 - 1)

    def logits_to(buf_ref, kt):
        s0 = pl.multiple_of(jnp.minimum(kt, last_tile) * tka, tka)
        buf_ref[...] = _dot(ckv_ref[pl.ds(s0, tka), :], qlat_t)

    def consume(buf_ref, kt, carry):
        m_run, l_run = carry
        ktc = jnp.minimum(kt, last_tile)
        s0 = pl.multiple_of(ktc * tka, tka)
        thr_v = jnp.where(kt <= last_tile, thr, int_max)
        parts = []
        for r in range(nsub):
            kv = key_ref[pl.ds(pl.multiple_of(s0 + r * sub, sub), sub), :]
            madd = jnp.where(kv >= thr_v, 0.0, NEG_BIG)
            madd = jnp.concatenate([madd] * nh, axis=1) + bias_ref[jnp.clip(j - (ktc * nsub + r), 0, 2)]
            parts.append(buf_ref[r * sub:(r + 1) * sub, :] + madd)
        lg = jnp.concatenate(parts, axis=0)
        m_new = jnp.maximum(m_run, jnp.max(lg, axis=0, keepdims=True))
        p = jnp.exp2(lg - m_new)
        alpha = jnp.exp2(m_run - m_new)
        l_new = l_run * alpha + jnp.sum(p, axis=0, keepdims=True)
        ct = ckvt_ref[:, pl.ds(s0, tka)]
        acc_ref[...] = acc_ref[...] * alpha + _dot(ct, p.astype(BF16))
        return m_new, l_new

    def pair_body(t, carry):
        kt = 2 * t
        logits_to(lg1_ref, kt + 1)
        carry = consume(lg0_ref, kt, carry)
        logits_to(lg0_ref, kt + 2)
        return consume(lg1_ref, kt + 1, carry)

    m0 = jnp.full((1, nh * tq), NEG_BIG, F32)
    l0 = jnp.zeros((1, nh * tq), F32)
    logits_to(lg0_ref, 0)
    _, l_fin = lax.fori_loop(0, (n_a + 1) // 2, pair_body, (m0, l0))
    olat = (acc_ref[...] * (1.0 / l_fin)).astype(BF16)
    for h in range(nh):
        o_t = _dot(wuv_ref[h], olat[:, h * tq:(h + 1) * tq])
        o_ref[:, h * dh:(h + 1) * dh] = o_t.T.astype(o_ref.dtype)


def _dsa(cq, misc_t, kidx, ckv, ckv_t, wiq_t, wuq_t, wuk_h, wuv_h, bias_tab, *, batch, topk, scale):
    n, q_rank = cq.shape
    s = n // batch
    kv_rank = ckv.shape[-1]
    idx_dim = kidx.shape[-1]
    nh_idx = wiq_t.shape[0] // idx_dim
    nh, dh, _ = wuv_h.shape
    tq = LANE
    nb = s // tq
    tks = 256
    tka = 256
    assert s % tks == 0 and s % tka == 0
    return pl.pallas_call(
        functools.partial(_dsa_kernel, nh_idx=nh_idx, idx_dim=idx_dim, nh=nh, dh=dh, topk=topk,
                          scale=scale, tks=tks, tka=tka),
        grid=(batch, nb),
        in_specs=[
            pl.BlockSpec((tq, q_rank), lambda bi, j: (bi * nb + j, 0)),
            pl.BlockSpec((MISC_ROWS, tq), lambda bi, j: (0, bi * nb + j)),
            pl.BlockSpec((s, idx_dim), lambda bi, j: (bi, 0)),
            pl.BlockSpec((s, kv_rank), lambda bi, j: (bi, 0)),
            pl.BlockSpec((kv_rank, s), lambda bi, j: (0, bi)),
            _resident(wiq_t.shape), _resident(wuq_t.shape), _resident(wuk_h.shape),
            _resident(wuv_h.shape), _resident(bias_tab.shape),
        ],
        out_specs=pl.BlockSpec((tq, nh * dh), lambda bi, j: (bi * nb + j, 0)),
        out_shape=jax.ShapeDtypeStruct((n, nh * dh), BF16),
        scratch_shapes=[
            pltpu.VMEM((s, tq), jnp.int32),
            pltpu.VMEM((32, s // tks, 8, tq), jnp.int32),
            pltpu.VMEM((kv_rank, nh * tq), F32),
            pltpu.VMEM((tka, nh * tq), F32),
            pltpu.VMEM((tka, nh * tq), F32),
        ],
        compiler_params=_cparams(("parallel", "arbitrary")),
        name="dsa",
    )(cq, misc_t, kidx, ckv, ckv_t, wiq_t, wuq_t, wuk_h, wuv_h, bias_tab)


def _t5_bucket_np(n, n_buckets):
    max_exact = n_buckets // 2
    nf = np.maximum(n, 1).astype(np.float32)
    large = max_exact + (np.log(nf / max_exact) / math.log(REL_MAX_DIST / max_exact)
                         * (n_buckets - max_exact)).astype(np.int32)
    large = np.minimum(large, n_buckets - 1)
    return np.where(n < max_exact, n, large)


def _bias_table(rel_bias, tq):
    n_buckets, nh = rel_bias.shape
    assert _t5_bucket_np(np.array([tq + 1]), n_buckets)[0] == n_buckets - 1
    s_loc = np.arange(tq)[:, None]
    t_loc = np.arange(tq)[None, :]
    dist = np.stack([np.maximum(d * tq + t_loc - s_loc, 0) for d in range(3)])
    onehot = np.eye(n_buckets, dtype=np.float32)[_t5_bucket_np(dist, n_buckets)]
    tb = jnp.einsum('dstb,bh->dsht', jnp.asarray(onehot), rel_bias.astype(F32),
                    precision=lax.Precision.HIGHEST)
    far = rel_bias[n_buckets - 1].astype(F32)[None, None, :, None]
    return ((tb - far) * LOG2E).reshape(3, tq, nh * tq)


def kernel(x, ffn1_norm, ffn1_wg, ffn1_wu, ffn1_wd, mix_norm, w_in, conv_w, a_log, dt_bias, dn_out_norm, q_norm, kv_norm, w_uq, w_uk, w_uv, w_iq, idx_k_g, idx_k_b, rel_bias, w_o, ffn2_norm, ffn2_wg, ffn2_wu, ffn2_wd, final_norm):
    b, s, d = x.shape
    depth = ffn1_norm.shape[0]
    nh_dn = a_log.shape[1]
    dk = dn_out_norm.shape[1]
    qk_w = nh_dn * dk
    q_rank = q_norm.shape[1]
    kv_rank = kv_norm.shape[1]
    idx_dim = idx_k_g.shape[1]
    nh_idx = w_iq.shape[2] // idx_dim
    nh_sa, dh_sa = w_uk.shape[2], w_uk.shape[3]
    assert qk_w == d and nh_sa * dh_sa == d and nh_idx == nh_dn and 3 * nh_dn <= MISC_ROWS
    topk = min(TOPK_MAX, s // 4)
    n = b * s

    sizes = (3 * qk_w, qk_w, nh_dn, nh_dn, q_rank, kv_rank, idx_dim, nh_idx, d, d)
    offs = np.concatenate([[0], np.cumsum(sizes)])
    (o_qkv, o_z, o_b, o_a, o_cq, o_ckv, o_ik, o_iw, o_ga, o_gb) = [int(v) for v in offs[:-1]]
    l_cq, l_ckv, l_ik, l_misc, small_cols = _small_layout(q_rank, kv_rank, idx_dim)
    bias_tab = _bias_table(rel_bias, LANE)

    x2d = x.reshape(n, d)
    for l in range(depth):
        x1 = _ffn(x2d, ffn1_norm[l], ffn1_wg[l], ffn1_wu[l], ffn1_wd[l])

        wl = w_in[l]
        w_big = jnp.concatenate([wl[:, o_qkv:o_qkv + 4 * qk_w], wl[:, o_ga:o_ga + 2 * d]],
                                axis=1).astype(BF16)
        w_small = jnp.zeros((d, small_cols), F32)
        w_small = w_small.at[:, l_cq:l_cq + q_rank].set(wl[:, o_cq:o_cq + q_rank])
        w_small = w_small.at[:, l_ckv:l_ckv + kv_rank].set(wl[:, o_ckv:o_ckv + kv_rank])
        w_small = w_small.at[:, l_ik:l_ik + idx_dim].set(wl[:, o_ik:o_ik + idx_dim])
        w_small = w_small.at[:, l_misc:l_misc + nh_idx].set(wl[:, o_iw:o_iw + nh_idx])
        w_small = w_small.at[:, l_misc + nh_dn:l_misc + 2 * nh_dn].set(wl[:, o_b:o_b + nh_dn])
        w_small = w_small.at[:, l_misc + 2 * nh_dn:l_misc + 3 * nh_dn].set(wl[:, o_a:o_a + nh_dn])
        neg_a_pad = jnp.zeros((1, LANE), F32).at[0, 2 * nh_dn:3 * nh_dn].set(-jnp.exp(a_log[l].astype(F32)))
        dtb_pad = jnp.zeros((1, LANE), F32).at[0, 2 * nh_dn:3 * nh_dn].set(dt_bias[l].astype(F32))
        big2d, cq, ckv, ckv_t, kidx, misc, misc_t = _in_proj(
            x1, mix_norm[l], w_big, w_small.astype(BF16), q_norm[l], kv_norm[l], idx_k_g[l],
            idx_k_b[l], neg_a_pad, dtb_pad, q_rank=q_rank, kv_rank=kv_rank, idx_dim=idx_dim,
            nh=nh_dn, w_scale=(nh_idx ** -0.5) * (idx_dim ** -0.5))

        o_dn = _deltanet(big2d, conv_w[l], misc, misc_t, dn_out_norm[l], batch=b, nh=nh_dn, dk=dk)

        o_sa = _dsa(cq, misc_t, kidx, ckv, ckv_t,
                    w_iq[l].T.astype(BF16), w_uq[l].T.astype(BF16),
                    jnp.transpose(w_uk[l], (1, 0, 2)).astype(BF16),
                    jnp.transpose(w_uv[l], (1, 2, 0)).astype(BF16),
                    bias_tab, batch=b, topk=topk, scale=dh_sa ** -0.5)

        x2d = _ffn(x1, ffn2_norm[l], ffn2_wg[l], ffn2_wu[l], ffn2_wd[l],
                   final_g=final_norm if l == depth - 1 else None,
                   merge=(big2d, 4, o_dn, o_sa, w_o[l]))
    return x2d.reshape(b, s, d)
```

```python
import functools
import math

import numpy as np
import jax
import jax.numpy as jnp
from jax import lax
from jax.experimental import pallas as pl
from jax.experimental.pallas import tpu as pltpu

F32 = jnp.float32
BF16 = jnp.bfloat16
EPS = 1e-6
TOPK_MAX = 256
REL_MAX_DIST = 128
DN_CHUNK = 64
LANE = 128
SUM_ROWS = 8
MISC_ROWS = 32
INT_MIN = -(2 ** 31)
NEG_BIG = -1e30
LOG2E = math.log2(math.e)
VMEM_LIMIT = 56 * 1024 * 1024


def _cparams(sem):
    return pltpu.CompilerParams(dimension_semantics=sem, vmem_limit_bytes=VMEM_LIMIT)


def _resident(shape):
    return pl.BlockSpec(shape, lambda *_: (0,) * len(shape), pipeline_mode=pl.Buffered(1))


def _rms(x, g):
    return x * lax.rsqrt(jnp.mean(x * x, axis=-1, keepdims=True) + EPS) * g


def _silu(x):
    return x * jax.nn.sigmoid(x)


def _dot(a, b):
    return jnp.dot(a, b, preferred_element_type=F32)


def _dot_nt(a, b):
    return lax.dot_general(a, b, (((1,), (1,)), ((), ())), preferred_element_type=F32)


def _bdot(a, b):
    return lax.dot_general(a, b, (((2,), (1,)), ((0,), (0,))), preferred_element_type=F32)


def _bdot_nt(a, b):
    return lax.dot_general(a, b, (((2,), (2,)), ((0,), (0,))), preferred_element_type=F32)


def _bdot_tn(a, b):
    return lax.dot_general(a, b, (((1,), (1,)), ((0,), (0,))), preferred_element_type=F32)


def _ffn_kernel(*refs, final, merge, ff_chunk):
    refs = list(refs)
    x_ref = refs.pop(0)
    if merge:
        ga_ref, gb_ref, oa_ref, ob_ref, wo_ref = refs[:5]
        refs = refs[5:]
    g_ref, wg_ref, wu_ref, wd_ref = refs[:4]
    refs = refs[4:]
    if final:
        fg_ref = refs.pop(0)
    o_ref, act_ref = refs

    x = x_ref[...]
    if merge:
        merged = (jax.nn.sigmoid(ga_ref[...].astype(F32)) * oa_ref[...].astype(F32)
                  + jax.nn.sigmoid(gb_ref[...].astype(F32)) * ob_ref[...].astype(F32))
        x = x + _dot(merged.astype(BF16), wo_ref[...])
    h = _rms(x, g_ref[...]).astype(BF16)
    ff = wg_ref.shape[1]
    for c0 in range(0, ff, ff_chunk):
        c1 = min(c0 + ff_chunk, ff)
        a = _dot(h, wg_ref[:, c0:c1])
        u = _dot(h, wu_ref[:, c0:c1])
        act_ref[:, c0:c1] = (_silu(a) * u).astype(BF16)
    y = x + 0.5 * _dot(act_ref[...], wd_ref[...])
    if final:
        y = _rms(y, fg_ref[...])
    o_ref[...] = y


def _ffn(x2d, g, wg, wu, wd, *, final_g=None, merge=None, tm=512, ff_chunk=256):
    n, d = x2d.shape
    ff = wg.shape[1]
    row = lambda cols, blk=0: pl.BlockSpec((tm, cols), lambda i, blk=blk: (i, blk))
    in_specs = [row(d)]
    args = [x2d]
    if merge is not None:
        big2d, gate_blk, o_dn, o_sa, w_o = merge
        in_specs += [row(d, gate_blk), row(d, gate_blk + 1), row(d), row(d), _resident((d, d))]
        args += [big2d, big2d, o_dn, o_sa, w_o.astype(BF16)]
    in_specs += [_resident((1, d)), _resident((d, ff)), _resident((d, ff)), _resident((ff, d))]
    args += [g.reshape(1, d), wg.astype(BF16), wu.astype(BF16), wd.astype(BF16)]
    if final_g is not None:
        in_specs.append(_resident((1, d)))
        args.append(final_g.reshape(1, d))
    return pl.pallas_call(
        functools.partial(_ffn_kernel, final=final_g is not None, merge=merge is not None,
                          ff_chunk=ff_chunk),
        grid=(n // tm,),
        in_specs=in_specs,
        out_specs=row(d),
        out_shape=jax.ShapeDtypeStruct((n, d), F32),
        scratch_shapes=[pltpu.VMEM((tm, ff), BF16)],
        compiler_params=_cparams(("parallel",)),
        name="ffn_merge" if merge is not None else "ffn",
    )(*args)


def _small_layout(q_rank, kv_rank, idx_dim):
    o_cq = 0
    o_ckv = o_cq + q_rank
    o_ik = o_ckv + kv_rank
    o_misc = o_ik + LANE * ((idx_dim + LANE - 1) // LANE)
    return o_cq, o_ckv, o_ik, o_misc, o_misc + LANE


def _inproj_kernel(x_ref, g_ref, wb_ref, ws_ref, qn_ref, kvn_ref, ikg_ref, ikb_ref, negA_ref, dtb_ref,
                   big_ref, cq_ref, ckv_ref, ckvt_ref, ik_ref, misc_ref, misct_ref,
                   *, q_rank, kv_rank, idx_dim, nh, w_scale, col_chunk):
    o_cq, o_ckv, o_ik, o_misc, _ = _small_layout(q_rank, kv_rank, idx_dim)
    h = _rms(x_ref[...], g_ref[...]).astype(BF16)
    cols = wb_ref.shape[1]
    for c0 in range(0, cols, col_chunk):
        c1 = min(c0 + col_chunk, cols)
        big_ref[:, c0:c1] = _dot(h, wb_ref[:, c0:c1]).astype(big_ref.dtype)
    p = _dot(h, ws_ref[...])
    cq_ref[...] = _rms(p[:, o_cq:o_cq + q_rank], qn_ref[...]).astype(cq_ref.dtype)
    ckv = _rms(p[:, o_ckv:o_ckv + kv_rank], kvn_ref[...])
    ckv_ref[...] = ckv.astype(ckv_ref.dtype)
    ones = jnp.ones((SUM_ROWS, ckv.shape[0]), F32)
    ckvt_ref[...] = jnp.concatenate([ckv.T, ones], axis=0).astype(ckvt_ref.dtype)
    ik = p[:, o_ik:o_ik + idx_dim]
    mu = jnp.mean(ik, axis=-1, keepdims=True)
    var = jnp.mean(jnp.square(ik - mu), axis=-1, keepdims=True)
    ik_ref[...] = ((ik - mu) * lax.rsqrt(var + EPS) * ikg_ref[...] + ikb_ref[...]).astype(ik_ref.dtype)
    m = p[:, o_misc:o_misc + LANE]
    lane = lax.broadcasted_iota(jnp.int32, m.shape, 1)
    beta = jax.nn.sigmoid(m)
    sp_in = m + dtb_ref[...]
    softplus = jnp.maximum(sp_in, 0.0) + jnp.log(1.0 + jnp.exp(-jnp.abs(sp_in)))
    gdec = negA_ref[...] * softplus
    misc = jnp.where(lane < nh, m * w_scale,
                     jnp.where(lane < 2 * nh, beta, jnp.where(lane < 3 * nh, gdec, 0.0)))
    misc_ref[...] = misc
    misct_ref[...] = misc.T[0:MISC_ROWS, :]


def _in_proj(x2d, g, w_big, w_small, q_norm, kv_norm, ik_g, ik_b, neg_a_pad, dtb_pad, *,
             q_rank, kv_rank, idx_dim, nh, w_scale, tm=512, col_chunk=512):
    n, d = x2d.shape
    cols = w_big.shape[1]
    row = lambda c: pl.BlockSpec((tm, c), lambda i: (i, 0))
    colblk = lambda r: pl.BlockSpec((r, tm), lambda i: (0, i))
    return pl.pallas_call(
        functools.partial(_inproj_kernel, q_rank=q_rank, kv_rank=kv_rank, idx_dim=idx_dim, nh=nh,
                          w_scale=w_scale, col_chunk=col_chunk),
        grid=(n // tm,),
        in_specs=[
            row(d), _resident((1, d)), _resident((d, cols)), _resident((d, w_small.shape[1])),
            _resident((1, q_rank)), _resident((1, kv_rank)), _resident((1, idx_dim)),
            _resident((1, idx_dim)), _resident((1, LANE)), _resident((1, LANE)),
        ],
        out_specs=[row(cols), row(q_rank), row(kv_rank), colblk(kv_rank + SUM_ROWS), row(idx_dim), row(LANE),
                   colblk(MISC_ROWS)],
        out_shape=[
            jax.ShapeDtypeStruct((n, cols), BF16),
            jax.ShapeDtypeStruct((n, q_rank), BF16),
            jax.ShapeDtypeStruct((n, kv_rank), BF16),
            jax.ShapeDtypeStruct((kv_rank + SUM_ROWS, n), BF16),
            jax.ShapeDtypeStruct((n, idx_dim), BF16),
            jax.ShapeDtypeStruct((n, LANE), F32),
            jax.ShapeDtypeStruct((MISC_ROWS, n), F32),
        ],
        compiler_params=_cparams(("parallel",)),
        name="in_proj",
    )(x2d, g.reshape(1, d), w_big, w_small, q_norm.reshape(1, -1), kv_norm.reshape(1, -1),
      ik_g.reshape(1, -1), ik_b.reshape(1, -1), neg_a_pad, dtb_pad)


def _tri_inverse(lmat, masks, eye):
    x = eye - lmat * masks[0]
    for mk in masks[1:]:
        lm = (lmat * mk).astype(BF16)
        t = _bdot(x.astype(BF16), lm)
        x = x - _bdot(t.astype(BF16), x.astype(BF16))
    return x


def _deltanet_kernel(q_ref, k_ref, v_ref, z_ref, cw_ref, bg_ref, mt_ref, on_ref, o_ref,
                     xpad_ref, act_ref, s_ref, *, nh, dk, ts, conv_k):
    c = DN_CHUNK
    cpi = ts // c
    i = pl.program_id(1)
    qk_w = nh * dk
    halo = 8

    @pl.when(i == 0)
    def _():
        xpad_ref[0:halo, :] = jnp.zeros((halo, 3 * qk_w), F32)
        s_ref[...] = jnp.zeros_like(s_ref)

    xpad_ref[halo:halo + ts, 0:qk_w] = q_ref[...].astype(F32)
    xpad_ref[halo:halo + ts, qk_w:2 * qk_w] = k_ref[...].astype(F32)
    xpad_ref[halo:halo + ts, 2 * qk_w:3 * qk_w] = v_ref[...].astype(F32)
    y = xpad_ref[halo:halo + ts, :] * cw_ref[conv_k - 1:conv_k, :]
    for dlt in range(1, conv_k):
        y = y + xpad_ref[halo - dlt:halo - dlt + ts, :] * cw_ref[conv_k - 1 - dlt:conv_k - dlt, :]
    act_ref[...] = _silu(y)
    xpad_ref[0:halo, :] = xpad_ref[ts:ts + halo, :]

    row = lax.broadcasted_iota(jnp.int32, (c, c), 0)
    col = lax.broadcasted_iota(jnp.int32, (c, c), 1)
    tril = row >= col
    strict = row > col
    eye = jnp.where(row == col, 1.0, 0.0).astype(F32)
    tril_f = jnp.where(tril, 1.0, 0.0).astype(F32)
    masks = []
    m = 1
    while m < c:
        mk = strict & ((row // (2 * m)) == (col // (2 * m))) & ((row // m) != (col // m))
        masks.append(jnp.where(mk, 1.0, 0.0).astype(F32))
        m *= 2
    onorm = on_ref[...]

    def heads(ref, r0, base):
        return [ref[r0:r0 + c, base + h * dk:base + (h + 1) * dk] for h in range(nh)]

    qs, ks, vs, betas, gccs, gcrs, glasts = [], [], [], [], [], [], []
    for cc in range(cpi):
        r0 = cc * c
        bg = bg_ref[r0:r0 + c, :]
        grow = mt_ref[2 * nh:3 * nh, r0:r0 + c]
        gc_col = jnp.dot(tril_f, bg, precision=lax.Precision.HIGHEST,
                         preferred_element_type=F32)
        gc_row = lax.dot_general(grow, tril_f, (((1,), (1,)), ((), ())),
                                 precision=lax.Precision.HIGHEST,
                                 preferred_element_type=F32)
        qs += heads(act_ref, r0, 0)
        ks += heads(act_ref, r0, qk_w)
        vs += heads(act_ref, r0, 2 * qk_w)
        betas += [bg[:, nh + h:nh + h + 1] for h in range(nh)]
        gccs += [gc_col[:, 2 * nh + h:2 * nh + h + 1] for h in range(nh)]
        gcrs += [gc_row[h:h + 1, :] for h in range(nh)]
        glasts += [gc_row[h:h + 1, c - 1:c] for h in range(nh)]
    q = jnp.stack(qs)
    k = jnp.stack(ks)
    v = jnp.stack(vs)
    beta = jnp.stack(betas)
    gcc = jnp.stack(gccs)
    gcr = jnp.stack(gcrs)
    g_last = jnp.stack(glasts)
    q = q * (lax.rsqrt(jnp.sum(q * q, axis=-1, keepdims=True) + EPS) * (dk ** -0.5))
    k = k * lax.rsqrt(jnp.sum(k * k, axis=-1, keepdims=True) + EPS)
    decay = jnp.where(tril, jnp.exp(jnp.minimum(gcc - gcr, 0.0)), 0.0)
    eg = jnp.exp(gcc)
    kb = k * beta
    vb = v * beta
    aq = _bdot_nt(jnp.concatenate([kb, q], axis=1).astype(BF16), k.astype(BF16))
    lmat = jnp.where(strict, aq[:, :c] * decay, 0.0)
    intra = (aq[:, c:] * decay).astype(BF16)
    tinv = _tri_inverse(lmat, masks, eye)
    uw = _bdot(tinv.astype(BF16), jnp.concatenate([vb, kb * eg], axis=2).astype(BF16))
    u = uw[:, :, :dk]
    wq = jnp.concatenate([uw[:, :, dk:], q * eg], axis=1).astype(BF16)
    kd = (k * jnp.exp(g_last - gcc)).astype(BF16)
    sdec = jnp.exp(g_last)
    s = s_ref[...]
    for cc in range(cpi):
        sl = slice(cc * nh, (cc + 1) * nh)
        m1 = _bdot(wq[sl], s.astype(BF16))
        vnb = (u[sl] - m1[:, :c]).astype(BF16)
        o = _rms(m1[:, c:] + _bdot(intra[sl], vnb), onorm)
        s = s * sdec[sl] + _bdot_tn(kd[sl], vnb)
        for h in range(nh):
            zz = z_ref[cc * c:(cc + 1) * c, h * dk:(h + 1) * dk].astype(F32)
            o_ref[cc * c:(cc + 1) * c, h * dk:(h + 1) * dk] = (o[h] * _silu(zz)).astype(o_ref.dtype)
    s_ref[...] = s


def _deltanet(big2d, conv_w, misc, misc_t, out_norm, *, batch, nh, dk, ts=256):
    n = big2d.shape[0]
    steps = n // batch // ts
    qk_w = nh * dk
    conv_k = conv_w.shape[0]
    tok = lambda cols, blk: pl.BlockSpec((ts, cols), lambda bi, i, blk=blk: (bi * steps + i, blk))
    return pl.pallas_call(
        functools.partial(_deltanet_kernel, nh=nh, dk=dk, ts=ts, conv_k=conv_k),
        grid=(batch, steps),
        in_specs=[
            tok(qk_w, 0), tok(qk_w, 1), tok(qk_w, 2), tok(qk_w, 3),
            _resident((conv_k, 3 * qk_w)),
            tok(LANE, 0),
            pl.BlockSpec((MISC_ROWS, ts), lambda bi, i: (0, bi * steps + i)),
            _resident((1, dk)),
        ],
        out_specs=tok(qk_w, 0),
        out_shape=jax.ShapeDtypeStruct((n, qk_w), BF16),
        scratch_shapes=[
            pltpu.VMEM((ts + 8, 3 * qk_w), F32),
            pltpu.VMEM((ts, 3 * qk_w), F32),
            pltpu.VMEM((nh, dk, dk), F32),
        ],
        compiler_params=_cparams(("arbitrary", "arbitrary")),
        name="deltanet",
    )(big2d, big2d, big2d, big2d, conv_w, misc, misc_t, out_norm.reshape(1, dk))


def _sortable(x):
    i = pltpu.bitcast(x, jnp.int32)
    return i ^ ((i >> 31) & jnp.int32(0x7FFFFFFF))


def _bit_transpose32(words):
    a = list(words)
    assert len(a) == 32
    j, m = 16, 0x0000FFFF
    while j:
        for k in range(32):
            if not k & j:
                t = (lax.shift_right_logical(a[k], jnp.int32(j)) ^ a[k + j]) & jnp.int32(m)
                a[k + j] = a[k + j] ^ t
                a[k] = a[k] ^ (t << j)
        j >>= 1
        m = (m ^ (m << j)) & 0xFFFFFFFF if j else m
    return a


def _dsa_kernel(cq_ref, mt_ref, ki_ref, ckv_ref, ckvt_ref, wiq_ref, wuq_ref, wuk_ref, wuv_ref,
                bias_ref, o_ref, key_ref, planes_ref, acc_ref, lg0_ref, lg1_ref, qidx_s, qlat_s, olat_s,
                *, nh_idx, idx_dim, nh, dh, topk, scale, tks, tka):
    tq = LANE
    sub = LANE
    j = pl.program_id(1)
    n_s = (j * tq + tq + tks - 1) // tks
    n_a = (j * tq + tq + tka - 1) // tka
    grp = cq_ref.shape[0] // tq
    jg = j % grp
    rows = lax.broadcasted_iota(jnp.int32, (sub, tq), 0)
    qpos = j * tq + lax.broadcasted_iota(jnp.int32, (sub, tq), 1)
    rows_s = lax.broadcasted_iota(jnp.int32, (tks, tq), 0)

    @pl.when(jg == 0)
    def _():
        cqg = cq_ref[...]
        qi = _dot_nt(wiq_ref[...], cqg).astype(BF16)
        qg = _dot_nt(wuq_ref[...], cqg).astype(BF16)
        for g in range(grp):
            qidx_s[g] = jnp.concatenate(
                [qi[h * idx_dim:(h + 1) * idx_dim, g * tq:(g + 1) * tq] for h in range(nh_idx)], axis=1)
        for h in range(nh):
            ql = (_dot(wuk_ref[h], qg[h * dh:(h + 1) * dh, :]) * (scale * LOG2E)).astype(BF16)
            for g in range(grp):
                qlat_s[g, :, h * tq:(h + 1) * tq] = ql[:, g * tq:(g + 1) * tq]

    wrow_all = jnp.concatenate([mt_ref[h:h + 1, :] for h in range(nh_idx)], axis=1)

    @pl.when((pl.program_id(0) == 0) & (j == 0))
    def _():
        planes_ref[...] = jnp.zeros_like(planes_ref)

    def score_tile(kt):
        words = []
        for r in range(tks // sub):
            s0 = pl.multiple_of(kt * tks + r * sub, sub)
            kk = ki_ref[pl.ds(s0, sub), :]
            d = jnp.maximum(_dot(kk, qidx_s[jg]), 0.0) * wrow_all
            parts = [d[:, h * tq:(h + 1) * tq] for h in range(nh_idx)]
            while len(parts) > 1:
                parts = [parts[a] + parts[a + 1] for a in range(0, len(parts), 2)]
            causal = (s0 + rows) <= qpos
            key = jnp.where(causal, _sortable(parts[0]), INT_MIN)
            key_ref[pl.ds(s0, sub), :] = key
            ukey = key ^ INT_MIN
            words += [ukey[8 * w:8 * (w + 1), :] for w in range(sub // 8)]
        for b, plane in enumerate(_bit_transpose32(words)):
            planes_ref[b, kt] = plane

    def score_body(t, carry):
        score_tile(2 * t)
        score_tile(2 * t + 1)
        return carry

    assert planes_ref.shape[1] % 2 == 0
    lax.fori_loop(0, (n_s + 1) // 2, score_body, 0)

    nt = planes_ref.shape[1]
    tile_ids = lax.broadcasted_iota(jnp.int32, (nt, 8, tq), 0)
    eq0 = jnp.where(tile_ids < n_s, -1, 0).astype(jnp.int32)

    def bit_body(bi, state):
        ans, cnt_gt, eq = state
        b = 31 - bi
        t = eq & planes_ref[b]
        c = jnp.sum(jnp.sum(lax.population_count(t), axis=0), axis=0, keepdims=True)
        accept = (cnt_gt + c) >= topk
        ans = jnp.where(accept, ans | jnp.left_shift(jnp.int32(1), b), ans)
        cnt_gt = jnp.where(accept, cnt_gt, cnt_gt + c)
        eq = jnp.where(accept, t, eq ^ t)
        return ans, cnt_gt, eq

    zero_row = jnp.zeros((1, tq), jnp.int32)
    ans_u, cnt_gt, eq = lax.fori_loop(0, 32, bit_body, (zero_row, zero_row, eq0))
    thr = jnp.maximum(ans_u ^ INT_MIN, INT_MIN + 1)

    n_eq = jnp.sum(jnp.sum(lax.population_count(eq), axis=0), axis=0, keepdims=True)
    any_tie = jnp.max(jnp.where((cnt_gt + n_eq > topk) & (ans_u != 0), 1, 0)) > 0

    def count_tiles(hit_fn):
        def body(kt, acc):
            s0 = pl.multiple_of(kt * tks, tks)
            hit = hit_fn(key_ref[pl.ds(s0, tks), :], s0)
            return acc + jnp.sum(hit.reshape(tks // 32, 4, 8, tq), axis=0)
        acc = lax.fori_loop(0, n_s, body, jnp.zeros((4, 8, tq), jnp.int32))
        return jnp.sum(acc.reshape(32, tq), axis=0, keepdims=True)

    def count_ge(cand):
        return count_tiles(lambda kv, s0: jnp.where(kv >= cand, 1, 0).astype(jnp.int32))

    @pl.when(any_tie)
    def _():
        need = topk - count_ge(thr + 1)

        def count_eq_below(lim):
            return count_tiles(lambda kv, s0: jnp.where((kv == thr) & ((s0 + rows_s) < lim), 1, 0)
                               .astype(jnp.int32))

        nbits = int(math.ceil(math.log2(key_ref.shape[0]))) + 1

        def jbit_body(bi, lim):
            cand = lim | jnp.left_shift(jnp.int32(1), nbits - 1 - bi)
            return jnp.where(count_eq_below(cand) <= need, cand, lim)

        lim = lax.fori_loop(0, nbits, jbit_body, jnp.zeros((1, tq), jnp.int32))

        def demote_body(kt, carry):
            s0 = pl.multiple_of(kt * tks, tks)
            kv = key_ref[pl.ds(s0, tks), :]
            drop = (kv == thr) & ((s0 + rows_s) >= lim)
            key_ref[pl.ds(s0, tks), :] = jnp.where(drop, INT_MIN, kv)
            return carry

        lax.fori_loop(0, n_s, demote_body, 0)

    acc_ref[...] = jnp.zeros_like(acc_ref)
    nsub = tka // sub

    last_tile = n_a - 1
    int_max = jnp.int32(2 ** 31 - 1)

    def logits_to(buf_ref, kt):
        s0 = pl.multiple_of(jnp.minimum(kt, last_tile) * tka, tka)
        buf_ref[...] = _dot(ckv_ref[pl.ds(s0, tka), :], qlat_s[jg]).astype(buf_ref.dtype)

    def consume(buf_ref, kt, m_run, near):
        ktc = jnp.minimum(kt, last_tile)
        s0 = pl.multiple_of(ktc * tka, tka)
        thr_v = jnp.where(kt <= last_tile, thr, int_max)
        parts = []
        for r in range(nsub):
            kv = key_ref[pl.ds(pl.multiple_of(s0 + r * sub, sub), sub), :]
            madd = jnp.where(kv >= thr_v, 0.0, NEG_BIG)
            if near:
                madd = jnp.concatenate([madd] * nh, axis=1) + bias_ref[jnp.clip(j - (ktc * nsub + r), 0, 2)]
                madd = madd.astype(BF16)
            else:
                madd = jnp.concatenate([madd.astype(BF16)] * nh, axis=1)
            parts.append(buf_ref[r * sub:(r + 1) * sub, :] + madd)
        lg = jnp.concatenate(parts, axis=0)
        tile_max = jnp.max(lg.reshape(tka // 16, 16, nh * tq), axis=0).astype(F32)
        m_new = jnp.maximum(m_run, jnp.max(tile_max, axis=0, keepdims=True))
        p = jnp.exp2(lg - m_new.astype(BF16))
        alpha = jnp.exp2(m_run - m_new)
        ct = ckvt_ref[:, pl.ds(s0, tka)]
        acc_ref[...] = acc_ref[...] * alpha + _dot(ct, p)
        return m_new

    def pair_body(t, m_run, near):
        kt = 2 * t
        logits_to(lg1_ref, kt + 1)
        m_run = consume(lg0_ref, kt, m_run, near)
        logits_to(lg0_ref, kt + 2)
        return consume(lg1_ref, kt + 1, m_run, near)

    m0 = jnp.full((1, nh * tq), NEG_BIG, F32).astype(BF16).astype(F32)
    n_pairs = (n_a + 1) // 2
    far_pairs = jnp.maximum(n_pairs - 2, 0)
    logits_to(lg0_ref, 0)
    m_run = lax.fori_loop(0, far_pairs, functools.partial(pair_body, near=False), m0)
    lax.fori_loop(far_pairs, n_pairs, functools.partial(pair_body, near=True), m_run)
    kvr = acc_ref.shape[0] - SUM_ROWS
    l_fin = acc_ref[kvr:kvr + 1, :]
    olat_s[jg] = (acc_ref[0:kvr, :] * (1.0 / l_fin)).astype(BF16)

    @pl.when(jg == grp - 1)
    def _():
        for h in range(nh):
            ol = jnp.concatenate([olat_s[g, :, h * tq:(h + 1) * tq] for g in range(grp)], axis=1)
            o_ref[:, h * dh:(h + 1) * dh] = _dot(wuv_ref[h], ol).T.astype(o_ref.dtype)


def _dsa(cq, misc_t, kidx, ckv, ckv_t, wiq_t, wuq_t, wuk_h, wuv_h, bias_tab, *, batch, topk, scale):
    n, q_rank = cq.shape
    s = n // batch
    kv_rank = ckv.shape[-1]
    idx_dim = kidx.shape[-1]
    nh_idx = wiq_t.shape[0] // idx_dim
    nh, dh, _ = wuv_h.shape
    tq = LANE
    nb = s // tq
    tks = 256
    tka = 256
    grp = 4
    assert s % tks == 0 and s % tka == 0 and nb % grp == 0
    return pl.pallas_call(
        functools.partial(_dsa_kernel, nh_idx=nh_idx, idx_dim=idx_dim, nh=nh, dh=dh, topk=topk,
                          scale=scale, tks=tks, tka=tka),
        grid=(batch, nb),
        in_specs=[
            pl.BlockSpec((grp * tq, q_rank), lambda bi, j: ((bi * nb + j) // grp, 0)),
            pl.BlockSpec((MISC_ROWS, tq), lambda bi, j: (0, bi * nb + j)),
            pl.BlockSpec((s, idx_dim), lambda bi, j: (bi, 0)),
            pl.BlockSpec((s, kv_rank), lambda bi, j: (bi, 0)),
            pl.BlockSpec((kv_rank + SUM_ROWS, s), lambda bi, j: (0, bi)),
            _resident(wiq_t.shape), _resident(wuq_t.shape), _resident(wuk_h.shape),
            _resident(wuv_h.shape), _resident(bias_tab.shape),
        ],
        out_specs=pl.BlockSpec((grp * tq, nh * dh), lambda bi, j: ((bi * nb + j) // grp, 0)),
        out_shape=jax.ShapeDtypeStruct((n, nh * dh), BF16),
        scratch_shapes=[
            pltpu.VMEM((s, tq), jnp.int32),
            pltpu.VMEM((32, s // tks, 8, tq), jnp.int32),
            pltpu.VMEM((kv_rank + SUM_ROWS, nh * tq), F32),
            pltpu.VMEM((tka, nh * tq), BF16),
            pltpu.VMEM((tka, nh * tq), BF16),
            pltpu.VMEM((grp, idx_dim, nh_idx * tq), BF16),
            pltpu.VMEM((grp, kv_rank, nh * tq), BF16),
            pltpu.VMEM((grp, kv_rank, nh * tq), BF16),
        ],
        compiler_params=_cparams(("arbitrary", "arbitrary")),
        name="dsa",
    )(cq, misc_t, kidx, ckv, ckv_t, wiq_t, wuq_t, wuk_h, wuv_h, bias_tab)


def _t5_bucket_np(n, n_buckets):
    max_exact = n_buckets // 2
    nf = np.maximum(n, 1).astype(np.float32)
    large = max_exact + (np.log(nf / max_exact) / math.log(REL_MAX_DIST / max_exact)
                         * (n_buckets - max_exact)).astype(np.int32)
    large = np.minimum(large, n_buckets - 1)
    return np.where(n < max_exact, n, large)


def _bias_table(rel_bias, tq):
    n_buckets, nh = rel_bias.shape
    assert _t5_bucket_np(np.array([tq + 1]), n_buckets)[0] == n_buckets - 1
    s_loc = np.arange(tq)[:, None]
    t_loc = np.arange(tq)[None, :]
    dist = np.stack([np.maximum(d * tq + t_loc - s_loc, 0) for d in range(3)])
    onehot = np.eye(n_buckets, dtype=np.float32)[_t5_bucket_np(dist, n_buckets)]
    tb = jnp.einsum('dstb,bh->dsht', jnp.asarray(onehot), rel_bias.astype(F32),
                    precision=lax.Precision.HIGHEST)
    far = rel_bias[n_buckets - 1].astype(F32)[None, None, :, None]
    return ((tb - far) * LOG2E).reshape(3, tq, nh * tq)


def kernel(x, ffn1_norm, ffn1_wg, ffn1_wu, ffn1_wd, mix_norm, w_in, conv_w, a_log, dt_bias, dn_out_norm, q_norm, kv_norm, w_uq, w_uk, w_uv, w_iq, idx_k_g, idx_k_b, rel_bias, w_o, ffn2_norm, ffn2_wg, ffn2_wu, ffn2_wd, final_norm):
    b, s, d = x.shape
    depth = ffn1_norm.shape[0]
    nh_dn = a_log.shape[1]
    dk = dn_out_norm.shape[1]
    qk_w = nh_dn * dk
    q_rank = q_norm.shape[1]
    kv_rank = kv_norm.shape[1]
    idx_dim = idx_k_g.shape[1]
    nh_idx = w_iq.shape[2] // idx_dim
    nh_sa, dh_sa = w_uk.shape[2], w_uk.shape[3]
    assert qk_w == d and nh_sa * dh_sa == d and nh_idx == nh_dn and 3 * nh_dn <= MISC_ROWS
    topk = min(TOPK_MAX, s // 4)
    n = b * s

    sizes = (3 * qk_w, qk_w, nh_dn, nh_dn, q_rank, kv_rank, idx_dim, nh_idx, d, d)
    offs = np.concatenate([[0], np.cumsum(sizes)])
    (o_qkv, o_z, o_b, o_a, o_cq, o_ckv, o_ik, o_iw, o_ga, o_gb) = [int(v) for v in offs[:-1]]
    l_cq, l_ckv, l_ik, l_misc, small_cols = _small_layout(q_rank, kv_rank, idx_dim)
    bias_tab = _bias_table(rel_bias, LANE)

    x2d = x.reshape(n, d)
    for l in range(depth):
        x1 = _ffn(x2d, ffn1_norm[l], ffn1_wg[l], ffn1_wu[l], ffn1_wd[l])

        wl = w_in[l]
        w_big = jnp.concatenate([wl[:, o_qkv:o_qkv + 4 * qk_w], wl[:, o_ga:o_ga + 2 * d]],
                                axis=1).astype(BF16)
        w_small = jnp.zeros((d, small_cols), F32)
        w_small = w_small.at[:, l_cq:l_cq + q_rank].set(wl[:, o_cq:o_cq + q_rank])
        w_small = w_small.at[:, l_ckv:l_ckv + kv_rank].set(wl[:, o_ckv:o_ckv + kv_rank])
        w_small = w_small.at[:, l_ik:l_ik + idx_dim].set(wl[:, o_ik:o_ik + idx_dim])
        w_small = w_small.at[:, l_misc:l_misc + nh_idx].set(wl[:, o_iw:o_iw + nh_idx])
        w_small = w_small.at[:, l_misc + nh_dn:l_misc + 2 * nh_dn].set(wl[:, o_b:o_b + nh_dn])
        w_small = w_small.at[:, l_misc + 2 * nh_dn:l_misc + 3 * nh_dn].set(wl[:, o_a:o_a + nh_dn])
        neg_a_pad = jnp.zeros((1, LANE), F32).at[0, 2 * nh_dn:3 * nh_dn].set(-jnp.exp(a_log[l].astype(F32)))
        dtb_pad = jnp.zeros((1, LANE), F32).at[0, 2 * nh_dn:3 * nh_dn].set(dt_bias[l].astype(F32))
        big2d, cq, ckv, ckv_t, kidx, misc, misc_t = _in_proj(
            x1, mix_norm[l], w_big, w_small.astype(BF16), q_norm[l], kv_norm[l], idx_k_g[l],
            idx_k_b[l], neg_a_pad, dtb_pad, q_rank=q_rank, kv_rank=kv_rank, idx_dim=idx_dim,
            nh=nh_dn, w_scale=(nh_idx ** -0.5) * (idx_dim ** -0.5))

        o_dn = _deltanet(big2d, conv_w[l], misc, misc_t, dn_out_norm[l], batch=b, nh=nh_dn, dk=dk)

        o_sa = _dsa(cq, misc_t, kidx, ckv, ckv_t,
                    w_iq[l].T.astype(BF16), w_uq[l].T.astype(BF16),
                    jnp.transpose(w_uk[l], (1, 0, 2)).astype(BF16),
                    jnp.transpose(w_uv[l], (1, 2, 0)).astype(BF16),
                    bias_tab, batch=b, topk=topk, scale=dh_sa ** -0.5)

        x2d = _ffn(x1, ffn2_norm[l], ffn2_wg[l], ffn2_wu[l], ffn2_wd[l],
                   final_g=final_norm if l == depth - 1 else None,
                   merge=(big2d, 4, o_dn, o_sa, w_o[l]))
    return x2d.reshape(b, s, d)
```

```python
import functools
import math

import numpy as np
import jax
import jax.numpy as jnp
from jax import lax
from jax.experimental import pallas as pl
from jax.experimental.pallas import tpu as pltpu

F32 = jnp.float32
BF16 = jnp.bfloat16
EPS = 1e-6
TOPK_MAX = 256
REL_MAX_DIST = 128
DN_CHUNK = 64
LANE = 128
SUM_ROWS = 8
MISC_ROWS = 32
INT_MIN = -(2 ** 31)
NEG_BIG = -1e30
LOG2E = math.log2(math.e)
VMEM_LIMIT = 56 * 1024 * 1024


def _cparams(sem):
    return pltpu.CompilerParams(dimension_semantics=sem, vmem_limit_bytes=VMEM_LIMIT)


def _resident(shape):
    return pl.BlockSpec(shape, lambda *_: (0,) * len(shape), pipeline_mode=pl.Buffered(1))


def _rms(x, g):
    return x * lax.rsqrt(jnp.mean(x * x, axis=-1, keepdims=True) + EPS) * g


def _silu(x):
    return x * jax.nn.sigmoid(x)


def _dot(a, b):
    return jnp.dot(a, b, preferred_element_type=F32)


def _dot_nt(a, b):
    return lax.dot_general(a, b, (((1,), (1,)), ((), ())), preferred_element_type=F32)


def _bdot(a, b):
    return lax.dot_general(a, b, (((2,), (1,)), ((0,), (0,))), preferred_element_type=F32)


def _bdot_nt(a, b):
    return lax.dot_general(a, b, (((2,), (2,)), ((0,), (0,))), preferred_element_type=F32)


def _bdot_tn(a, b):
    return lax.dot_general(a, b, (((1,), (1,)), ((0,), (0,))), preferred_element_type=F32)


def _ffn_kernel(*refs, final, merge, ff_chunk):
    refs = list(refs)
    x_ref = refs.pop(0)
    if merge:
        ga_ref, gb_ref, oa_ref, ob_ref, wo_ref = refs[:5]
        refs = refs[5:]
    g_ref, wg_ref, wu_ref, wd_ref = refs[:4]
    refs = refs[4:]
    if final:
        fg_ref = refs.pop(0)
    o_ref, act_ref = refs

    x = x_ref[...]
    if merge:
        merged = (jax.nn.sigmoid(ga_ref[...].astype(F32)) * oa_ref[...].astype(F32)
                  + jax.nn.sigmoid(gb_ref[...].astype(F32)) * ob_ref[...].astype(F32))
        x = x + _dot(merged.astype(BF16), wo_ref[...])
    h = _rms(x, g_ref[...]).astype(BF16)
    ff = wg_ref.shape[1]
    for c0 in range(0, ff, ff_chunk):
        c1 = min(c0 + ff_chunk, ff)
        a = _dot(h, wg_ref[:, c0:c1])
        u = _dot(h, wu_ref[:, c0:c1])
        act_ref[:, c0:c1] = (_silu(a) * u).astype(BF16)
    y = x + 0.5 * _dot(act_ref[...], wd_ref[...])
    if final:
        y = _rms(y, fg_ref[...])
    o_ref[...] = y


def _ffn(x2d, g, wg, wu, wd, *, final_g=None, merge=None, tm=512, ff_chunk=256):
    n, d = x2d.shape
    ff = wg.shape[1]
    row = lambda cols, blk=0: pl.BlockSpec((tm, cols), lambda i, blk=blk: (i, blk))
    in_specs = [row(d)]
    args = [x2d]
    if merge is not None:
        big2d, gate_blk, o_dn, o_sa, w_o = merge
        in_specs += [row(d, gate_blk), row(d, gate_blk + 1), row(d), row(d), _resident((d, d))]
        args += [big2d, big2d, o_dn, o_sa, w_o.astype(BF16)]
    in_specs += [_resident((1, d)), _resident((d, ff)), _resident((d, ff)), _resident((ff, d))]
    args += [g.reshape(1, d), wg.astype(BF16), wu.astype(BF16), wd.astype(BF16)]
    if final_g is not None:
        in_specs.append(_resident((1, d)))
        args.append(final_g.reshape(1, d))
    return pl.pallas_call(
        functools.partial(_ffn_kernel, final=final_g is not None, merge=merge is not None,
                          ff_chunk=ff_chunk),
        grid=(n // tm,),
        in_specs=in_specs,
        out_specs=row(d),
        out_shape=jax.ShapeDtypeStruct((n, d), F32),
        scratch_shapes=[pltpu.VMEM((tm, ff), BF16)],
        compiler_params=_cparams(("parallel",)),
        name="ffn_merge" if merge is not None else "ffn",
    )(*args)


def _small_layout(q_rank, kv_rank, idx_dim):
    o_cq = 0
    o_ckv = o_cq + q_rank
    o_ik = o_ckv + kv_rank
    o_misc = o_ik + LANE * ((idx_dim + LANE - 1) // LANE)
    return o_cq, o_ckv, o_ik, o_misc, o_misc + LANE


def _tri_inverse(lmat, masks, eye):
    x = eye - lmat * masks[0]
    for mk in masks[1:]:
        lm = (lmat * mk).astype(BF16)
        t = _bdot(x.astype(BF16), lm)
        x = x - _bdot(t.astype(BF16), x.astype(BF16))
    return x


def _dn_consts(c):
    row = lax.broadcasted_iota(jnp.int32, (c, c), 0)
    col = lax.broadcasted_iota(jnp.int32, (c, c), 1)
    tril = row >= col
    strict = row > col
    eye = jnp.where(row == col, 1.0, 0.0).astype(F32)
    tril_f = jnp.where(tril, 1.0, 0.0).astype(F32)
    masks = []
    m = 1
    while m < c:
        mk = strict & ((row // (2 * m)) == (col // (2 * m))) & ((row // m) != (col // m))
        masks.append(jnp.where(mk, 1.0, 0.0).astype(F32))
        m *= 2
    return tril, strict, eye, tril_f, masks


def _dn_conv(in_ref, rb, ts, cw_ref, xpad_ref, act_ref, keep, cols):
    halo = 8
    conv_k = cw_ref.shape[0]
    if keep is not None:
        xpad_ref[0:halo, cols] = xpad_ref[0:halo, cols] * keep
    xpad_ref[halo:halo + ts, cols] = in_ref[rb:rb + ts, cols].astype(F32)
    y = xpad_ref[halo:halo + ts, cols] * cw_ref[conv_k - 1:conv_k, cols]
    for dlt in range(1, conv_k):
        y = y + xpad_ref[halo - dlt:halo - dlt + ts, cols] * cw_ref[conv_k - 1 - dlt:conv_k - dlt, cols]
    act_ref[:, cols] = _silu(y)
    xpad_ref[0:halo, cols] = xpad_ref[ts:ts + halo, cols]


def _dn_stages(in_ref, bg_ref, mt_ref, rb, ts, act_ref, onorm, s, o_ref, consts, *, nh, dk):
    c = DN_CHUNK
    cpi = ts // c
    qk_w = nh * dk
    tril, strict, eye, tril_f, masks = consts

    def heads(r0, base):
        return [act_ref[r0:r0 + c, base + h * dk:base + (h + 1) * dk] for h in range(nh)]

    qs, ks, vs, betas, gccs, gcrs, glasts = [], [], [], [], [], [], []
    for cc in range(cpi):
        r0 = cc * c
        bg = bg_ref[rb + r0:rb + r0 + c, :]
        grow = mt_ref[2 * nh:3 * nh, rb + r0:rb + r0 + c]
        gc_col = jnp.dot(tril_f, bg, precision=lax.Precision.HIGHEST,
                         preferred_element_type=F32)
        gc_row = lax.dot_general(grow, tril_f, (((1,), (1,)), ((), ())),
                                 precision=lax.Precision.HIGHEST,
                                 preferred_element_type=F32)
        qs += heads(r0, 0)
        ks += heads(r0, qk_w)
        vs += heads(r0, 2 * qk_w)
        betas += [bg[:, nh + h:nh + h + 1] for h in range(nh)]
        gccs += [gc_col[:, 2 * nh + h:2 * nh + h + 1] for h in range(nh)]
        gcrs += [gc_row[h:h + 1, :] for h in range(nh)]
        glasts += [gc_row[h:h + 1, c - 1:c] for h in range(nh)]
    q = jnp.stack(qs)
    k = jnp.stack(ks)
    v = jnp.stack(vs)
    beta = jnp.stack(betas)
    gcc = jnp.stack(gccs)
    gcr = jnp.stack(gcrs)
    g_last = jnp.stack(glasts)
    q = q * (lax.rsqrt(jnp.sum(q * q, axis=-1, keepdims=True) + EPS) * (dk ** -0.5))
    k = k * lax.rsqrt(jnp.sum(k * k, axis=-1, keepdims=True) + EPS)
    decay = jnp.where(tril, jnp.exp(jnp.minimum(gcc - gcr, 0.0)), 0.0)
    eg = jnp.exp(gcc)
    kb = k * beta
    vb = v * beta
    aq = _bdot_nt(jnp.concatenate([kb, q], axis=1).astype(BF16), k.astype(BF16))
    lmat = jnp.where(strict, aq[:, :c] * decay, 0.0)
    intra = (aq[:, c:] * decay).astype(BF16)
    tinv = _tri_inverse(lmat, masks, eye)
    uw = _bdot(tinv.astype(BF16), jnp.concatenate([vb, kb * eg], axis=2).astype(BF16))
    u = uw[:, :, :dk]
    wq = jnp.concatenate([uw[:, :, dk:], q * eg], axis=1).astype(BF16)
    kd = (k * jnp.exp(g_last - gcc)).astype(BF16)
    sdec = jnp.exp(g_last)
    for cc in range(cpi):
        sl = slice(cc * nh, (cc + 1) * nh)
        m1 = _bdot(wq[sl], s.astype(BF16))
        vnb = (u[sl] - m1[:, :c]).astype(BF16)
        o = _rms(m1[:, c:] + _bdot(intra[sl], vnb), onorm)
        s = s * sdec[sl] + _bdot_tn(kd[sl], vnb)
        for h in range(nh):
            rows = slice(rb + cc * c, rb + (cc + 1) * c)
            zz = in_ref[rows, 3 * qk_w + h * dk:3 * qk_w + (h + 1) * dk].astype(F32)
            o_ref[rows, h * dk:(h + 1) * dk] = (o[h] * _silu(zz)).astype(o_ref.dtype)
    return s


def _inproj_dn_kernel(x_ref, g_ref, wb_ref, ws_ref, qn_ref, kvn_ref, ikg_ref, ikb_ref, negA_ref, dtb_ref,
                      cw_ref, on_ref,
                      gates_ref, cq_ref, ckv_ref, ckvt_ref, ik_ref, misct_ref, odn_ref,
                      qkvz_s, misc_s, mt_s, dn_in, dn_misc, dn_mt, xpad_ref, act_ref, s_ref,
                      *, q_rank, kv_rank, idx_dim, nh, dk, w_scale, col_chunk, seq, ts):
    o_cq, o_ckv, o_ik, o_misc, _ = _small_layout(q_rank, kv_rank, idx_dim)
    i = pl.program_id(0)
    tm = x_ref.shape[0]
    qkvz_w = qkvz_s.shape[1]

    @pl.when(i == 0)
    def _():
        qkvz_s[...] = jnp.zeros_like(qkvz_s)
        misc_s[...] = jnp.zeros_like(misc_s)
        mt_s[...] = jnp.zeros_like(mt_s)
        xpad_ref[0:8, :] = jnp.zeros((8, xpad_ref.shape[1]), F32)
        s_ref[...] = jnp.zeros_like(s_ref)

    dn_in[...] = qkvz_s[...]
    dn_misc[...] = misc_s[...]
    dn_mt[...] = mt_s[...]
    keep = jnp.where(((i - 1) * tm) % seq == 0, 0.0, 1.0)
    consts = _dn_consts(DN_CHUNK)
    onorm = on_ref[...]

    h = _rms(x_ref[...], g_ref[...]).astype(BF16)
    cols = wb_ref.shape[1]

    def proj_chunk(c0):
        c1 = min(c0 + col_chunk, cols)
        p = _dot(h, wb_ref[:, c0:c1]).astype(BF16)
        if c1 <= qkvz_w:
            qkvz_s[:, c0:c1] = p
        else:
            assert c0 >= qkvz_w
            gates_ref[:, c0 - qkvz_w:c1 - qkvz_w] = p

    def proj_small():
        p = _dot(h, ws_ref[...])
        cq_ref[...] = _rms(p[:, o_cq:o_cq + q_rank], qn_ref[...]).astype(cq_ref.dtype)
        ckv = _rms(p[:, o_ckv:o_ckv + kv_rank], kvn_ref[...])
        ckv_ref[...] = ckv.astype(ckv_ref.dtype)
        ones = jnp.ones((SUM_ROWS, ckv.shape[0]), F32)
        ckvt_ref[...] = jnp.concatenate([ckv.T, ones], axis=0).astype(ckvt_ref.dtype)
        ik = p[:, o_ik:o_ik + idx_dim]
        mu = jnp.mean(ik, axis=-1, keepdims=True)
        var = jnp.mean(jnp.square(ik - mu), axis=-1, keepdims=True)
        ik_ref[...] = ((ik - mu) * lax.rsqrt(var + EPS) * ikg_ref[...] + ikb_ref[...]).astype(ik_ref.dtype)
        m = p[:, o_misc:o_misc + LANE]
        lane = lax.broadcasted_iota(jnp.int32, m.shape, 1)
        beta = jax.nn.sigmoid(m)
        sp_in = m + dtb_ref[...]
        softplus = jnp.maximum(sp_in, 0.0) + jnp.log(1.0 + jnp.exp(-jnp.abs(sp_in)))
        gdec = negA_ref[...] * softplus
        misc = jnp.where(lane < nh, m * w_scale,
                         jnp.where(lane < 2 * nh, beta, jnp.where(lane < 3 * nh, gdec, 0.0)))
        misc_s[...] = misc
        mt = misc.T[0:MISC_ROWS, :]
        mt_s[...] = mt
        misct_ref[...] = mt

    chunks = list(range(0, cols, col_chunk))
    n_sub = tm // ts
    s = s_ref[...] * keep
    ci = 0
    for st in range(n_sub):
        for k0 in range(0, cw_ref.shape[1], col_chunk):
            _dn_conv(dn_in, st * ts, ts, cw_ref, xpad_ref, act_ref, keep if st == 0 else None,
                     slice(k0, k0 + col_chunk))
            if ci < len(chunks):
                proj_chunk(chunks[ci])
                ci += 1
        s = _dn_stages(dn_in, dn_misc, dn_mt, st * ts, ts, act_ref, onorm, s, odn_ref, consts, nh=nh, dk=dk)
    for c0 in chunks[ci:]:
        proj_chunk(c0)
    proj_small()
    s_ref[...] = s


def _in_proj_deltanet(x2d, g, w_big, w_small, q_norm, kv_norm, ik_g, ik_b, neg_a_pad, dtb_pad, conv_w,
                      out_norm, *, seq, q_rank, kv_rank, idx_dim, nh, dk, w_scale, tm=512, ts=256,
                      col_chunk=512):
    n, d = x2d.shape
    cols = w_big.shape[1]
    qk_w = nh * dk
    qkvz_w = 4 * qk_w
    gate_w = cols - qkvz_w
    nt = n // tm
    assert seq % tm == 0 and tm % ts == 0 and qkvz_w % col_chunk == 0
    cur = lambda i: jnp.minimum(i, nt - 1)
    prev = lambda i: jnp.maximum(i - 1, 0)
    row = lambda c: pl.BlockSpec((tm, c), lambda i: (cur(i), 0))
    colblk = lambda r: pl.BlockSpec((r, tm), lambda i: (0, cur(i)))
    return pl.pallas_call(
        functools.partial(_inproj_dn_kernel, q_rank=q_rank, kv_rank=kv_rank, idx_dim=idx_dim, nh=nh,
                          dk=dk, w_scale=w_scale, col_chunk=col_chunk, seq=seq, ts=ts),
        grid=(nt + 1,),
        in_specs=[
            row(d), _resident((1, d)), _resident((d, cols)), _resident((d, w_small.shape[1])),
            _resident((1, q_rank)), _resident((1, kv_rank)), _resident((1, idx_dim)),
            _resident((1, idx_dim)), _resident((1, LANE)), _resident((1, LANE)),
            _resident(conv_w.shape), _resident((1, dk)),
        ],
        out_specs=[row(gate_w), row(q_rank), row(kv_rank), colblk(kv_rank + SUM_ROWS), row(idx_dim),
                   colblk(MISC_ROWS), pl.BlockSpec((tm, qk_w), lambda i: (prev(i), 0))],
        out_shape=[
            jax.ShapeDtypeStruct((n, gate_w), BF16),
            jax.ShapeDtypeStruct((n, q_rank), BF16),
            jax.ShapeDtypeStruct((n, kv_rank), BF16),
            jax.ShapeDtypeStruct((kv_rank + SUM_ROWS, n), BF16),
            jax.ShapeDtypeStruct((n, idx_dim), BF16),
            jax.ShapeDtypeStruct((MISC_ROWS, n), F32),
            jax.ShapeDtypeStruct((n, qk_w), BF16),
        ],
        scratch_shapes=[
            pltpu.VMEM((tm, qkvz_w), BF16), pltpu.VMEM((tm, LANE), F32), pltpu.VMEM((MISC_ROWS, tm), F32),
            pltpu.VMEM((tm, qkvz_w), BF16), pltpu.VMEM((tm, LANE), F32), pltpu.VMEM((MISC_ROWS, tm), F32),
            pltpu.VMEM((ts + 8, 3 * qk_w), F32),
            pltpu.VMEM((ts, 3 * qk_w), F32),
            pltpu.VMEM((nh, dk, dk), F32),
        ],
        compiler_params=_cparams(("arbitrary",)),
        name="in_proj_deltanet",
    )(x2d, g.reshape(1, d), w_big, w_small, q_norm.reshape(1, -1), kv_norm.reshape(1, -1),
      ik_g.reshape(1, -1), ik_b.reshape(1, -1), neg_a_pad, dtb_pad, conv_w, out_norm.reshape(1, dk))


def _sortable(x):
    i = pltpu.bitcast(x, jnp.int32)
    return i ^ ((i >> 31) & jnp.int32(0x7FFFFFFF))


def _bit_transpose32(words):
    a = list(words)
    assert len(a) == 32
    j, m = 16, 0x0000FFFF
    while j:
        for k in range(32):
            if not k & j:
                t = (lax.shift_right_logical(a[k], jnp.int32(j)) ^ a[k + j]) & jnp.int32(m)
                a[k + j] = a[k + j] ^ t
                a[k] = a[k] ^ (t << j)
        j >>= 1
        m = (m ^ (m << j)) & 0xFFFFFFFF if j else m
    return a


def _dsa_kernel(cq_ref, mt_ref, ki_ref, ckv_ref, ckvt_ref, wiq_ref, wuq_ref, wuk_ref, wuv_ref,
                bias_ref, o_ref, key_ref, planes_ref, acc_ref, lg0_ref, lg1_ref, qidx_s, qlat_s, olat_s,
                *, nh_idx, idx_dim, nh, dh, topk, scale, tks, tka):
    tq = LANE
    sub = LANE
    j = pl.program_id(1)
    n_s = (j * tq + tq + tks - 1) // tks
    n_a = (j * tq + tq + tka - 1) // tka
    grp = cq_ref.shape[0] // tq
    jg = j % grp
    rows = lax.broadcasted_iota(jnp.int32, (sub, tq), 0)
    qpos = j * tq + lax.broadcasted_iota(jnp.int32, (sub, tq), 1)
    rows_s = lax.broadcasted_iota(jnp.int32, (tks, tq), 0)

    @pl.when(jg == 0)
    def _():
        cqg = cq_ref[...]
        qi = _dot_nt(wiq_ref[...], cqg).astype(BF16)
        qg = _dot_nt(wuq_ref[...], cqg).astype(BF16)
        for g in range(grp):
            qidx_s[g] = jnp.concatenate(
                [qi[h * idx_dim:(h + 1) * idx_dim, g * tq:(g + 1) * tq] for h in range(nh_idx)], axis=1)
        for h in range(nh):
            ql = (_dot(wuk_ref[h], qg[h * dh:(h + 1) * dh, :]) * (scale * LOG2E)).astype(BF16)
            for g in range(grp):
                qlat_s[g, :, h * tq:(h + 1) * tq] = ql[:, g * tq:(g + 1) * tq]

    wrow_all = jnp.concatenate([mt_ref[h:h + 1, :] for h in range(nh_idx)], axis=1)

    @pl.when((pl.program_id(0) == 0) & (j == 0))
    def _():
        planes_ref[...] = jnp.zeros_like(planes_ref)

    def score_tile(kt):
        words = []
        for r in range(tks // sub):
            s0 = pl.multiple_of(kt * tks + r * sub, sub)
            kk = ki_ref[pl.ds(s0, sub), :]
            d = jnp.maximum(_dot(kk, qidx_s[jg]), 0.0) * wrow_all
            parts = [d[:, h * tq:(h + 1) * tq] for h in range(nh_idx)]
            while len(parts) > 1:
                parts = [parts[a] + parts[a + 1] for a in range(0, len(parts), 2)]
            causal = (s0 + rows) <= qpos
            key = jnp.where(causal, _sortable(parts[0]), INT_MIN)
            key_ref[pl.ds(s0, sub), :] = key
            ukey = key ^ INT_MIN
            words += [ukey[8 * w:8 * (w + 1), :] for w in range(sub // 8)]
        for b, plane in enumerate(_bit_transpose32(words)):
            planes_ref[b, kt] = plane

    def score_body(t, carry):
        score_tile(2 * t)
        score_tile(2 * t + 1)
        return carry

    assert planes_ref.shape[1] % 2 == 0
    lax.fori_loop(0, (n_s + 1) // 2, score_body, 0)

    nt = planes_ref.shape[1]
    tile_ids = lax.broadcasted_iota(jnp.int32, (nt, 8, tq), 0)
    eq0 = jnp.where(tile_ids < n_s, -1, 0).astype(jnp.int32)

    def bit_body(bi, state):
        ans, cnt_gt, eq = state
        b = 31 - bi
        t = eq & planes_ref[b]
        c = jnp.sum(jnp.sum(lax.population_count(t), axis=0), axis=0, keepdims=True)
        accept = (cnt_gt + c) >= topk
        ans = jnp.where(accept, ans | jnp.left_shift(jnp.int32(1), b), ans)
        cnt_gt = jnp.where(accept, cnt_gt, cnt_gt + c)
        eq = jnp.where(accept, t, eq ^ t)
        return ans, cnt_gt, eq

    zero_row = jnp.zeros((1, tq), jnp.int32)
    ans_u, cnt_gt, eq = lax.fori_loop(0, 32, bit_body, (zero_row, zero_row, eq0))
    thr = jnp.maximum(ans_u ^ INT_MIN, INT_MIN + 1)

    n_eq = jnp.sum(jnp.sum(lax.population_count(eq), axis=0), axis=0, keepdims=True)
    any_tie = jnp.max(jnp.where((cnt_gt + n_eq > topk) & (ans_u != 0), 1, 0)) > 0

    def count_tiles(hit_fn):
        def body(kt, acc):
            s0 = pl.multiple_of(kt * tks, tks)
            hit = hit_fn(key_ref[pl.ds(s0, tks), :], s0)
            return acc + jnp.sum(hit.reshape(tks // 32, 4, 8, tq), axis=0)
        acc = lax.fori_loop(0, n_s, body, jnp.zeros((4, 8, tq), jnp.int32))
        return jnp.sum(acc.reshape(32, tq), axis=0, keepdims=True)

    def count_ge(cand):
        return count_tiles(lambda kv, s0: jnp.where(kv >= cand, 1, 0).astype(jnp.int32))

    @pl.when(any_tie)
    def _():
        need = topk - count_ge(thr + 1)

        def count_eq_below(lim):
            return count_tiles(lambda kv, s0: jnp.where((kv == thr) & ((s0 + rows_s) < lim), 1, 0)
                               .astype(jnp.int32))

        nbits = int(math.ceil(math.log2(key_ref.shape[0]))) + 1

        def jbit_body(bi, lim):
            cand = lim | jnp.left_shift(jnp.int32(1), nbits - 1 - bi)
            return jnp.where(count_eq_below(cand) <= need, cand, lim)

        lim = lax.fori_loop(0, nbits, jbit_body, jnp.zeros((1, tq), jnp.int32))

        def demote_body(kt, carry):
            s0 = pl.multiple_of(kt * tks, tks)
            kv = key_ref[pl.ds(s0, tks), :]
            drop = (kv == thr) & ((s0 + rows_s) >= lim)
            key_ref[pl.ds(s0, tks), :] = jnp.where(drop, INT_MIN, kv)
            return carry

        lax.fori_loop(0, n_s, demote_body, 0)

    acc_ref[...] = jnp.zeros_like(acc_ref)
    nsub = tka // sub

    last_tile = n_a - 1
    int_max = jnp.int32(2 ** 31 - 1)

    def logits_to(buf_ref, kt):
        s0 = pl.multiple_of(jnp.minimum(kt, last_tile) * tka, tka)
        buf_ref[...] = _dot(ckv_ref[pl.ds(s0, tka), :], qlat_s[jg]).astype(buf_ref.dtype)

    def consume(buf_ref, kt, m_run, near):
        ktc = jnp.minimum(kt, last_tile)
        s0 = pl.multiple_of(ktc * tka, tka)
        thr_v = jnp.where(kt <= last_tile, thr, int_max)
        parts = []
        for r in range(nsub):
            kv = key_ref[pl.ds(pl.multiple_of(s0 + r * sub, sub), sub), :]
            madd = jnp.where(kv >= thr_v, 0.0, NEG_BIG)
            if near:
                madd = jnp.concatenate([madd] * nh, axis=1) + bias_ref[jnp.clip(j - (ktc * nsub + r), 0, 2)]
                madd = madd.astype(BF16)
            else:
                madd = jnp.concatenate([madd.astype(BF16)] * nh, axis=1)
            parts.append(buf_ref[r * sub:(r + 1) * sub, :] + madd)
        lg = jnp.concatenate(parts, axis=0)
        tile_max = jnp.max(lg.reshape(tka // 16, 16, nh * tq), axis=0).astype(F32)
        m_new = jnp.maximum(m_run, jnp.max(tile_max, axis=0, keepdims=True))
        p = jnp.exp2(lg - m_new.astype(BF16))
        alpha = jnp.exp2(m_run - m_new)
        ct = ckvt_ref[:, pl.ds(s0, tka)]
        acc_ref[...] = acc_ref[...] * alpha + _dot(ct, p)
        return m_new

    def pair_body(t, m_run, near):
        kt = 2 * t
        logits_to(lg1_ref, kt + 1)
        m_run = consume(lg0_ref, kt, m_run, near)
        logits_to(lg0_ref, kt + 2)
        return consume(lg1_ref, kt + 1, m_run, near)

    m0 = jnp.full((1, nh * tq), NEG_BIG, F32).astype(BF16).astype(F32)
    n_pairs = (n_a + 1) // 2
    far_pairs = jnp.maximum(n_pairs - 2, 0)
    logits_to(lg0_ref, 0)
    m_run = lax.fori_loop(0, far_pairs, functools.partial(pair_body, near=False), m0)
    lax.fori_loop(far_pairs, n_pairs, functools.partial(pair_body, near=True), m_run)
    kvr = acc_ref.shape[0] - SUM_ROWS
    l_fin = acc_ref[kvr:kvr + 1, :]
    olat_s[jg] = (acc_ref[0:kvr, :] * (1.0 / l_fin)).astype(BF16)

    @pl.when(jg == grp - 1)
    def _():
        for h in range(nh):
            ol = jnp.concatenate([olat_s[g, :, h * tq:(h + 1) * tq] for g in range(grp)], axis=1)
            o_ref[:, h * dh:(h + 1) * dh] = _dot(wuv_ref[h], ol).T.astype(o_ref.dtype)


def _dsa(cq, misc_t, kidx, ckv, ckv_t, wiq_t, wuq_t, wuk_h, wuv_h, bias_tab, *, batch, topk, scale):
    n, q_rank = cq.shape
    s = n // batch
    kv_rank = ckv.shape[-1]
    idx_dim = kidx.shape[-1]
    nh_idx = wiq_t.shape[0] // idx_dim
    nh, dh, _ = wuv_h.shape
    tq = LANE
    nb = s // tq
    tks = 256
    tka = 256
    grp = 4
    assert s % tks == 0 and s % tka == 0 and nb % grp == 0
    return pl.pallas_call(
        functools.partial(_dsa_kernel, nh_idx=nh_idx, idx_dim=idx_dim, nh=nh, dh=dh, topk=topk,
                          scale=scale, tks=tks, tka=tka),
        grid=(batch, nb),
        in_specs=[
            pl.BlockSpec((grp * tq, q_rank), lambda bi, j: ((bi * nb + j) // grp, 0)),
            pl.BlockSpec((MISC_ROWS, tq), lambda bi, j: (0, bi * nb + j)),
            pl.BlockSpec((s, idx_dim), lambda bi, j: (bi, 0)),
            pl.BlockSpec((s, kv_rank), lambda bi, j: (bi, 0)),
            pl.BlockSpec((kv_rank + SUM_ROWS, s), lambda bi, j: (0, bi)),
            _resident(wiq_t.shape), _resident(wuq_t.shape), _resident(wuk_h.shape),
            _resident(wuv_h.shape), _resident(bias_tab.shape),
        ],
        out_specs=pl.BlockSpec((grp * tq, nh * dh), lambda bi, j: ((bi * nb + j) // grp, 0)),
        out_shape=jax.ShapeDtypeStruct((n, nh * dh), BF16),
        scratch_shapes=[
            pltpu.VMEM((s, tq), jnp.int32),
            pltpu.VMEM((32, s // tks, 8, tq), jnp.int32),
            pltpu.VMEM((kv_rank + SUM_ROWS, nh * tq), F32),
            pltpu.VMEM((tka, nh * tq), BF16),
            pltpu.VMEM((tka, nh * tq), BF16),
            pltpu.VMEM((grp, idx_dim, nh_idx * tq), BF16),
            pltpu.VMEM((grp, kv_rank, nh * tq), BF16),
            pltpu.VMEM((grp, kv_rank, nh * tq), BF16),
        ],
        compiler_params=_cparams(("arbitrary", "arbitrary")),
        name="dsa",
    )(cq, misc_t, kidx, ckv, ckv_t, wiq_t, wuq_t, wuk_h, wuv_h, bias_tab)


def _t5_bucket_np(n, n_buckets):
    max_exact = n_buckets // 2
    nf = np.maximum(n, 1).astype(np.float32)
    large = max_exact + (np.log(nf / max_exact) / math.log(REL_MAX_DIST / max_exact)
                         * (n_buckets - max_exact)).astype(np.int32)
    large = np.minimum(large, n_buckets - 1)
    return np.where(n < max_exact, n, large)


def _bias_table(rel_bias, tq):
    n_buckets, nh = rel_bias.shape
    assert _t5_bucket_np(np.array([tq + 1]), n_buckets)[0] == n_buckets - 1
    s_loc = np.arange(tq)[:, None]
    t_loc = np.arange(tq)[None, :]
    dist = np.stack([np.maximum(d * tq + t_loc - s_loc, 0) for d in range(3)])
    onehot = np.eye(n_buckets, dtype=np.float32)[_t5_bucket_np(dist, n_buckets)]
    tb = jnp.einsum('dstb,bh->dsht', jnp.asarray(onehot), rel_bias.astype(F32),
                    precision=lax.Precision.HIGHEST)
    far = rel_bias[n_buckets - 1].astype(F32)[None, None, :, None]
    return ((tb - far) * LOG2E).reshape(3, tq, nh * tq)


def kernel(x, ffn1_norm, ffn1_wg, ffn1_wu, ffn1_wd, mix_norm, w_in, conv_w, a_log, dt_bias, dn_out_norm, q_norm, kv_norm, w_uq, w_uk, w_uv, w_iq, idx_k_g, idx_k_b, rel_bias, w_o, ffn2_norm, ffn2_wg, ffn2_wu, ffn2_wd, final_norm):
    b, s, d = x.shape
    depth = ffn1_norm.shape[0]
    nh_dn = a_log.shape[1]
    dk = dn_out_norm.shape[1]
    qk_w = nh_dn * dk
    q_rank = q_norm.shape[1]
    kv_rank = kv_norm.shape[1]
    idx_dim = idx_k_g.shape[1]
    nh_idx = w_iq.shape[2] // idx_dim
    nh_sa, dh_sa = w_uk.shape[2], w_uk.shape[3]
    assert qk_w == d and nh_sa * dh_sa == d and nh_idx == nh_dn and 3 * nh_dn <= MISC_ROWS
    topk = min(TOPK_MAX, s // 4)
    n = b * s

    sizes = (3 * qk_w, qk_w, nh_dn, nh_dn, q_rank, kv_rank, idx_dim, nh_idx, d, d)
    offs = np.concatenate([[0], np.cumsum(sizes)])
    (o_qkv, o_z, o_b, o_a, o_cq, o_ckv, o_ik, o_iw, o_ga, o_gb) = [int(v) for v in offs[:-1]]
    l_cq, l_ckv, l_ik, l_misc, small_cols = _small_layout(q_rank, kv_rank, idx_dim)
    bias_tab = _bias_table(rel_bias, LANE)

    x2d = x.reshape(n, d)
    for l in range(depth):
        x1 = _ffn(x2d, ffn1_norm[l], ffn1_wg[l], ffn1_wu[l], ffn1_wd[l])

        wl = w_in[l]
        w_big = jnp.concatenate([wl[:, o_qkv:o_qkv + 4 * qk_w], wl[:, o_ga:o_ga + 2 * d]],
                                axis=1).astype(BF16)
        w_small = jnp.zeros((d, small_cols), F32)
        w_small = w_small.at[:, l_cq:l_cq + q_rank].set(wl[:, o_cq:o_cq + q_rank])
        w_small = w_small.at[:, l_ckv:l_ckv + kv_rank].set(wl[:, o_ckv:o_ckv + kv_rank])
        w_small = w_small.at[:, l_ik:l_ik + idx_dim].set(wl[:, o_ik:o_ik + idx_dim])
        w_small = w_small.at[:, l_misc:l_misc + nh_idx].set(wl[:, o_iw:o_iw + nh_idx])
        w_small = w_small.at[:, l_misc + nh_dn:l_misc + 2 * nh_dn].set(wl[:, o_b:o_b + nh_dn])
        w_small = w_small.at[:, l_misc + 2 * nh_dn:l_misc + 3 * nh_dn].set(wl[:, o_a:o_a + nh_dn])
        neg_a_pad = jnp.zeros((1, LANE), F32).at[0, 2 * nh_dn:3 * nh_dn].set(-jnp.exp(a_log[l].astype(F32)))
        dtb_pad = jnp.zeros((1, LANE), F32).at[0, 2 * nh_dn:3 * nh_dn].set(dt_bias[l].astype(F32))
        gates, cq, ckv, ckv_t, kidx, misc_t, o_dn = _in_proj_deltanet(
            x1, mix_norm[l], w_big, w_small.astype(BF16), q_norm[l], kv_norm[l], idx_k_g[l],
            idx_k_b[l], neg_a_pad, dtb_pad, conv_w[l].astype(F32), dn_out_norm[l], seq=s,
            q_rank=q_rank, kv_rank=kv_rank, idx_dim=idx_dim, nh=nh_dn, dk=dk,
            w_scale=(nh_idx ** -0.5) * (idx_dim ** -0.5))

        o_sa = _dsa(cq, misc_t, kidx, ckv, ckv_t,
                    w_iq[l].T.astype(BF16), w_uq[l].T.astype(BF16),
                    jnp.transpose(w_uk[l], (1, 0, 2)).astype(BF16),
                    jnp.transpose(w_uv[l], (1, 2, 0)).astype(BF16),
                    bias_tab, batch=b, topk=topk, scale=dh_sa ** -0.5)

        x2d = _ffn(x1, ffn2_norm[l], ffn2_wg[l], ffn2_wu[l], ffn2_wd[l],
                   final_g=final_norm if l == depth - 1 else None,
                   merge=(gates, 0, o_dn, o_sa, w_o[l]))
    return x2d.reshape(b, s, d)
```

```python
import functools
import math

import numpy as np
import jax
import jax.numpy as jnp
from jax import lax
from jax.experimental import pallas as pl
from jax.experimental.pallas import tpu as pltpu

F32 = jnp.float32
BF16 = jnp.bfloat16
EPS = 1e-6
TOPK_MAX = 256
REL_MAX_DIST = 128
DN_CHUNK = 64
LANE = 128
SUM_ROWS = 8
MISC_ROWS = 32
INT_MIN = -(2 ** 31)
NEG_BIG = -1e30
LOG2E = math.log2(math.e)
VMEM_LIMIT = 56 * 1024 * 1024


def _cparams(sem):
    return pltpu.CompilerParams(dimension_semantics=sem, vmem_limit_bytes=VMEM_LIMIT)


def _resident(shape):
    return pl.BlockSpec(shape, lambda *_: (0,) * len(shape), pipeline_mode=pl.Buffered(1))


def _rms(x, g):
    return x * lax.rsqrt(jnp.mean(x * x, axis=-1, keepdims=True) + EPS) * g


def _silu(x):
    return x * jax.nn.sigmoid(x)


def _dot(a, b):
    return jnp.dot(a, b, preferred_element_type=F32)


def _dot_nt(a, b):
    return lax.dot_general(a, b, (((1,), (1,)), ((), ())), preferred_element_type=F32)


def _bdot(a, b):
    return lax.dot_general(a, b, (((2,), (1,)), ((0,), (0,))), preferred_element_type=F32)


def _bdot_nt(a, b):
    return lax.dot_general(a, b, (((2,), (2,)), ((0,), (0,))), preferred_element_type=F32)


def _bdot_tn(a, b):
    return lax.dot_general(a, b, (((1,), (1,)), ((0,), (0,))), preferred_element_type=F32)


def _ffn_kernel(*refs, final, merge, ff_chunk):
    refs = list(refs)
    x_ref = refs.pop(0)
    if merge:
        ga_ref, gb_ref, oa_ref, ob_ref, wo_ref = refs[:5]
        refs = refs[5:]
    g_ref, wg_ref, wu_ref, wd_ref = refs[:4]
    refs = refs[4:]
    if final:
        fg_ref = refs.pop(0)
    o_ref, act_ref = refs

    x = x_ref[...]
    if merge:
        merged = (jax.nn.sigmoid(ga_ref[...].astype(F32)) * oa_ref[...].astype(F32)
                  + jax.nn.sigmoid(gb_ref[...].astype(F32)) * ob_ref[...].astype(F32))
        x = x + _dot(merged.astype(BF16), wo_ref[...])
    h = _rms(x, g_ref[...]).astype(BF16)
    ff = wg_ref.shape[1]
    for c0 in range(0, ff, ff_chunk):
        c1 = min(c0 + ff_chunk, ff)
        a = _dot(h, wg_ref[:, c0:c1])
        u = _dot(h, wu_ref[:, c0:c1])
        act_ref[:, c0:c1] = (_silu(a) * u).astype(BF16)
    y = x + 0.5 * _dot(act_ref[...], wd_ref[...])
    if final:
        y = _rms(y, fg_ref[...])
    o_ref[...] = y


def _ffn(x2d, g, wg, wu, wd, *, final_g=None, merge=None, tm=512, ff_chunk=256):
    n, d = x2d.shape
    ff = wg.shape[1]
    row = lambda cols, blk=0: pl.BlockSpec((tm, cols), lambda i, blk=blk: (i, blk))
    in_specs = [row(d)]
    args = [x2d]
    if merge is not None:
        big2d, gate_blk, o_dn, o_sa, w_o = merge
        in_specs += [row(d, gate_blk), row(d, gate_blk + 1), row(d), row(d), _resident((d, d))]
        args += [big2d, big2d, o_dn, o_sa, w_o.astype(BF16)]
    in_specs += [_resident((1, d)), _resident((d, ff)), _resident((d, ff)), _resident((ff, d))]
    args += [g.reshape(1, d), wg.astype(BF16), wu.astype(BF16), wd.astype(BF16)]
    if final_g is not None:
        in_specs.append(_resident((1, d)))
        args.append(final_g.reshape(1, d))
    return pl.pallas_call(
        functools.partial(_ffn_kernel, final=final_g is not None, merge=merge is not None,
                          ff_chunk=ff_chunk),
        grid=(n // tm,),
        in_specs=in_specs,
        out_specs=row(d),
        out_shape=jax.ShapeDtypeStruct((n, d), F32),
        scratch_shapes=[pltpu.VMEM((tm, ff), BF16)],
        compiler_params=_cparams(("parallel",)),
        name="ffn_merge" if merge is not None else "ffn",
    )(*args)


def _small_layout(q_rank, kv_rank, idx_dim):
    o_cq = 0
    o_ckv = o_cq + q_rank
    o_ik = o_ckv + kv_rank
    o_misc = o_ik + LANE * ((idx_dim + LANE - 1) // LANE)
    return o_cq, o_ckv, o_ik, o_misc, o_misc + LANE


def _tri_inverse(lmat, masks, eye):
    x = eye - lmat * masks[0]
    for mk in masks[1:]:
        lm = (lmat * mk).astype(BF16)
        t = _bdot(x.astype(BF16), lm)
        x = x - _bdot(t.astype(BF16), x.astype(BF16))
    return x


def _dn_consts(c):
    row = lax.broadcasted_iota(jnp.int32, (c, c), 0)
    col = lax.broadcasted_iota(jnp.int32, (c, c), 1)
    tril = row >= col
    strict = row > col
    eye = jnp.where(row == col, 1.0, 0.0).astype(F32)
    tril_f = jnp.where(tril, 1.0, 0.0).astype(F32)
    masks = []
    m = 1
    while m < c:
        mk = strict & ((row // (2 * m)) == (col // (2 * m))) & ((row // m) != (col // m))
        masks.append(jnp.where(mk, 1.0, 0.0).astype(F32))
        m *= 2
    return tril, strict, eye, tril_f, masks


def _dn_conv(in_ref, rb, ts, cw_ref, xpad_ref, act_ref, keep, cols):
    halo = 8
    conv_k = cw_ref.shape[0]
    if keep is not None:
        xpad_ref[0:halo, cols] = xpad_ref[0:halo, cols] * keep
    xpad_ref[halo:halo + ts, cols] = in_ref[rb:rb + ts, cols].astype(F32)
    y = xpad_ref[halo:halo + ts, cols] * cw_ref[conv_k - 1:conv_k, cols]
    for dlt in range(1, conv_k):
        y = y + xpad_ref[halo - dlt:halo - dlt + ts, cols] * cw_ref[conv_k - 1 - dlt:conv_k - dlt, cols]
    act_ref[:, cols] = _silu(y)
    xpad_ref[0:halo, cols] = xpad_ref[ts:ts + halo, cols]


def _dn_stages(in_ref, bg_ref, mt_ref, rb, ts, act_ref, onorm, s, o_ref, consts, *, nh, dk):
    c = DN_CHUNK
    cpi = ts // c
    qk_w = nh * dk
    tril, strict, eye, tril_f, masks = consts

    def heads(r0, base):
        return [act_ref[r0:r0 + c, base + h * dk:base + (h + 1) * dk] for h in range(nh)]

    qs, ks, vs, betas, gccs, gcrs, glasts = [], [], [], [], [], [], []
    for cc in range(cpi):
        r0 = cc * c
        bg = bg_ref[rb + r0:rb + r0 + c, :]
        grow = mt_ref[2 * nh:3 * nh, rb + r0:rb + r0 + c]
        gc_col = jnp.dot(tril_f, bg, precision=lax.Precision.HIGHEST,
                         preferred_element_type=F32)
        gc_row = lax.dot_general(grow, tril_f, (((1,), (1,)), ((), ())),
                                 precision=lax.Precision.HIGHEST,
                                 preferred_element_type=F32)
        qs += heads(r0, 0)
        ks += heads(r0, qk_w)
        vs += heads(r0, 2 * qk_w)
        betas += [bg[:, nh + h:nh + h + 1] for h in range(nh)]
        gccs += [gc_col[:, 2 * nh + h:2 * nh + h + 1] for h in range(nh)]
        gcrs += [gc_row[h:h + 1, :] for h in range(nh)]
        glasts += [gc_row[h:h + 1, c - 1:c] for h in range(nh)]
    q = jnp.stack(qs)
    k = jnp.stack(ks)
    v = jnp.stack(vs)
    beta = jnp.stack(betas)
    gcc = jnp.stack(gccs)
    gcr = jnp.stack(gcrs)
    g_last = jnp.stack(glasts)
    q = q * (lax.rsqrt(jnp.sum(q * q, axis=-1, keepdims=True) + EPS) * (dk ** -0.5))
    k = k * lax.rsqrt(jnp.sum(k * k, axis=-1, keepdims=True) + EPS)
    decay = jnp.where(tril, jnp.exp(jnp.minimum(gcc - gcr, 0.0)), 0.0)
    eg = jnp.exp(gcc)
    kb = k * beta
    vb = v * beta
    aq = _bdot_nt(jnp.concatenate([kb, q], axis=1).astype(BF16), k.astype(BF16))
    lmat = jnp.where(strict, aq[:, :c] * decay, 0.0)
    intra = (aq[:, c:] * decay).astype(BF16)
    tinv = _tri_inverse(lmat, masks, eye)
    uw = _bdot(tinv.astype(BF16), jnp.concatenate([vb, kb * eg], axis=2).astype(BF16))
    u = uw[:, :, :dk]
    wq = jnp.concatenate([uw[:, :, dk:], q * eg], axis=1).astype(BF16)
    kd = (k * jnp.exp(g_last - gcc)).astype(BF16)
    sdec = jnp.exp(g_last)
    for cc in range(cpi):
        sl = slice(cc * nh, (cc + 1) * nh)
        m1 = _bdot(wq[sl], s.astype(BF16))
        vnb = (u[sl] - m1[:, :c]).astype(BF16)
        o = _rms(m1[:, c:] + _bdot(intra[sl], vnb), onorm)
        s = s * sdec[sl] + _bdot_tn(kd[sl], vnb)
        for h in range(nh):
            rows = slice(rb + cc * c, rb + (cc + 1) * c)
            zz = in_ref[rows, 3 * qk_w + h * dk:3 * qk_w + (h + 1) * dk].astype(F32)
            o_ref[rows, h * dk:(h + 1) * dk] = (o[h] * _silu(zz)).astype(o_ref.dtype)
    return s


def _inproj_dn_kernel(x_ref, g_ref, wb_ref, ws_ref, qn_ref, kvn_ref, ikg_ref, ikb_ref, negA_ref, dtb_ref,
                      cw_ref, on_ref,
                      gates_ref, cq_ref, ckv_ref, ckvt_ref, ik_ref, misct_ref, odn_ref,
                      qkvz_s, misc_s, mt_s, dn_in, dn_misc, dn_mt, xpad_ref, act_ref, s_ref,
                      *, q_rank, kv_rank, idx_dim, nh, dk, w_scale, col_chunk, seq, ts):
    o_cq, o_ckv, o_ik, o_misc, _ = _small_layout(q_rank, kv_rank, idx_dim)
    i = pl.program_id(0)
    tm = x_ref.shape[0]
    qkvz_w = qkvz_s.shape[1]

    @pl.when(i == 0)
    def _():
        qkvz_s[...] = jnp.zeros_like(qkvz_s)
        misc_s[...] = jnp.zeros_like(misc_s)
        mt_s[...] = jnp.zeros_like(mt_s)
        xpad_ref[0:8, :] = jnp.zeros((8, xpad_ref.shape[1]), F32)
        s_ref[...] = jnp.zeros_like(s_ref)

    dn_in[...] = qkvz_s[...]
    dn_misc[...] = misc_s[...]
    dn_mt[...] = mt_s[...]
    keep = jnp.where(((i - 1) * tm) % seq == 0, 0.0, 1.0)
    consts = _dn_consts(DN_CHUNK)
    onorm = on_ref[...]

    h = _rms(x_ref[...], g_ref[...]).astype(BF16)
    cols = wb_ref.shape[1]

    def proj_chunk(c0):
        c1 = min(c0 + col_chunk, cols)
        p = _dot(h, wb_ref[:, c0:c1]).astype(BF16)
        if c1 <= qkvz_w:
            qkvz_s[:, c0:c1] = p
        else:
            assert c0 >= qkvz_w
            gates_ref[:, c0 - qkvz_w:c1 - qkvz_w] = p

    def proj_small():
        p = _dot(h, ws_ref[...])
        cq_ref[...] = _rms(p[:, o_cq:o_cq + q_rank], qn_ref[...]).astype(cq_ref.dtype)
        ckv = _rms(p[:, o_ckv:o_ckv + kv_rank], kvn_ref[...])
        ckv_ref[...] = ckv.astype(ckv_ref.dtype)
        ones = jnp.ones((SUM_ROWS, ckv.shape[0]), F32)
        ckvt_ref[...] = jnp.concatenate([ckv.T, ones], axis=0).astype(ckvt_ref.dtype)
        ik = p[:, o_ik:o_ik + idx_dim]
        mu = jnp.mean(ik, axis=-1, keepdims=True)
        var = jnp.mean(jnp.square(ik - mu), axis=-1, keepdims=True)
        ik_ref[...] = ((ik - mu) * lax.rsqrt(var + EPS) * ikg_ref[...] + ikb_ref[...]).astype(ik_ref.dtype)
        m = p[:, o_misc:o_misc + LANE]
        lane = lax.broadcasted_iota(jnp.int32, m.shape, 1)
        beta = jax.nn.sigmoid(m)
        sp_in = m + dtb_ref[...]
        softplus = jnp.maximum(sp_in, 0.0) + jnp.log(1.0 + jnp.exp(-jnp.abs(sp_in)))
        gdec = negA_ref[...] * softplus
        misc = jnp.where(lane < nh, m * w_scale,
                         jnp.where(lane < 2 * nh, beta, jnp.where(lane < 3 * nh, gdec, 0.0)))
        misc_s[...] = misc
        mt = misc.T[0:MISC_ROWS, :]
        mt_s[...] = mt
        misct_ref[...] = mt

    chunks = list(range(0, cols, col_chunk))
    n_sub = tm // ts
    s = s_ref[...] * keep
    ci = 0
    for st in range(n_sub):
        for k0 in range(0, cw_ref.shape[1], col_chunk):
            _dn_conv(dn_in, st * ts, ts, cw_ref, xpad_ref, act_ref, keep if st == 0 else None,
                     slice(k0, k0 + col_chunk))
            if ci < len(chunks):
                proj_chunk(chunks[ci])
                ci += 1
        s = _dn_stages(dn_in, dn_misc, dn_mt, st * ts, ts, act_ref, onorm, s, odn_ref, consts, nh=nh, dk=dk)
    for c0 in chunks[ci:]:
        proj_chunk(c0)
    proj_small()
    s_ref[...] = s


def _in_proj_deltanet(x2d, g, w_big, w_small, q_norm, kv_norm, ik_g, ik_b, neg_a_pad, dtb_pad, conv_w,
                      out_norm, *, seq, q_rank, kv_rank, idx_dim, nh, dk, w_scale, tm=512, ts=256,
                      col_chunk=512):
    n, d = x2d.shape
    cols = w_big.shape[1]
    qk_w = nh * dk
    qkvz_w = 4 * qk_w
    gate_w = cols - qkvz_w
    nt = n // tm
    assert seq % tm == 0 and tm % ts == 0 and qkvz_w % col_chunk == 0
    cur = lambda i: jnp.minimum(i, nt - 1)
    prev = lambda i: jnp.maximum(i - 1, 0)
    row = lambda c: pl.BlockSpec((tm, c), lambda i: (cur(i), 0))
    colblk = lambda r: pl.BlockSpec((r, tm), lambda i: (0, cur(i)))
    return pl.pallas_call(
        functools.partial(_inproj_dn_kernel, q_rank=q_rank, kv_rank=kv_rank, idx_dim=idx_dim, nh=nh,
                          dk=dk, w_scale=w_scale, col_chunk=col_chunk, seq=seq, ts=ts),
        grid=(nt + 1,),
        in_specs=[
            row(d), _resident((1, d)), _resident((d, cols)), _resident((d, w_small.shape[1])),
            _resident((1, q_rank)), _resident((1, kv_rank)), _resident((1, idx_dim)),
            _resident((1, idx_dim)), _resident((1, LANE)), _resident((1, LANE)),
            _resident(conv_w.shape), _resident((1, dk)),
        ],
        out_specs=[row(gate_w), row(q_rank), row(kv_rank), colblk(kv_rank + SUM_ROWS), row(idx_dim),
                   colblk(MISC_ROWS), pl.BlockSpec((tm, qk_w), lambda i: (prev(i), 0))],
        out_shape=[
            jax.ShapeDtypeStruct((n, gate_w), BF16),
            jax.ShapeDtypeStruct((n, q_rank), BF16),
            jax.ShapeDtypeStruct((n, kv_rank), BF16),
            jax.ShapeDtypeStruct((kv_rank + SUM_ROWS, n), BF16),
            jax.ShapeDtypeStruct((n, idx_dim), BF16),
            jax.ShapeDtypeStruct((MISC_ROWS, n), F32),
            jax.ShapeDtypeStruct((n, qk_w), BF16),
        ],
        scratch_shapes=[
            pltpu.VMEM((tm, qkvz_w), BF16), pltpu.VMEM((tm, LANE), F32), pltpu.VMEM((MISC_ROWS, tm), F32),
            pltpu.VMEM((tm, qkvz_w), BF16), pltpu.VMEM((tm, LANE), F32), pltpu.VMEM((MISC_ROWS, tm), F32),
            pltpu.VMEM((ts + 8, 3 * qk_w), F32),
            pltpu.VMEM((ts, 3 * qk_w), F32),
            pltpu.VMEM((nh, dk, dk), F32),
        ],
        compiler_params=_cparams(("arbitrary",)),
        name="in_proj_deltanet",
    )(x2d, g.reshape(1, d), w_big, w_small, q_norm.reshape(1, -1), kv_norm.reshape(1, -1),
      ik_g.reshape(1, -1), ik_b.reshape(1, -1), neg_a_pad, dtb_pad, conv_w, out_norm.reshape(1, dk))


def _sortable(x):
    i = pltpu.bitcast(x, jnp.int32)
    return i ^ ((i >> 31) & jnp.int32(0x7FFFFFFF))


def _bit_transpose32(words):
    a = list(words)
    assert len(a) == 32
    j, m = 16, 0x0000FFFF
    while j:
        for k in range(32):
            if not k & j:
                t = (lax.shift_right_logical(a[k], jnp.int32(j)) ^ a[k + j]) & jnp.int32(m)
                a[k + j] = a[k + j] ^ t
                a[k] = a[k] ^ (t << j)
        j >>= 1
        m = (m ^ (m << j)) & 0xFFFFFFFF if j else m
    return a


def _dsa_kernel(cq_ref, mtc_ref, mtn_ref, ki_ref, ckv_ref, ckvt_ref, wiq_ref, wuq_ref, wuk_ref, wuv_ref,
                bias_ref, o_ref, key_ref, keyn_ref, planes_ref, thr_s, acc_ref, lg0_ref, lg1_ref,
                qidx_s, qlat_s, olat_s, *, nh_idx, idx_dim, nh, dh, topk, scale, tks, tka, grp):
    tq = LANE
    sub = LANE
    j = pl.program_id(1)
    nb = pl.num_programs(1)
    jg = j % grp
    rows = lax.broadcasted_iota(jnp.int32, (sub, tq), 0)
    cols = lax.broadcasted_iota(jnp.int32, (sub, tq), 1)
    rows_s = lax.broadcasted_iota(jnp.int32, (tks, tq), 0)
    nt = planes_ref.shape[1]
    assert nt % 2 == 0 and tks == tka

    def n_tiles(blk):
        return (blk * tq + tq + tks - 1) // tks

    def project_group(first_slot):
        cqg = cq_ref[...]
        qi = _dot_nt(wiq_ref[...], cqg).astype(BF16)
        qg = _dot_nt(wuq_ref[...], cqg).astype(BF16)
        for g in range(grp):
            qidx_s[first_slot + g] = jnp.concatenate(
                [qi[h * idx_dim:(h + 1) * idx_dim, g * tq:(g + 1) * tq] for h in range(nh_idx)], axis=1)
        for h in range(nh):
            ql = (_dot(wuk_ref[h], qg[h * dh:(h + 1) * dh, :]) * (scale * LOG2E)).astype(BF16)
            for g in range(grp):
                qlat_s[first_slot + g, :, h * tq:(h + 1) * tq] = ql[:, g * tq:(g + 1) * tq]

    @pl.when((pl.program_id(0) == 0) & (j == 0))
    def _():
        planes_ref[...] = jnp.zeros_like(planes_ref)

    @pl.when(j == 0)
    def _():
        project_group(0)

    @pl.when((j > 0) & ((j + 1) % grp == 0) & (j + 1 < nb))
    def _():
        project_group((((j + 1) // grp) % 2) * grp)

    def head_weights(mt_ref):
        return jnp.concatenate([mt_ref[h:h + 1, :] for h in range(nh_idx)], axis=1)

    def score_tile(kt, blk, kref, wrow_all):
        slot = blk % (2 * grp)
        qpos = blk * tq + cols
        words = []
        for r in range(tks // sub):
            s0 = pl.multiple_of(kt * tks + r * sub, sub)
            kk = ki_ref[pl.ds(s0, sub), :]
            d = jnp.maximum(_dot(kk, qidx_s[slot]), 0.0) * wrow_all
            parts = [d[:, h * tq:(h + 1) * tq] for h in range(nh_idx)]
            while len(parts) > 1:
                parts = [parts[a] + parts[a + 1] for a in range(0, len(parts), 2)]
            causal = (s0 + rows) <= qpos
            key = jnp.where(causal, _sortable(parts[0]), INT_MIN)
            kref[pl.ds(s0, sub), :] = key
            ukey = key ^ INT_MIN
            words += [ukey[8 * w:8 * (w + 1), :] for w in range(sub // 8)]
        for b, plane in enumerate(_bit_transpose32(words)):
            planes_ref[b, kt] = plane

    def score_pair(t, blk, kref, wrow_all):
        score_tile(2 * t, blk, kref, wrow_all)
        score_tile(2 * t + 1, blk, kref, wrow_all)

    def select(blk, kref):
        n_s = n_tiles(blk)
        tile_ids = lax.broadcasted_iota(jnp.int32, (nt, 8, tq), 0)
        eq0 = jnp.where(tile_ids < n_s, -1, 0).astype(jnp.int32)

        def bit_body(bi, state):
            ans, cnt_gt, eq = state
            b = 31 - bi
            t = eq & planes_ref[b]
            c = jnp.sum(jnp.sum(lax.population_count(t), axis=0), axis=0, keepdims=True)
            accept = (cnt_gt + c) >= topk
            ans = jnp.where(accept, ans | jnp.left_shift(jnp.int32(1), b), ans)
            cnt_gt = jnp.where(accept, cnt_gt, cnt_gt + c)
            eq = jnp.where(accept, t, eq ^ t)
            return ans, cnt_gt, eq

        zero_row = jnp.zeros((1, tq), jnp.int32)
        ans_u, cnt_gt, eq = lax.fori_loop(0, 32, bit_body, (zero_row, zero_row, eq0))
        thr = jnp.maximum(ans_u ^ INT_MIN, INT_MIN + 1)

        n_eq = jnp.sum(jnp.sum(lax.population_count(eq), axis=0), axis=0, keepdims=True)
        any_tie = jnp.max(jnp.where((cnt_gt + n_eq > topk) & (ans_u != 0), 1, 0)) > 0

        def count_tiles(hit_fn):
            def body(kt, acc):
                s0 = pl.multiple_of(kt * tks, tks)
                hit = hit_fn(kref[pl.ds(s0, tks), :], s0)
                return acc + jnp.sum(hit.reshape(tks // 32, 4, 8, tq), axis=0)
            acc = lax.fori_loop(0, n_s, body, jnp.zeros((4, 8, tq), jnp.int32))
            return jnp.sum(acc.reshape(32, tq), axis=0, keepdims=True)

        @pl.when(any_tie)
        def _():
            n_gt = count_tiles(lambda kv, s0: jnp.where(kv > thr, 1, 0).astype(jnp.int32))
            need = topk - n_gt

            def count_eq_below(lim):
                return count_tiles(lambda kv, s0: jnp.where((kv == thr) & ((s0 + rows_s) < lim), 1, 0)
                                   .astype(jnp.int32))

            nbits = int(math.ceil(math.log2(key_ref.shape[0]))) + 1

            def jbit_body(bi, lim):
                cand = lim | jnp.left_shift(jnp.int32(1), nbits - 1 - bi)
                return jnp.where(count_eq_below(cand) <= need, cand, lim)

            lim = lax.fori_loop(0, nbits, jbit_body, jnp.zeros((1, tq), jnp.int32))

            def demote_body(kt, carry):
                s0 = pl.multiple_of(kt * tks, tks)
                kv = kref[pl.ds(s0, tks), :]
                drop = (kv == thr) & ((s0 + rows_s) >= lim)
                kref[pl.ds(s0, tks), :] = jnp.where(drop, INT_MIN, kv)
                return carry

            lax.fori_loop(0, n_s, demote_body, 0)

        thr_s[0:1, :] = thr

    @pl.when(j == 0)
    def _():
        w0 = head_weights(mtc_ref)
        lax.fori_loop(0, (n_tiles(0) + 1) // 2, lambda t, c: (score_pair(t, 0, key_ref, w0), c)[1], 0)
        select(0, key_ref)

    @pl.when(j > 0)
    def _():
        def copy_body(kt, carry):
            s0 = pl.multiple_of(kt * tks, tks)
            key_ref[pl.ds(s0, tks), :] = keyn_ref[pl.ds(s0, tks), :]
            return carry
        lax.fori_loop(0, n_tiles(j), copy_body, 0)

    thr = thr_s[0:1, :]

    acc_ref[...] = jnp.zeros_like(acc_ref)
    nsub = tka // sub
    n_a = n_tiles(j)
    last_tile = n_a - 1
    int_max = jnp.int32(2 ** 31 - 1)
    qslot = j % (2 * grp)

    def logits_to(buf_ref, kt):
        s0 = pl.multiple_of(jnp.minimum(kt, last_tile) * tka, tka)
        buf_ref[...] = _dot(ckv_ref[pl.ds(s0, tka), :], qlat_s[qslot]).astype(buf_ref.dtype)

    def consume(buf_ref, kt, m_run, near):
        ktc = jnp.minimum(kt, last_tile)
        s0 = pl.multiple_of(ktc * tka, tka)
        thr_v = jnp.where(kt <= last_tile, thr, int_max)
        parts = []
        for r in range(nsub):
            kv = key_ref[pl.ds(pl.multiple_of(s0 + r * sub, sub), sub), :]
            madd = jnp.where(kv >= thr_v, 0.0, NEG_BIG)
            if near:
                madd = jnp.concatenate([madd] * nh, axis=1) + bias_ref[jnp.clip(j - (ktc * nsub + r), 0, 2)]
                madd = madd.astype(BF16)
            else:
                madd = jnp.concatenate([madd.astype(BF16)] * nh, axis=1)
            parts.append(buf_ref[r * sub:(r + 1) * sub, :] + madd)
        lg = jnp.concatenate(parts, axis=0)
        tile_max = jnp.max(lg.reshape(tka // 16, 16, nh * tq), axis=0).astype(F32)
        m_new = jnp.maximum(m_run, jnp.max(tile_max, axis=0, keepdims=True))
        p = jnp.exp2(lg - m_new.astype(BF16))
        alpha = jnp.exp2(m_run - m_new)
        ct = ckvt_ref[:, pl.ds(s0, tka)]
        acc_ref[...] = acc_ref[...] * alpha + _dot(ct, p)
        return m_new

    def pair_body(t, m_run, near, with_scores):
        kt = 2 * t
        logits_to(lg1_ref, kt + 1)
        m_run = consume(lg0_ref, kt, m_run, near)
        if with_scores:
            score_tile(kt, j + 1, keyn_ref, wrow_n)
        logits_to(lg0_ref, kt + 2)
        m_run = consume(lg1_ref, kt + 1, m_run, near)
        if with_scores:
            score_tile(kt + 1, j + 1, keyn_ref, wrow_n)
        return m_run

    m0 = jnp.full((1, nh * tq), NEG_BIG, F32).astype(BF16).astype(F32)
    n_pairs = (n_a + 1) // 2
    far_pairs = jnp.maximum(n_pairs - 2, 0)
    logits_to(lg0_ref, 0)
    wrow_n = head_weights(mtn_ref)

    @pl.when(j + 1 < nb)
    def _():
        m_run = lax.fori_loop(0, far_pairs, functools.partial(pair_body, near=False, with_scores=True), m0)
        lax.fori_loop(far_pairs, n_pairs, functools.partial(pair_body, near=True, with_scores=True), m_run)
        lax.fori_loop(n_pairs, (n_tiles(j + 1) + 1) // 2,
                      lambda t, c: (score_pair(t, j + 1, keyn_ref, wrow_n), c)[1], 0)

    @pl.when(j + 1 >= nb)
    def _():
        m_run = lax.fori_loop(0, far_pairs, functools.partial(pair_body, near=False, with_scores=False), m0)
        lax.fori_loop(far_pairs, n_pairs, functools.partial(pair_body, near=True, with_scores=False), m_run)

    kvr = acc_ref.shape[0] - SUM_ROWS
    l_fin = acc_ref[kvr:kvr + 1, :]
    olat_s[jg] = (acc_ref[0:kvr, :] * (1.0 / l_fin)).astype(BF16)

    @pl.when(jg == grp - 1)
    def _():
        for h in range(nh):
            ol = jnp.concatenate([olat_s[g, :, h * tq:(h + 1) * tq] for g in range(grp)], axis=1)
            o_ref[:, h * dh:(h + 1) * dh] = _dot(wuv_ref[h], ol).T.astype(o_ref.dtype)

    @pl.when(j + 1 < nb)
    def _():
        select(j + 1, keyn_ref)


def _dsa(cq, misc_t, kidx, ckv, ckv_t, wiq_t, wuq_t, wuk_h, wuv_h, bias_tab, *, batch, topk, scale):
    n, q_rank = cq.shape
    s = n // batch
    kv_rank = ckv.shape[-1]
    idx_dim = kidx.shape[-1]
    nh_idx = wiq_t.shape[0] // idx_dim
    nh, dh, _ = wuv_h.shape
    tq = LANE
    nb = s // tq
    tks = 256
    tka = 256
    grp = 4
    assert s % tks == 0 and s % tka == 0 and nb % grp == 0
    last_blk = batch * nb - 1
    prep = lambda bi, j: jnp.minimum(bi * nb + j + jnp.where(j > 0, 1, 0), last_blk)
    return pl.pallas_call(
        functools.partial(_dsa_kernel, nh_idx=nh_idx, idx_dim=idx_dim, nh=nh, dh=dh, topk=topk,
                          scale=scale, tks=tks, tka=tka, grp=grp),
        grid=(batch, nb),
        in_specs=[
            pl.BlockSpec((grp * tq, q_rank), lambda bi, j: (prep(bi, j) // grp, 0)),
            pl.BlockSpec((MISC_ROWS, tq), lambda bi, j: (0, bi * nb + j)),
            pl.BlockSpec((MISC_ROWS, tq), lambda bi, j: (0, jnp.minimum(bi * nb + j + 1, last_blk))),
            pl.BlockSpec((s, idx_dim), lambda bi, j: (bi, 0)),
            pl.BlockSpec((s, kv_rank), lambda bi, j: (bi, 0)),
            pl.BlockSpec((kv_rank + SUM_ROWS, s), lambda bi, j: (0, bi)),
            _resident(wiq_t.shape), _resident(wuq_t.shape), _resident(wuk_h.shape),
            _resident(wuv_h.shape), _resident(bias_tab.shape),
        ],
        out_specs=pl.BlockSpec((grp * tq, nh * dh), lambda bi, j: ((bi * nb + j) // grp, 0)),
        out_shape=jax.ShapeDtypeStruct((n, nh * dh), BF16),
        scratch_shapes=[
            pltpu.VMEM((s, tq), jnp.int32),
            pltpu.VMEM((s, tq), jnp.int32),
            pltpu.VMEM((32, s // tks, 8, tq), jnp.int32),
            pltpu.VMEM((8, tq), jnp.int32),
            pltpu.VMEM((kv_rank + SUM_ROWS, nh * tq), F32),
            pltpu.VMEM((tka, nh * tq), BF16),
            pltpu.VMEM((tka, nh * tq), BF16),
            pltpu.VMEM((2 * grp, idx_dim, nh_idx * tq), BF16),
            pltpu.VMEM((2 * grp, kv_rank, nh * tq), BF16),
            pltpu.VMEM((grp, kv_rank, nh * tq), BF16),
        ],
        compiler_params=_cparams(("arbitrary", "arbitrary")),
        name="dsa",
    )(cq, misc_t, misc_t, kidx, ckv, ckv_t, wiq_t, wuq_t, wuk_h, wuv_h, bias_tab)


def _t5_bucket_np(n, n_buckets):
    max_exact = n_buckets // 2
    nf = np.maximum(n, 1).astype(np.float32)
    large = max_exact + (np.log(nf / max_exact) / math.log(REL_MAX_DIST / max_exact)
                         * (n_buckets - max_exact)).astype(np.int32)
    large = np.minimum(large, n_buckets - 1)
    return np.where(n < max_exact, n, large)


def _bias_table(rel_bias, tq):
    n_buckets, nh = rel_bias.shape
    assert _t5_bucket_np(np.array([tq + 1]), n_buckets)[0] == n_buckets - 1
    s_loc = np.arange(tq)[:, None]
    t_loc = np.arange(tq)[None, :]
    dist = np.stack([np.maximum(d * tq + t_loc - s_loc, 0) for d in range(3)])
    onehot = np.eye(n_buckets, dtype=np.float32)[_t5_bucket_np(dist, n_buckets)]
    tb = jnp.einsum('dstb,bh->dsht', jnp.asarray(onehot), rel_bias.astype(F32),
                    precision=lax.Precision.HIGHEST)
    far = rel_bias[n_buckets - 1].astype(F32)[None, None, :, None]
    return ((tb - far) * LOG2E).reshape(3, tq, nh * tq)


def kernel(x, ffn1_norm, ffn1_wg, ffn1_wu, ffn1_wd, mix_norm, w_in, conv_w, a_log, dt_bias, dn_out_norm, q_norm, kv_norm, w_uq, w_uk, w_uv, w_iq, idx_k_g, idx_k_b, rel_bias, w_o, ffn2_norm, ffn2_wg, ffn2_wu, ffn2_wd, final_norm):
    b, s, d = x.shape
    depth = ffn1_norm.shape[0]
    nh_dn = a_log.shape[1]
    dk = dn_out_norm.shape[1]
    qk_w = nh_dn * dk
    q_rank = q_norm.shape[1]
    kv_rank = kv_norm.shape[1]
    idx_dim = idx_k_g.shape[1]
    nh_idx = w_iq.shape[2] // idx_dim
    nh_sa, dh_sa = w_uk.shape[2], w_uk.shape[3]
    assert qk_w == d and nh_sa * dh_sa == d and nh_idx == nh_dn and 3 * nh_dn <= MISC_ROWS
    topk = min(TOPK_MAX, s // 4)
    n = b * s

    sizes = (3 * qk_w, qk_w, nh_dn, nh_dn, q_rank, kv_rank, idx_dim, nh_idx, d, d)
    offs = np.concatenate([[0], np.cumsum(sizes)])
    (o_qkv, o_z, o_b, o_a, o_cq, o_ckv, o_ik, o_iw, o_ga, o_gb) = [int(v) for v in offs[:-1]]
    l_cq, l_ckv, l_ik, l_misc, small_cols = _small_layout(q_rank, kv_rank, idx_dim)
    bias_tab = _bias_table(rel_bias, LANE)

    x2d = x.reshape(n, d)
    for l in range(depth):
        x1 = _ffn(x2d, ffn1_norm[l], ffn1_wg[l], ffn1_wu[l], ffn1_wd[l])

        wl = w_in[l]
        w_big = jnp.concatenate([wl[:, o_qkv:o_qkv + 4 * qk_w], wl[:, o_ga:o_ga + 2 * d]],
                                axis=1).astype(BF16)
        w_small = jnp.zeros((d, small_cols), F32)
        w_small = w_small.at[:, l_cq:l_cq + q_rank].set(wl[:, o_cq:o_cq + q_rank])
        w_small = w_small.at[:, l_ckv:l_ckv + kv_rank].set(wl[:, o_ckv:o_ckv + kv_rank])
        w_small = w_small.at[:, l_ik:l_ik + idx_dim].set(wl[:, o_ik:o_ik + idx_dim])
        w_small = w_small.at[:, l_misc:l_misc + nh_idx].set(wl[:, o_iw:o_iw + nh_idx])
        w_small = w_small.at[:, l_misc + nh_dn:l_misc + 2 * nh_dn].set(wl[:, o_b:o_b + nh_dn])
        w_small = w_small.at[:, l_misc + 2 * nh_dn:l_misc + 3 * nh_dn].set(wl[:, o_a:o_a + nh_dn])
        neg_a_pad = jnp.zeros((1, LANE), F32).at[0, 2 * nh_dn:3 * nh_dn].set(-jnp.exp(a_log[l].astype(F32)))
        dtb_pad = jnp.zeros((1, LANE), F32).at[0, 2 * nh_dn:3 * nh_dn].set(dt_bias[l].astype(F32))
        gates, cq, ckv, ckv_t, kidx, misc_t, o_dn = _in_proj_deltanet(
            x1, mix_norm[l], w_big, w_small.astype(BF16), q_norm[l], kv_norm[l], idx_k_g[l],
            idx_k_b[l], neg_a_pad, dtb_pad, conv_w[l].astype(F32), dn_out_norm[l], seq=s,
            q_rank=q_rank, kv_rank=kv_rank, idx_dim=idx_dim, nh=nh_dn, dk=dk,
            w_scale=(nh_idx ** -0.5) * (idx_dim ** -0.5))

        o_sa = _dsa(cq, misc_t, kidx, ckv, ckv_t,
                    w_iq[l].T.astype(BF16), w_uq[l].T.astype(BF16),
                    jnp.transpose(w_uk[l], (1, 0, 2)).astype(BF16),
                    jnp.transpose(w_uv[l], (1, 2, 0)).astype(BF16),
                    bias_tab, batch=b, topk=topk, scale=dh_sa ** -0.5)

        x2d = _ffn(x1, ffn2_norm[l], ffn2_wg[l], ffn2_wu[l], ffn2_wd[l],
                   final_g=final_norm if l == depth - 1 else None,
                   merge=(gates, 0, o_dn, o_sa, w_o[l]))
    return x2d.reshape(b, s, d)
```

```python
import functools
import math

import numpy as np
import jax
import jax.numpy as jnp
from jax import lax
from jax.experimental import pallas as pl
from jax.experimental.pallas import tpu as pltpu

F32 = jnp.float32
BF16 = jnp.bfloat16
EPS = 1e-6
TOPK_MAX = 256
REL_MAX_DIST = 128
DN_CHUNK = 64
LANE = 128
SUM_ROWS = 8
MISC_ROWS = 32
INT_MIN = -(2 ** 31)
NEG_BIG = -1e30
LOG2E = math.log2(math.e)
VMEM_LIMIT = 56 * 1024 * 1024


def _cparams(sem):
    return pltpu.CompilerParams(dimension_semantics=sem, vmem_limit_bytes=VMEM_LIMIT)


def _resident(shape):
    return pl.BlockSpec(shape, lambda *_: (0,) * len(shape), pipeline_mode=pl.Buffered(1))


def _rms(x, g):
    return x * lax.rsqrt(jnp.mean(x * x, axis=-1, keepdims=True) + EPS) * g


def _silu(x):
    return x * jax.nn.sigmoid(x)


def _dot(a, b):
    return jnp.dot(a, b, preferred_element_type=F32)


def _dot_nt(a, b):
    return lax.dot_general(a, b, (((1,), (1,)), ((), ())), preferred_element_type=F32)


def _bdot(a, b):
    return lax.dot_general(a, b, (((2,), (1,)), ((0,), (0,))), preferred_element_type=F32)


def _bdot_nt(a, b):
    return lax.dot_general(a, b, (((2,), (2,)), ((0,), (0,))), preferred_element_type=F32)


def _bdot_tn(a, b):
    return lax.dot_general(a, b, (((1,), (1,)), ((0,), (0,))), preferred_element_type=F32)


def _ffn_kernel(*refs, final, merge, ff_chunk):
    refs = list(refs)
    x_ref = refs.pop(0)
    if merge:
        ga_ref, gb_ref, oa_ref, ob_ref, wo_ref = refs[:5]
        refs = refs[5:]
    g_ref, wg_ref, wu_ref, wd_ref = refs[:4]
    refs = refs[4:]
    if final:
        fg_ref = refs.pop(0)
    o_ref, act_ref = refs

    x = x_ref[...]
    if merge:
        merged = (jax.nn.sigmoid(ga_ref[...].astype(F32)) * oa_ref[...].astype(F32)
                  + jax.nn.sigmoid(gb_ref[...].astype(F32)) * ob_ref[...].astype(F32))
        x = x + _dot(merged.astype(BF16), wo_ref[...])
    h = _rms(x, g_ref[...]).astype(BF16)
    ff = wg_ref.shape[1]
    for c0 in range(0, ff, ff_chunk):
        c1 = min(c0 + ff_chunk, ff)
        a = _dot(h, wg_ref[:, c0:c1])
        u = _dot(h, wu_ref[:, c0:c1])
        act_ref[:, c0:c1] = (_silu(a) * u).astype(BF16)
    y = x + 0.5 * _dot(act_ref[...], wd_ref[...])
    if final:
        y = _rms(y, fg_ref[...])
    o_ref[...] = y


def _ffn(x2d, g, wg, wu, wd, *, final_g=None, merge=None, tm=512, ff_chunk=256):
    n, d = x2d.shape
    ff = wg.shape[1]
    row = lambda cols, blk=0: pl.BlockSpec((tm, cols), lambda i, blk=blk: (i, blk))
    in_specs = [row(d)]
    args = [x2d]
    if merge is not None:
        big2d, gate_blk, o_dn, o_sa, w_o = merge
        in_specs += [row(d, gate_blk), row(d, gate_blk + 1), row(d), row(d), _resident((d, d))]
        args += [big2d, big2d, o_dn, o_sa, w_o.astype(BF16)]
    in_specs += [_resident((1, d)), _resident((d, ff)), _resident((d, ff)), _resident((ff, d))]
    args += [g.reshape(1, d), wg.astype(BF16), wu.astype(BF16), wd.astype(BF16)]
    if final_g is not None:
        in_specs.append(_resident((1, d)))
        args.append(final_g.reshape(1, d))
    return pl.pallas_call(
        functools.partial(_ffn_kernel, final=final_g is not None, merge=merge is not None,
                          ff_chunk=ff_chunk),
        grid=(n // tm,),
        in_specs=in_specs,
        out_specs=row(d),
        out_shape=jax.ShapeDtypeStruct((n, d), F32),
        scratch_shapes=[pltpu.VMEM((tm, ff), BF16)],
        compiler_params=_cparams(("parallel",)),
        name="ffn_merge" if merge is not None else "ffn",
    )(*args)


def _small_layout(q_rank, kv_rank, idx_dim):
    o_cq = 0
    o_ckv = o_cq + q_rank
    o_ik = o_ckv + kv_rank
    o_misc = o_ik + LANE * ((idx_dim + LANE - 1) // LANE)
    return o_cq, o_ckv, o_ik, o_misc, o_misc + LANE


def _tri_inverse(lmat, masks, eye):
    x = eye - lmat * masks[0]
    for mk in masks[1:]:
        lm = (lmat * mk).astype(BF16)
        t = _bdot(x.astype(BF16), lm)
        x = x - _bdot(t.astype(BF16), x.astype(BF16))
    return x


def _dn_consts(c):
    row = lax.broadcasted_iota(jnp.int32, (c, c), 0)
    col = lax.broadcasted_iota(jnp.int32, (c, c), 1)
    tril = row >= col
    strict = row > col
    eye = jnp.where(row == col, 1.0, 0.0).astype(F32)
    tril_f = jnp.where(tril, 1.0, 0.0).astype(F32)
    masks = []
    m = 1
    while m < c:
        mk = strict & ((row // (2 * m)) == (col // (2 * m))) & ((row // m) != (col // m))
        masks.append(jnp.where(mk, 1.0, 0.0).astype(F32))
        m *= 2
    return tril, strict, eye, tril_f, masks


def _dn_conv(in_ref, rb, ts, cw_ref, xpad_ref, act_ref, keep, cols):
    halo = 8
    conv_k = cw_ref.shape[0]
    if keep is not None:
        xpad_ref[0:halo, cols] = xpad_ref[0:halo, cols] * keep
    xpad_ref[halo:halo + ts, cols] = in_ref[rb:rb + ts, cols].astype(F32)
    y = xpad_ref[halo:halo + ts, cols] * cw_ref[conv_k - 1:conv_k, cols]
    for dlt in range(1, conv_k):
        y = y + xpad_ref[halo - dlt:halo - dlt + ts, cols] * cw_ref[conv_k - 1 - dlt:conv_k - dlt, cols]
    act_ref[:, cols] = _silu(y)
    xpad_ref[0:halo, cols] = xpad_ref[ts:ts + halo, cols]


def _dn_stages(in_ref, bg_ref, mt_ref, rb, ts, act_ref, onorm, s, o_ref, consts, *, nh, dk):
    c = DN_CHUNK
    cpi = ts // c
    qk_w = nh * dk
    tril, strict, eye, tril_f, masks = consts

    def heads(r0, base):
        return [act_ref[r0:r0 + c, base + h * dk:base + (h + 1) * dk] for h in range(nh)]

    qs, ks, vs, betas, gccs, gcrs, glasts = [], [], [], [], [], [], []
    for cc in range(cpi):
        r0 = cc * c
        bg = bg_ref[rb + r0:rb + r0 + c, :]
        grow = mt_ref[2 * nh:3 * nh, rb + r0:rb + r0 + c]
        gc_col = jnp.dot(tril_f, bg, precision=lax.Precision.HIGHEST,
                         preferred_element_type=F32)
        gc_row = lax.dot_general(grow, tril_f, (((1,), (1,)), ((), ())),
                                 precision=lax.Precision.HIGHEST,
                                 preferred_element_type=F32)
        qs += heads(r0, 0)
        ks += heads(r0, qk_w)
        vs += heads(r0, 2 * qk_w)
        betas += [bg[:, nh + h:nh + h + 1] for h in range(nh)]
        gccs += [gc_col[:, 2 * nh + h:2 * nh + h + 1] for h in range(nh)]
        gcrs += [gc_row[h:h + 1, :] for h in range(nh)]
        glasts += [gc_row[h:h + 1, c - 1:c] for h in range(nh)]
    q = jnp.stack(qs)
    k = jnp.stack(ks)
    v = jnp.stack(vs)
    beta = jnp.stack(betas)
    gcc = jnp.stack(gccs)
    gcr = jnp.stack(gcrs)
    g_last = jnp.stack(glasts)
    q = q * (lax.rsqrt(jnp.sum(q * q, axis=-1, keepdims=True) + EPS) * (dk ** -0.5))
    k = k * lax.rsqrt(jnp.sum(k * k, axis=-1, keepdims=True) + EPS)
    decay = jnp.where(tril, jnp.exp(jnp.minimum(gcc - gcr, 0.0)), 0.0)
    eg = jnp.exp(gcc)
    kb = k * beta
    vb = v * beta
    aq = _bdot_nt(jnp.concatenate([kb, q], axis=1).astype(BF16), k.astype(BF16))
    lmat = jnp.where(strict, aq[:, :c] * decay, 0.0)
    intra = (aq[:, c:] * decay).astype(BF16)
    tinv = _tri_inverse(lmat, masks, eye)
    uw = _bdot(tinv.astype(BF16), jnp.concatenate([vb, kb * eg], axis=2).astype(BF16))
    u = uw[:, :, :dk]
    wq = jnp.concatenate([uw[:, :, dk:], q * eg], axis=1).astype(BF16)
    kd = (k * jnp.exp(g_last - gcc)).astype(BF16)
    sdec = jnp.exp(g_last)
    for cc in range(cpi):
        sl = slice(cc * nh, (cc + 1) * nh)
        m1 = _bdot(wq[sl], s.astype(BF16))
        vnb = (u[sl] - m1[:, :c]).astype(BF16)
        o = _rms(m1[:, c:] + _bdot(intra[sl], vnb), onorm)
        s = s * sdec[sl] + _bdot_tn(kd[sl], vnb)
        for h in range(nh):
            rows = slice(rb + cc * c, rb + (cc + 1) * c)
            zz = in_ref[rows, 3 * qk_w + h * dk:3 * qk_w + (h + 1) * dk].astype(F32)
            o_ref[rows, h * dk:(h + 1) * dk] = (o[h] * _silu(zz)).astype(o_ref.dtype)
    return s


def _inproj_dn_kernel(x_ref, g_ref, wb_ref, ws_ref, qn_ref, kvn_ref, ikg_ref, ikb_ref, negA_ref, dtb_ref,
                      cw_ref, on_ref,
                      gates_ref, cq_ref, ckv_ref, ckvt_ref, ik_ref, misct_ref, odn_ref,
                      qkvz_s, misc_s, mt_s, dn_in, dn_misc, dn_mt, xpad_ref, act_ref, s_ref,
                      *, q_rank, kv_rank, idx_dim, nh, dk, w_scale, col_chunk, seq, ts):
    o_cq, o_ckv, o_ik, o_misc, _ = _small_layout(q_rank, kv_rank, idx_dim)
    i = pl.program_id(0)
    tm = x_ref.shape[0]
    qkvz_w = qkvz_s.shape[1]

    @pl.when(i == 0)
    def _():
        qkvz_s[...] = jnp.zeros_like(qkvz_s)
        misc_s[...] = jnp.zeros_like(misc_s)
        mt_s[...] = jnp.zeros_like(mt_s)
        xpad_ref[0:8, :] = jnp.zeros((8, xpad_ref.shape[1]), F32)
        s_ref[...] = jnp.zeros_like(s_ref)

    dn_in[...] = qkvz_s[...]
    dn_misc[...] = misc_s[...]
    dn_mt[...] = mt_s[...]
    keep = jnp.where(((i - 1) * tm) % seq == 0, 0.0, 1.0)
    consts = _dn_consts(DN_CHUNK)
    onorm = on_ref[...]

    h = _rms(x_ref[...], g_ref[...]).astype(BF16)
    cols = wb_ref.shape[1]

    def proj_chunk(c0):
        c1 = min(c0 + col_chunk, cols)
        p = _dot(h, wb_ref[:, c0:c1]).astype(BF16)
        if c1 <= qkvz_w:
            qkvz_s[:, c0:c1] = p
        else:
            assert c0 >= qkvz_w
            gates_ref[:, c0 - qkvz_w:c1 - qkvz_w] = p

    def proj_small():
        p = _dot(h, ws_ref[...])
        cq_ref[...] = _rms(p[:, o_cq:o_cq + q_rank], qn_ref[...]).astype(cq_ref.dtype)
        ckv = _rms(p[:, o_ckv:o_ckv + kv_rank], kvn_ref[...])
        ckv_ref[...] = ckv.astype(ckv_ref.dtype)
        ones = jnp.ones((SUM_ROWS, ckv.shape[0]), F32)
        ckvt_ref[...] = jnp.concatenate([ckv.T, ones], axis=0).astype(ckvt_ref.dtype)
        ik = p[:, o_ik:o_ik + idx_dim]
        mu = jnp.mean(ik, axis=-1, keepdims=True)
        var = jnp.mean(jnp.square(ik - mu), axis=-1, keepdims=True)
        ik_ref[...] = ((ik - mu) * lax.rsqrt(var + EPS) * ikg_ref[...] + ikb_ref[...]).astype(ik_ref.dtype)
        m = p[:, o_misc:o_misc + LANE]
        lane = lax.broadcasted_iota(jnp.int32, m.shape, 1)
        beta = jax.nn.sigmoid(m)
        sp_in = m + dtb_ref[...]
        softplus = jnp.maximum(sp_in, 0.0) + jnp.log(1.0 + jnp.exp(-jnp.abs(sp_in)))
        gdec = negA_ref[...] * softplus
        misc = jnp.where(lane < nh, m * w_scale,
                         jnp.where(lane < 2 * nh, beta, jnp.where(lane < 3 * nh, gdec, 0.0)))
        misc_s[...] = misc
        mt = misc.T[0:MISC_ROWS, :]
        mt_s[...] = mt
        misct_ref[...] = mt

    chunks = list(range(0, cols, col_chunk))
    n_sub = tm // ts
    s = s_ref[...] * keep
    ci = 0
    for st in range(n_sub):
        for k0 in range(0, cw_ref.shape[1], col_chunk):
            _dn_conv(dn_in, st * ts, ts, cw_ref, xpad_ref, act_ref, keep if st == 0 else None,
                     slice(k0, k0 + col_chunk))
            if ci < len(chunks):
                proj_chunk(chunks[ci])
                ci += 1
        s = _dn_stages(dn_in, dn_misc, dn_mt, st * ts, ts, act_ref, onorm, s, odn_ref, consts, nh=nh, dk=dk)
    for c0 in chunks[ci:]:
        proj_chunk(c0)
    proj_small()
    s_ref[...] = s


def _in_proj_deltanet(x2d, g, w_big, w_small, q_norm, kv_norm, ik_g, ik_b, neg_a_pad, dtb_pad, conv_w,
                      out_norm, *, seq, q_rank, kv_rank, idx_dim, nh, dk, w_scale, tm=512, ts=256,
                      col_chunk=512):
    n, d = x2d.shape
    cols = w_big.shape[1]
    qk_w = nh * dk
    qkvz_w = 4 * qk_w
    gate_w = cols - qkvz_w
    nt = n // tm
    assert seq % tm == 0 and tm % ts == 0 and qkvz_w % col_chunk == 0
    cur = lambda i: jnp.minimum(i, nt - 1)
    prev = lambda i: jnp.maximum(i - 1, 0)
    row = lambda c: pl.BlockSpec((tm, c), lambda i: (cur(i), 0))
    colblk = lambda r: pl.BlockSpec((r, tm), lambda i: (0, cur(i)))
    return pl.pallas_call(
        functools.partial(_inproj_dn_kernel, q_rank=q_rank, kv_rank=kv_rank, idx_dim=idx_dim, nh=nh,
                          dk=dk, w_scale=w_scale, col_chunk=col_chunk, seq=seq, ts=ts),
        grid=(nt + 1,),
        in_specs=[
            row(d), _resident((1, d)), _resident((d, cols)), _resident((d, w_small.shape[1])),
            _resident((1, q_rank)), _resident((1, kv_rank)), _resident((1, idx_dim)),
            _resident((1, idx_dim)), _resident((1, LANE)), _resident((1, LANE)),
            _resident(conv_w.shape), _resident((1, dk)),
        ],
        out_specs=[row(gate_w), row(q_rank), row(kv_rank), colblk(kv_rank + SUM_ROWS), row(idx_dim),
                   colblk(MISC_ROWS), pl.BlockSpec((tm, qk_w), lambda i: (prev(i), 0))],
        out_shape=[
            jax.ShapeDtypeStruct((n, gate_w), BF16),
            jax.ShapeDtypeStruct((n, q_rank), BF16),
            jax.ShapeDtypeStruct((n, kv_rank), BF16),
            jax.ShapeDtypeStruct((kv_rank + SUM_ROWS, n), BF16),
            jax.ShapeDtypeStruct((n, idx_dim), BF16),
            jax.ShapeDtypeStruct((MISC_ROWS, n), F32),
            jax.ShapeDtypeStruct((n, qk_w), BF16),
        ],
        scratch_shapes=[
            pltpu.VMEM((tm, qkvz_w), BF16), pltpu.VMEM((tm, LANE), F32), pltpu.VMEM((MISC_ROWS, tm), F32),
            pltpu.VMEM((tm, qkvz_w), BF16), pltpu.VMEM((tm, LANE), F32), pltpu.VMEM((MISC_ROWS, tm), F32),
            pltpu.VMEM((ts + 8, 3 * qk_w), F32),
            pltpu.VMEM((ts, 3 * qk_w), F32),
            pltpu.VMEM((nh, dk, dk), F32),
        ],
        compiler_params=_cparams(("arbitrary",)),
        name="in_proj_deltanet",
    )(x2d, g.reshape(1, d), w_big, w_small, q_norm.reshape(1, -1), kv_norm.reshape(1, -1),
      ik_g.reshape(1, -1), ik_b.reshape(1, -1), neg_a_pad, dtb_pad, conv_w, out_norm.reshape(1, dk))


def _sortable(x):
    i = pltpu.bitcast(x, jnp.int32)
    return i ^ ((i >> 31) & jnp.int32(0x7FFFFFFF))


def _bit_transpose32(words):
    a = list(words)
    assert len(a) == 32
    j, m = 16, 0x0000FFFF
    while j:
        for k in range(32):
            if not k & j:
                t = (lax.shift_right_logical(a[k], jnp.int32(j)) ^ a[k + j]) & jnp.int32(m)
                a[k + j] = a[k + j] ^ t
                a[k] = a[k] ^ (t << j)
        j >>= 1
        m = (m ^ (m << j)) & 0xFFFFFFFF if j else m
    return a


def _dsa_kernel(cq_ref, mtc_ref, mtn_ref, ki_ref, ckv_ref, ckvt_ref, wiq_ref, wuq_ref, wuk_ref, wuv_ref,
                bias_ref, o_ref, key_ref, keyn_ref, planes_ref, thr_s, acc_ref, lg0_ref, lg1_ref,
                qidx_s, qlat_s, olat_s, *, nh_idx, idx_dim, nh, dh, topk, scale, tks, tka, grp):
    tq = LANE
    sub = LANE
    j = pl.program_id(1)
    nb = pl.num_programs(1)
    jg = j % grp
    rows = lax.broadcasted_iota(jnp.int32, (sub, tq), 0)
    cols = lax.broadcasted_iota(jnp.int32, (sub, tq), 1)
    rows_s = lax.broadcasted_iota(jnp.int32, (tks, tq), 0)
    nt = planes_ref.shape[1]
    assert nt % 2 == 0 and tks == tka

    def n_tiles(blk):
        return (blk * tq + tq + tks - 1) // tks

    def project_group(first_slot):
        cqg = cq_ref[...]
        qi = _dot_nt(wiq_ref[...], cqg).astype(BF16)
        qg = _dot_nt(wuq_ref[...], cqg).astype(BF16)
        for g in range(grp):
            qidx_s[first_slot + g] = jnp.concatenate(
                [qi[h * idx_dim:(h + 1) * idx_dim, g * tq:(g + 1) * tq] for h in range(nh_idx)], axis=1)
        for h in range(nh):
            ql = (_dot(wuk_ref[h], qg[h * dh:(h + 1) * dh, :]) * (scale * LOG2E)).astype(BF16)
            for g in range(grp):
                qlat_s[first_slot + g, :, h * tq:(h + 1) * tq] = ql[:, g * tq:(g + 1) * tq]

    @pl.when((pl.program_id(0) == 0) & (j == 0))
    def _():
        planes_ref[...] = jnp.zeros_like(planes_ref)

    @pl.when(j == 0)
    def _():
        project_group(0)

    @pl.when((j > 0) & ((j + 1) % grp == 0) & (j + 1 < nb))
    def _():
        project_group((((j + 1) // grp) % 2) * grp)

    def head_weights(mt_ref):
        return jnp.concatenate([mt_ref[h:h + 1, :] for h in range(nh_idx)], axis=1)

    def score_tile(kt, blk, kref, wrow_all):
        slot = blk % (2 * grp)
        qpos = blk * tq + cols
        words = []
        for r in range(tks // sub):
            s0 = pl.multiple_of(kt * tks + r * sub, sub)
            kk = ki_ref[pl.ds(s0, sub), :]
            d = jnp.maximum(_dot(kk, qidx_s[slot]), 0.0) * wrow_all
            parts = [d[:, h * tq:(h + 1) * tq] for h in range(nh_idx)]
            while len(parts) > 1:
                parts = [parts[a] + parts[a + 1] for a in range(0, len(parts), 2)]
            causal = (s0 + rows) <= qpos
            key = jnp.where(causal, _sortable(parts[0]), INT_MIN)
            kref[pl.ds(s0, sub), :] = key
            ukey = key ^ INT_MIN
            words += [ukey[8 * w:8 * (w + 1), :] for w in range(sub // 8)]
        for b, plane in enumerate(_bit_transpose32(words)):
            planes_ref[b, kt] = plane

    def score_pair(t, blk, kref, wrow_all):
        score_tile(2 * t, blk, kref, wrow_all)
        score_tile(2 * t + 1, blk, kref, wrow_all)

    def select(blk, kref):
        n_s = n_tiles(blk)
        tile_ids = lax.broadcasted_iota(jnp.int32, (nt, 8, tq), 0)
        eq0 = jnp.where(tile_ids < n_s, -1, 0).astype(jnp.int32)

        def bit_body(bi, state):
            ans, cnt_gt, eq = state
            b = 31 - bi
            t = eq & planes_ref[b]
            c = jnp.sum(jnp.sum(lax.population_count(t), axis=0), axis=0, keepdims=True)
            accept = (cnt_gt + c) >= topk
            ans = jnp.where(accept, ans | jnp.left_shift(jnp.int32(1), b), ans)
            cnt_gt = jnp.where(accept, cnt_gt, cnt_gt + c)
            eq = jnp.where(accept, t, eq ^ t)
            return ans, cnt_gt, eq

        zero_row = jnp.zeros((1, tq), jnp.int32)
        ans_u, cnt_gt, eq = lax.fori_loop(0, 32, bit_body, (zero_row, zero_row, eq0))
        thr = jnp.maximum(ans_u ^ INT_MIN, INT_MIN + 1)

        n_eq = jnp.sum(jnp.sum(lax.population_count(eq), axis=0), axis=0, keepdims=True)
        any_tie = jnp.max(jnp.where((cnt_gt + n_eq > topk) & (ans_u != 0), 1, 0)) > 0

        def count_tiles(hit_fn):
            def body(kt, acc):
                s0 = pl.multiple_of(kt * tks, tks)
                hit = hit_fn(kref[pl.ds(s0, tks), :], s0)
                return acc + jnp.sum(hit.reshape(tks // 32, 4, 8, tq), axis=0)
            acc = lax.fori_loop(0, n_s, body, jnp.zeros((4, 8, tq), jnp.int32))
            return jnp.sum(acc.reshape(32, tq), axis=0, keepdims=True)

        @pl.when(any_tie)
        def _():
            n_gt = count_tiles(lambda kv, s0: jnp.where(kv > thr, 1, 0).astype(jnp.int32))
            need = topk - n_gt

            def count_eq_below(lim):
                return count_tiles(lambda kv, s0: jnp.where((kv == thr) & ((s0 + rows_s) < lim), 1, 0)
                                   .astype(jnp.int32))

            nbits = int(math.ceil(math.log2(key_ref.shape[0]))) + 1

            def jbit_body(bi, lim):
                cand = lim | jnp.left_shift(jnp.int32(1), nbits - 1 - bi)
                return jnp.where(count_eq_below(cand) <= need, cand, lim)

            lim = lax.fori_loop(0, nbits, jbit_body, jnp.zeros((1, tq), jnp.int32))

            def demote_body(kt, carry):
                s0 = pl.multiple_of(kt * tks, tks)
                kv = kref[pl.ds(s0, tks), :]
                drop = (kv == thr) & ((s0 + rows_s) >= lim)
                kref[pl.ds(s0, tks), :] = jnp.where(drop, INT_MIN, kv)
                return carry

            lax.fori_loop(0, n_s, demote_body, 0)

        thr_s[0:1, :] = thr

    @pl.when(j == 0)
    def _():
        w0 = head_weights(mtc_ref)
        lax.fori_loop(0, (n_tiles(0) + 1) // 2, lambda t, c: (score_pair(t, 0, key_ref, w0), c)[1], 0)
        select(0, key_ref)

    @pl.when(j > 0)
    def _():
        def copy_body(kt, carry):
            s0 = pl.multiple_of(kt * tks, tks)
            key_ref[pl.ds(s0, tks), :] = keyn_ref[pl.ds(s0, tks), :]
            return carry
        lax.fori_loop(0, n_tiles(j), copy_body, 0)

    thr = thr_s[0:1, :]

    acc_ref[...] = jnp.zeros_like(acc_ref)
    nsub = tka // sub
    n_a = n_tiles(j)
    last_tile = n_a - 1
    int_max = jnp.int32(2 ** 31 - 1)
    qslot = j % (2 * grp)

    def logits_to(buf_ref, kt):
        s0 = pl.multiple_of(jnp.minimum(kt, last_tile) * tka, tka)
        buf_ref[...] = _dot(ckv_ref[pl.ds(s0, tka), :], qlat_s[qslot]).astype(buf_ref.dtype)

    def consume(buf_ref, kt, m_run, near):
        ktc = jnp.minimum(kt, last_tile)
        s0 = pl.multiple_of(ktc * tka, tka)
        thr_v = jnp.where(kt <= last_tile, thr, int_max)
        parts = []
        for r in range(nsub):
            kv = key_ref[pl.ds(pl.multiple_of(s0 + r * sub, sub), sub), :]
            madd = jnp.where(kv >= thr_v, 0.0, NEG_BIG)
            if near:
                madd = jnp.concatenate([madd] * nh, axis=1) + bias_ref[jnp.clip(j - (ktc * nsub + r), 0, 2)]
                madd = madd.astype(BF16)
            else:
                madd = jnp.concatenate([madd.astype(BF16)] * nh, axis=1)
            parts.append(buf_ref[r * sub:(r + 1) * sub, :] + madd)
        lg = jnp.concatenate(parts, axis=0)
        tile_max = jnp.max(lg.reshape(tka // 16, 16, nh * tq), axis=0).astype(F32)
        m_new = jnp.maximum(m_run, jnp.max(tile_max, axis=0, keepdims=True))
        p = jnp.exp2(lg - m_new.astype(BF16))
        alpha = jnp.exp2(m_run - m_new)
        ct = ckvt_ref[:, pl.ds(s0, tka)]
        acc_ref[...] = acc_ref[...] * alpha + _dot(ct, p)
        return m_new

    def pair_body(t, m_run, near, with_scores):
        kt = 2 * t
        logits_to(lg1_ref, kt + 1)
        m_run = consume(lg0_ref, kt, m_run, near)
        if with_scores:
            score_tile(kt, j + 1, keyn_ref, wrow_n)
        logits_to(lg0_ref, kt + 2)
        m_run = consume(lg1_ref, kt + 1, m_run, near)
        if with_scores:
            score_tile(kt + 1, j + 1, keyn_ref, wrow_n)
        return m_run

    def quad_body(q, m_run, with_scores):
        m_run = pair_body(2 * q, m_run, False, with_scores)
        return pair_body(2 * q + 1, m_run, False, with_scores)

    m0 = jnp.full((1, nh * tq), NEG_BIG, F32).astype(BF16).astype(F32)
    n_pairs = (n_a + 1) // 2
    far_quads = jnp.maximum((j - 1) // (2 * nsub), 0) // 2
    logits_to(lg0_ref, 0)
    wrow_n = head_weights(mtn_ref)

    @pl.when(j + 1 < nb)
    def _():
        m_run = lax.fori_loop(0, far_quads, functools.partial(quad_body, with_scores=True), m0)
        lax.fori_loop(2 * far_quads, n_pairs, functools.partial(pair_body, near=True, with_scores=True), m_run)
        lax.fori_loop(n_pairs, (n_tiles(j + 1) + 1) // 2,
                      lambda t, c: (score_pair(t, j + 1, keyn_ref, wrow_n), c)[1], 0)

    @pl.when(j + 1 >= nb)
    def _():
        m_run = lax.fori_loop(0, far_quads, functools.partial(quad_body, with_scores=False), m0)
        lax.fori_loop(2 * far_quads, n_pairs, functools.partial(pair_body, near=True, with_scores=False), m_run)

    kvr = acc_ref.shape[0] - SUM_ROWS
    l_fin = acc_ref[kvr:kvr + 1, :]
    olat_s[jg] = (acc_ref[0:kvr, :] * (1.0 / l_fin)).astype(BF16)

    @pl.when(jg == grp - 1)
    def _():
        for h in range(nh):
            ol = jnp.concatenate([olat_s[g, :, h * tq:(h + 1) * tq] for g in range(grp)], axis=1)
            o_ref[:, h * dh:(h + 1) * dh] = _dot(wuv_ref[h], ol).T.astype(o_ref.dtype)

    @pl.when(j + 1 < nb)
    def _():
        select(j + 1, keyn_ref)


def _dsa(cq, misc_t, kidx, ckv, ckv_t, wiq_t, wuq_t, wuk_h, wuv_h, bias_tab, *, batch, topk, scale):
    n, q_rank = cq.shape
    s = n // batch
    kv_rank = ckv.shape[-1]
    idx_dim = kidx.shape[-1]
    nh_idx = wiq_t.shape[0] // idx_dim
    nh, dh, _ = wuv_h.shape
    tq = LANE
    nb = s // tq
    tks = 256
    tka = 256
    grp = 4
    assert s % tks == 0 and s % tka == 0 and nb % grp == 0
    last_blk = batch * nb - 1
    prep = lambda bi, j: jnp.minimum(bi * nb + j + jnp.where(j > 0, 1, 0), last_blk)
    return pl.pallas_call(
        functools.partial(_dsa_kernel, nh_idx=nh_idx, idx_dim=idx_dim, nh=nh, dh=dh, topk=topk,
                          scale=scale, tks=tks, tka=tka, grp=grp),
        grid=(batch, nb),
        in_specs=[
            pl.BlockSpec((grp * tq, q_rank), lambda bi, j: (prep(bi, j) // grp, 0)),
            pl.BlockSpec((MISC_ROWS, tq), lambda bi, j: (0, bi * nb + j)),
            pl.BlockSpec((MISC_ROWS, tq), lambda bi, j: (0, jnp.minimum(bi * nb + j + 1, last_blk))),
            pl.BlockSpec((s, idx_dim), lambda bi, j: (bi, 0)),
            pl.BlockSpec((s, kv_rank), lambda bi, j: (bi, 0)),
            pl.BlockSpec((kv_rank + SUM_ROWS, s), lambda bi, j: (0, bi)),
            _resident(wiq_t.shape), _resident(wuq_t.shape), _resident(wuk_h.shape),
            _resident(wuv_h.shape), _resident(bias_tab.shape),
        ],
        out_specs=pl.BlockSpec((grp * tq, nh * dh), lambda bi, j: ((bi * nb + j) // grp, 0)),
        out_shape=jax.ShapeDtypeStruct((n, nh * dh), BF16),
        scratch_shapes=[
            pltpu.VMEM((s, tq), jnp.int32),
            pltpu.VMEM((s, tq), jnp.int32),
            pltpu.VMEM((32, s // tks, 8, tq), jnp.int32),
            pltpu.VMEM((8, tq), jnp.int32),
            pltpu.VMEM((kv_rank + SUM_ROWS, nh * tq), F32),
            pltpu.VMEM((tka, nh * tq), BF16),
            pltpu.VMEM((tka, nh * tq), BF16),
            pltpu.VMEM((2 * grp, idx_dim, nh_idx * tq), BF16),
            pltpu.VMEM((2 * grp, kv_rank, nh * tq), BF16),
            pltpu.VMEM((grp, kv_rank, nh * tq), BF16),
        ],
        compiler_params=_cparams(("arbitrary", "arbitrary")),
        name="dsa",
    )(cq, misc_t, misc_t, kidx, ckv, ckv_t, wiq_t, wuq_t, wuk_h, wuv_h, bias_tab)


def _t5_bucket_np(n, n_buckets):
    max_exact = n_buckets // 2
    nf = np.maximum(n, 1).astype(np.float32)
    large = max_exact + (np.log(nf / max_exact) / math.log(REL_MAX_DIST / max_exact)
                         * (n_buckets - max_exact)).astype(np.int32)
    large = np.minimum(large, n_buckets - 1)
    return np.where(n < max_exact, n, large)


def _bias_table(rel_bias, tq):
    n_buckets, nh = rel_bias.shape
    assert _t5_bucket_np(np.array([tq + 1]), n_buckets)[0] == n_buckets - 1
    s_loc = np.arange(tq)[:, None]
    t_loc = np.arange(tq)[None, :]
    dist = np.stack([np.maximum(d * tq + t_loc - s_loc, 0) for d in range(3)])
    onehot = np.eye(n_buckets, dtype=np.float32)[_t5_bucket_np(dist, n_buckets)]
    tb = jnp.einsum('dstb,bh->dsht', jnp.asarray(onehot), rel_bias.astype(F32),
                    precision=lax.Precision.HIGHEST)
    far = rel_bias[n_buckets - 1].astype(F32)[None, None, :, None]
    return ((tb - far) * LOG2E).reshape(3, tq, nh * tq)


def kernel(x, ffn1_norm, ffn1_wg, ffn1_wu, ffn1_wd, mix_norm, w_in, conv_w, a_log, dt_bias, dn_out_norm, q_norm, kv_norm, w_uq, w_uk, w_uv, w_iq, idx_k_g, idx_k_b, rel_bias, w_o, ffn2_norm, ffn2_wg, ffn2_wu, ffn2_wd, final_norm):
    b, s, d = x.shape
    depth = ffn1_norm.shape[0]
    nh_dn = a_log.shape[1]
    dk = dn_out_norm.shape[1]
    qk_w = nh_dn * dk
    q_rank = q_norm.shape[1]
    kv_rank = kv_norm.shape[1]
    idx_dim = idx_k_g.shape[1]
    nh_idx = w_iq.shape[2] // idx_dim
    nh_sa, dh_sa = w_uk.shape[2], w_uk.shape[3]
    assert qk_w == d and nh_sa * dh_sa == d and nh_idx == nh_dn and 3 * nh_dn <= MISC_ROWS
    topk = min(TOPK_MAX, s // 4)
    n = b * s

    sizes = (3 * qk_w, qk_w, nh_dn, nh_dn, q_rank, kv_rank, idx_dim, nh_idx, d, d)
    offs = np.concatenate([[0], np.cumsum(sizes)])
    (o_qkv, o_z, o_b, o_a, o_cq, o_ckv, o_ik, o_iw, o_ga, o_gb) = [int(v) for v in offs[:-1]]
    l_cq, l_ckv, l_ik, l_misc, small_cols = _small_layout(q_rank, kv_rank, idx_dim)
    bias_tab = _bias_table(rel_bias, LANE)

    x2d = x.reshape(n, d)
    for l in range(depth):
        x1 = _ffn(x2d, ffn1_norm[l], ffn1_wg[l], ffn1_wu[l], ffn1_wd[l])

        wl = w_in[l]
        w_big = jnp.concatenate([wl[:, o_qkv:o_qkv + 4 * qk_w], wl[:, o_ga:o_ga + 2 * d]],
                                axis=1).astype(BF16)
        w_small = jnp.zeros((d, small_cols), F32)
        w_small = w_small.at[:, l_cq:l_cq + q_rank].set(wl[:, o_cq:o_cq + q_rank])
        w_small = w_small.at[:, l_ckv:l_ckv + kv_rank].set(wl[:, o_ckv:o_ckv + kv_rank])
        w_small = w_small.at[:, l_ik:l_ik + idx_dim].set(wl[:, o_ik:o_ik + idx_dim])
        w_small = w_small.at[:, l_misc:l_misc + nh_idx].set(wl[:, o_iw:o_iw + nh_idx])
        w_small = w_small.at[:, l_misc + nh_dn:l_misc + 2 * nh_dn].set(wl[:, o_b:o_b + nh_dn])
        w_small = w_small.at[:, l_misc + 2 * nh_dn:l_misc + 3 * nh_dn].set(wl[:, o_a:o_a + nh_dn])
        neg_a_pad = jnp.zeros((1, LANE), F32).at[0, 2 * nh_dn:3 * nh_dn].set(-jnp.exp(a_log[l].astype(F32)))
        dtb_pad = jnp.zeros((1, LANE), F32).at[0, 2 * nh_dn:3 * nh_dn].set(dt_bias[l].astype(F32))
        gates, cq, ckv, ckv_t, kidx, misc_t, o_dn = _in_proj_deltanet(
            x1, mix_norm[l], w_big, w_small.astype(BF16), q_norm[l], kv_norm[l], idx_k_g[l],
            idx_k_b[l], neg_a_pad, dtb_pad, conv_w[l].astype(F32), dn_out_norm[l], seq=s,
            q_rank=q_rank, kv_rank=kv_rank, idx_dim=idx_dim, nh=nh_dn, dk=dk,
            w_scale=(nh_idx ** -0.5) * (idx_dim ** -0.5))

        o_sa = _dsa(cq, misc_t, kidx, ckv, ckv_t,
                    w_iq[l].T.astype(BF16), w_uq[l].T.astype(BF16),
                    jnp.transpose(w_uk[l], (1, 0, 2)).astype(BF16),
                    jnp.transpose(w_uv[l], (1, 2, 0)).astype(BF16),
                    bias_tab, batch=b, topk=topk, scale=dh_sa ** -0.5)

        x2d = _ffn(x1, ffn2_norm[l], ffn2_wg[l], ffn2_wu[l], ffn2_wd[l],
                   final_g=final_norm if l == depth - 1 else None,
                   merge=(gates, 0, o_dn, o_sa, w_o[l]))
    return x2d.reshape(b, s, d)
```

```python
import functools
import math

import numpy as np
import jax
import jax.numpy as jnp
from jax import lax
from jax.experimental import pallas as pl
from jax.experimental.pallas import tpu as pltpu

F32 = jnp.float32
BF16 = jnp.bfloat16
EPS = 1e-6
TOPK_MAX = 256
REL_MAX_DIST = 128
DN_CHUNK = 64
LANE = 128
SUM_ROWS = 8
MISC_ROWS = 32
INT_MIN = -(2 ** 31)
NEG_BIG = -1e30
LOG2E = math.log2(math.e)
VMEM_LIMIT = 56 * 1024 * 1024


def _cparams(sem):
    return pltpu.CompilerParams(dimension_semantics=sem, vmem_limit_bytes=VMEM_LIMIT)


def _resident(shape):
    return pl.BlockSpec(shape, lambda *_: (0,) * len(shape), pipeline_mode=pl.Buffered(1))


def _rms(x, g):
    return x * lax.rsqrt(jnp.mean(x * x, axis=-1, keepdims=True) + EPS) * g


def _silu(x):
    return x * jax.nn.sigmoid(x)


def _dot(a, b):
    return jnp.dot(a, b, preferred_element_type=F32)


def _dot_nt(a, b):
    return lax.dot_general(a, b, (((1,), (1,)), ((), ())), preferred_element_type=F32)


def _bdot(a, b):
    return lax.dot_general(a, b, (((2,), (1,)), ((0,), (0,))), preferred_element_type=F32)


def _bdot_nt(a, b):
    return lax.dot_general(a, b, (((2,), (2,)), ((0,), (0,))), preferred_element_type=F32)


def _bdot_tn(a, b):
    return lax.dot_general(a, b, (((1,), (1,)), ((0,), (0,))), preferred_element_type=F32)


def _ffn_kernel(*refs, final, merge, ff_chunk):
    refs = list(refs)
    x_ref = refs.pop(0)
    if merge:
        ga_ref, gb_ref, oa_ref, ob_ref, wo_ref = refs[:5]
        refs = refs[5:]
    g_ref, wg_ref, wu_ref, wd_ref = refs[:4]
    refs = refs[4:]
    if final:
        fg_ref = refs.pop(0)
    o_ref, act_ref = refs

    x = x_ref[...]
    if merge:
        merged = (jax.nn.sigmoid(ga_ref[...].astype(F32)) * oa_ref[...].astype(F32)
                  + jax.nn.sigmoid(gb_ref[...].astype(F32)) * ob_ref[...].astype(F32))
        x = x + _dot(merged.astype(BF16), wo_ref[...])
    h = _rms(x, g_ref[...]).astype(BF16)
    ff = wg_ref.shape[1]
    for c0 in range(0, ff, ff_chunk):
        c1 = min(c0 + ff_chunk, ff)
        a = _dot(h, wg_ref[:, c0:c1])
        u = _dot(h, wu_ref[:, c0:c1])
        act_ref[:, c0:c1] = (_silu(a) * u).astype(BF16)
    y = x + 0.5 * _dot(act_ref[...], wd_ref[...])
    if final:
        y = _rms(y, fg_ref[...])
    o_ref[...] = y


def _ffn(x2d, g, wg, wu, wd, *, final_g=None, merge=None, tm=512, ff_chunk=256):
    n, d = x2d.shape
    ff = wg.shape[1]
    row = lambda cols, blk=0: pl.BlockSpec((tm, cols), lambda i, blk=blk: (i, blk))
    in_specs = [row(d)]
    args = [x2d]
    if merge is not None:
        big2d, gate_blk, o_dn, o_sa, w_o = merge
        in_specs += [row(d, gate_blk), row(d, gate_blk + 1), row(d), row(d), _resident((d, d))]
        args += [big2d, big2d, o_dn, o_sa, w_o.astype(BF16)]
    in_specs += [_resident((1, d)), _resident((d, ff)), _resident((d, ff)), _resident((ff, d))]
    args += [g.reshape(1, d), wg.astype(BF16), wu.astype(BF16), wd.astype(BF16)]
    if final_g is not None:
        in_specs.append(_resident((1, d)))
        args.append(final_g.reshape(1, d))
    return pl.pallas_call(
        functools.partial(_ffn_kernel, final=final_g is not None, merge=merge is not None,
                          ff_chunk=ff_chunk),
        grid=(n // tm,),
        in_specs=in_specs,
        out_specs=row(d),
        out_shape=jax.ShapeDtypeStruct((n, d), F32),
        scratch_shapes=[pltpu.VMEM((tm, ff), BF16)],
        compiler_params=_cparams(("parallel",)),
        name="ffn_merge" if merge is not None else "ffn",
    )(*args)


def _small_layout(q_rank, kv_rank, idx_dim):
    o_cq = 0
    o_ckv = o_cq + q_rank
    o_ik = o_ckv + kv_rank
    o_misc = o_ik + LANE * ((idx_dim + LANE - 1) // LANE)
    return o_cq, o_ckv, o_ik, o_misc, o_misc + LANE


def _tri_inverse(lmat, masks, eye):
    x = eye - lmat * masks[0]
    for mk in masks[1:]:
        lm = (lmat * mk).astype(BF16)
        t = _bdot(x.astype(BF16), lm)
        x = x - _bdot(t.astype(BF16), x.astype(BF16))
    return x


def _dn_consts(c):
    row = lax.broadcasted_iota(jnp.int32, (c, c), 0)
    col = lax.broadcasted_iota(jnp.int32, (c, c), 1)
    tril = row >= col
    strict = row > col
    eye = jnp.where(row == col, 1.0, 0.0).astype(F32)
    tril_f = jnp.where(tril, 1.0, 0.0).astype(F32)
    masks = []
    m = 1
    while m < c:
        mk = strict & ((row // (2 * m)) == (col // (2 * m))) & ((row // m) != (col // m))
        masks.append(jnp.where(mk, 1.0, 0.0).astype(F32))
        m *= 2
    return tril, strict, eye, tril_f, masks


def _dn_conv(in_ref, rb, ts, cw_ref, xpad_ref, act_ref, keep, cols):
    halo = 8
    conv_k = cw_ref.shape[0]
    if keep is not None:
        xpad_ref[0:halo, cols] = xpad_ref[0:halo, cols] * keep
    xpad_ref[halo:halo + ts, cols] = in_ref[rb:rb + ts, cols].astype(F32)
    y = xpad_ref[halo:halo + ts, cols] * cw_ref[conv_k - 1:conv_k, cols]
    for dlt in range(1, conv_k):
        y = y + xpad_ref[halo - dlt:halo - dlt + ts, cols] * cw_ref[conv_k - 1 - dlt:conv_k - dlt, cols]
    act_ref[:, cols] = _silu(y)
    xpad_ref[0:halo, cols] = xpad_ref[ts:ts + halo, cols]


def _dn_stages(in_ref, bg_ref, mt_ref, rb, ts, act_ref, onorm, s, o_ref, consts, *, nh, dk):
    c = DN_CHUNK
    cpi = ts // c
    qk_w = nh * dk
    tril, strict, eye, tril_f, masks = consts

    def heads(r0, base):
        return [act_ref[r0:r0 + c, base + h * dk:base + (h + 1) * dk] for h in range(nh)]

    qs, ks, vs, betas, gccs, gcrs, glasts = [], [], [], [], [], [], []
    for cc in range(cpi):
        r0 = cc * c
        bg = bg_ref[rb + r0:rb + r0 + c, :]
        grow = mt_ref[2 * nh:3 * nh, rb + r0:rb + r0 + c]
        gc_col = jnp.dot(tril_f, bg, precision=lax.Precision.HIGHEST,
                         preferred_element_type=F32)
        gc_row = lax.dot_general(grow, tril_f, (((1,), (1,)), ((), ())),
                                 precision=lax.Precision.HIGHEST,
                                 preferred_element_type=F32)
        qs += heads(r0, 0)
        ks += heads(r0, qk_w)
        vs += heads(r0, 2 * qk_w)
        betas += [bg[:, nh + h:nh + h + 1] for h in range(nh)]
        gccs += [gc_col[:, 2 * nh + h:2 * nh + h + 1] for h in range(nh)]
        gcrs += [gc_row[h:h + 1, :] for h in range(nh)]
        glasts += [gc_row[h:h + 1, c - 1:c] for h in range(nh)]
    q = jnp.stack(qs)
    k = jnp.stack(ks)
    v = jnp.stack(vs)
    beta = jnp.stack(betas)
    gcc = jnp.stack(gccs)
    gcr = jnp.stack(gcrs)
    g_last = jnp.stack(glasts)
    q = q * (lax.rsqrt(jnp.sum(q * q, axis=-1, keepdims=True) + EPS) * (dk ** -0.5))
    k = k * lax.rsqrt(jnp.sum(k * k, axis=-1, keepdims=True) + EPS)
    decay = jnp.where(tril, jnp.exp(jnp.minimum(gcc - gcr, 0.0)), 0.0)
    eg = jnp.exp(gcc)
    kb = k * beta
    vb = v * beta
    aq = _bdot_nt(jnp.concatenate([kb, q], axis=1).astype(BF16), k.astype(BF16))
    lmat = jnp.where(strict, aq[:, :c] * decay, 0.0)
    intra = (aq[:, c:] * decay).astype(BF16)
    tinv = _tri_inverse(lmat, masks, eye)
    uw = _bdot(tinv.astype(BF16), jnp.concatenate([vb, kb * eg], axis=2).astype(BF16))
    u = uw[:, :, :dk]
    wq = jnp.concatenate([uw[:, :, dk:], q * eg], axis=1).astype(BF16)
    kd = (k * jnp.exp(g_last - gcc)).astype(BF16)
    sdec = jnp.exp(g_last)
    for cc in range(cpi):
        sl = slice(cc * nh, (cc + 1) * nh)
        m1 = _bdot(wq[sl], s.astype(BF16))
        vnb = (u[sl] - m1[:, :c]).astype(BF16)
        o = _rms(m1[:, c:] + _bdot(intra[sl], vnb), onorm)
        s = s * sdec[sl] + _bdot_tn(kd[sl], vnb)
        for h in range(nh):
            rows = slice(rb + cc * c, rb + (cc + 1) * c)
            zz = in_ref[rows, 3 * qk_w + h * dk:3 * qk_w + (h + 1) * dk].astype(F32)
            o_ref[rows, h * dk:(h + 1) * dk] = (o[h] * _silu(zz)).astype(o_ref.dtype)
    return s


def _inproj_dn_kernel(x_ref, g_ref, wb_ref, ws_ref, qn_ref, kvn_ref, ikg_ref, ikb_ref, negA_ref, dtb_ref,
                      cw_ref, on_ref,
                      gates_ref, cq_ref, ckv_ref, ckvt_ref, ik_ref, misct_ref, odn_ref,
                      qkvz_s, misc_s, mt_s, dn_in, dn_misc, dn_mt, xpad_ref, act_ref, s_ref,
                      *, q_rank, kv_rank, idx_dim, nh, dk, w_scale, col_chunk, seq, ts):
    o_cq, o_ckv, o_ik, o_misc, _ = _small_layout(q_rank, kv_rank, idx_dim)
    i = pl.program_id(0)
    tm = x_ref.shape[0]
    qkvz_w = qkvz_s.shape[1]

    @pl.when(i == 0)
    def _():
        qkvz_s[...] = jnp.zeros_like(qkvz_s)
        misc_s[...] = jnp.zeros_like(misc_s)
        mt_s[...] = jnp.zeros_like(mt_s)
        xpad_ref[0:8, :] = jnp.zeros((8, xpad_ref.shape[1]), F32)
        s_ref[...] = jnp.zeros_like(s_ref)

    dn_in[...] = qkvz_s[...]
    dn_misc[...] = misc_s[...]
    dn_mt[...] = mt_s[...]
    keep = jnp.where(((i - 1) * tm) % seq == 0, 0.0, 1.0)
    consts = _dn_consts(DN_CHUNK)
    onorm = on_ref[...]

    h = _rms(x_ref[...], g_ref[...]).astype(BF16)
    cols = wb_ref.shape[1]

    def proj_chunk(c0):
        c1 = min(c0 + col_chunk, cols)
        p = _dot(h, wb_ref[:, c0:c1]).astype(BF16)
        if c1 <= qkvz_w:
            qkvz_s[:, c0:c1] = p
        else:
            assert c0 >= qkvz_w
            gates_ref[:, c0 - qkvz_w:c1 - qkvz_w] = p

    def proj_small():
        p = _dot(h, ws_ref[...])
        cq_ref[...] = _rms(p[:, o_cq:o_cq + q_rank], qn_ref[...]).astype(cq_ref.dtype)
        ckv = _rms(p[:, o_ckv:o_ckv + kv_rank], kvn_ref[...])
        ckv_ref[...] = ckv.astype(ckv_ref.dtype)
        ones = jnp.ones((SUM_ROWS, ckv.shape[0]), F32)
        ckvt_ref[...] = jnp.concatenate([ckv.T, ones], axis=0).astype(ckvt_ref.dtype)
        ik = p[:, o_ik:o_ik + idx_dim]
        mu = jnp.mean(ik, axis=-1, keepdims=True)
        var = jnp.mean(jnp.square(ik - mu), axis=-1, keepdims=True)
        ik_ref[...] = ((ik - mu) * lax.rsqrt(var + EPS) * ikg_ref[...] + ikb_ref[...]).astype(ik_ref.dtype)
        m = p[:, o_misc:o_misc + LANE]
        lane = lax.broadcasted_iota(jnp.int32, m.shape, 1)
        beta = jax.nn.sigmoid(m)
        sp_in = m + dtb_ref[...]
        softplus = jnp.maximum(sp_in, 0.0) + jnp.log(1.0 + jnp.exp(-jnp.abs(sp_in)))
        gdec = negA_ref[...] * softplus
        misc = jnp.where(lane < nh, m * w_scale,
                         jnp.where(lane < 2 * nh, beta, jnp.where(lane < 3 * nh, gdec, 0.0)))
        misc_s[...] = misc
        mt = misc.T[0:MISC_ROWS, :]
        mt_s[...] = mt
        misct_ref[...] = mt

    chunks = list(range(0, cols, col_chunk))
    n_sub = tm // ts
    s = s_ref[...] * keep
    ci = 0
    for st in range(n_sub):
        for k0 in range(0, cw_ref.shape[1], col_chunk):
            _dn_conv(dn_in, st * ts, ts, cw_ref, xpad_ref, act_ref, keep if st == 0 else None,
                     slice(k0, k0 + col_chunk))
            if ci < len(chunks):
                proj_chunk(chunks[ci])
                ci += 1
        s = _dn_stages(dn_in, dn_misc, dn_mt, st * ts, ts, act_ref, onorm, s, odn_ref, consts, nh=nh, dk=dk)
    for c0 in chunks[ci:]:
        proj_chunk(c0)
    proj_small()
    s_ref[...] = s


def _in_proj_deltanet(x2d, g, w_big, w_small, q_norm, kv_norm, ik_g, ik_b, neg_a_pad, dtb_pad, conv_w,
                      out_norm, *, seq, q_rank, kv_rank, idx_dim, nh, dk, w_scale, tm=512, ts=256,
                      col_chunk=512):
    n, d = x2d.shape
    cols = w_big.shape[1]
    qk_w = nh * dk
    qkvz_w = 4 * qk_w
    gate_w = cols - qkvz_w
    nt = n // tm
    assert seq % tm == 0 and tm % ts == 0 and qkvz_w % col_chunk == 0
    cur = lambda i: jnp.minimum(i, nt - 1)
    prev = lambda i: jnp.maximum(i - 1, 0)
    row = lambda c: pl.BlockSpec((tm, c), lambda i: (cur(i), 0))
    colblk = lambda r: pl.BlockSpec((r, tm), lambda i: (0, cur(i)))
    return pl.pallas_call(
        functools.partial(_inproj_dn_kernel, q_rank=q_rank, kv_rank=kv_rank, idx_dim=idx_dim, nh=nh,
                          dk=dk, w_scale=w_scale, col_chunk=col_chunk, seq=seq, ts=ts),
        grid=(nt + 1,),
        in_specs=[
            row(d), _resident((1, d)), _resident((d, cols)), _resident((d, w_small.shape[1])),
            _resident((1, q_rank)), _resident((1, kv_rank)), _resident((1, idx_dim)),
            _resident((1, idx_dim)), _resident((1, LANE)), _resident((1, LANE)),
            _resident(conv_w.shape), _resident((1, dk)),
        ],
        out_specs=[row(gate_w), row(q_rank), row(kv_rank), colblk(kv_rank + SUM_ROWS), row(idx_dim),
                   colblk(MISC_ROWS), pl.BlockSpec((tm, qk_w), lambda i: (prev(i), 0))],
        out_shape=[
            jax.ShapeDtypeStruct((n, gate_w), BF16),
            jax.ShapeDtypeStruct((n, q_rank), BF16),
            jax.ShapeDtypeStruct((n, kv_rank), BF16),
            jax.ShapeDtypeStruct((kv_rank + SUM_ROWS, n), BF16),
            jax.ShapeDtypeStruct((n, idx_dim), BF16),
            jax.ShapeDtypeStruct((MISC_ROWS, n), F32),
            jax.ShapeDtypeStruct((n, qk_w), BF16),
        ],
        scratch_shapes=[
            pltpu.VMEM((tm, qkvz_w), BF16), pltpu.VMEM((tm, LANE), F32), pltpu.VMEM((MISC_ROWS, tm), F32),
            pltpu.VMEM((tm, qkvz_w), BF16), pltpu.VMEM((tm, LANE), F32), pltpu.VMEM((MISC_ROWS, tm), F32),
            pltpu.VMEM((ts + 8, 3 * qk_w), F32),
            pltpu.VMEM((ts, 3 * qk_w), F32),
            pltpu.VMEM((nh, dk, dk), F32),
        ],
        compiler_params=_cparams(("arbitrary",)),
        name="in_proj_deltanet",
    )(x2d, g.reshape(1, d), w_big, w_small, q_norm.reshape(1, -1), kv_norm.reshape(1, -1),
      ik_g.reshape(1, -1), ik_b.reshape(1, -1), neg_a_pad, dtb_pad, conv_w, out_norm.reshape(1, dk))


def _sortable(x):
    i = pltpu.bitcast(x, jnp.int32)
    return i ^ ((i >> 31) & jnp.int32(0x7FFFFFFF))


def _bit_transpose32(words):
    a = list(words)
    assert len(a) == 32
    j, m = 16, 0x0000FFFF
    while j:
        for k in range(32):
            if not k & j:
                t = (lax.shift_right_logical(a[k], jnp.int32(j)) ^ a[k + j]) & jnp.int32(m)
                a[k + j] = a[k + j] ^ t
                a[k] = a[k] ^ (t << j)
        j >>= 1
        m = (m ^ (m << j)) & 0xFFFFFFFF if j else m
    return a


def _dsa_kernel(cq_ref, mtc_ref, mtn_ref, ki_ref, ckv_ref, ckvt_ref, wiq_ref, wuq_ref, wuk_ref, wuv_ref,
                bias_ref, o_ref, key_ref, keyn_ref, planes_ref, thr_s, acc_ref, lg0_ref, lg1_ref,
                qidx_s, qlat_s, olat_s, *, nh_idx, idx_dim, nh, dh, topk, scale, tks, tka, grp):
    tq = LANE
    sub = LANE
    j = pl.program_id(1)
    nb = pl.num_programs(1)
    jg = j % grp
    rows = lax.broadcasted_iota(jnp.int32, (sub, tq), 0)
    cols = lax.broadcasted_iota(jnp.int32, (sub, tq), 1)
    rows_s = lax.broadcasted_iota(jnp.int32, (tks, tq), 0)
    nt = planes_ref.shape[1]
    assert nt % 2 == 0 and tks == tka

    def n_tiles(blk):
        return (blk * tq + tq + tks - 1) // tks

    def project_group(first_slot):
        cqg = cq_ref[...]
        qi = _dot_nt(wiq_ref[...], cqg).astype(BF16)
        qg = _dot_nt(wuq_ref[...], cqg).astype(BF16)
        for g in range(grp):
            qidx_s[first_slot + g] = jnp.concatenate(
                [qi[h * idx_dim:(h + 1) * idx_dim, g * tq:(g + 1) * tq] for h in range(nh_idx)], axis=1)
        for h in range(nh):
            ql = (_dot(wuk_ref[h], qg[h * dh:(h + 1) * dh, :]) * (scale * LOG2E)).astype(BF16)
            for g in range(grp):
                qlat_s[first_slot + g, :, h * tq:(h + 1) * tq] = ql[:, g * tq:(g + 1) * tq]

    @pl.when((pl.program_id(0) == 0) & (j == 0))
    def _():
        planes_ref[...] = jnp.zeros_like(planes_ref)
        keyn_ref[...] = jnp.zeros_like(keyn_ref)

    @pl.when(j == 0)
    def _():
        project_group(0)

    @pl.when((j > 0) & ((j + 1) % grp == 0) & (j + 1 < nb))
    def _():
        project_group((((j + 1) // grp) % 2) * grp)

    def head_weights(mt_ref):
        return jnp.concatenate([mt_ref[h:h + 1, :] for h in range(nh_idx)], axis=1)

    def score_tile(kt, blk, kref, wrow_all):
        slot = blk % (2 * grp)
        qpos = blk * tq + cols
        words = []
        for r in range(tks // sub):
            s0 = pl.multiple_of(kt * tks + r * sub, sub)
            kk = ki_ref[pl.ds(s0, sub), :]
            d = jnp.maximum(_dot(kk, qidx_s[slot]), 0.0) * wrow_all
            parts = [d[:, h * tq:(h + 1) * tq] for h in range(nh_idx)]
            while len(parts) > 1:
                parts = [parts[a] + parts[a + 1] for a in range(0, len(parts), 2)]
            causal = (s0 + rows) <= qpos
            key = jnp.where(causal, _sortable(parts[0]), INT_MIN)
            kref[pl.ds(s0, sub), :] = key
            ukey = key ^ INT_MIN
            words += [ukey[8 * w:8 * (w + 1), :] for w in range(sub // 8)]
        for b, plane in enumerate(_bit_transpose32(words)):
            planes_ref[b, kt] = plane

    def score_pair(t, blk, kref, wrow_all):
        score_tile(2 * t, blk, kref, wrow_all)
        score_tile(2 * t + 1, blk, kref, wrow_all)

    def select(blk, kref):
        n_s = n_tiles(blk)
        tile_ids = lax.broadcasted_iota(jnp.int32, (nt, 8, tq), 0)
        eq0 = jnp.where(tile_ids < n_s, -1, 0).astype(jnp.int32)

        ans_u = cnt_gt = jnp.zeros((1, tq), jnp.int32)
        eq = eq0
        for b in range(31, -1, -1):
            t = eq & planes_ref[b]
            c = jnp.sum(jnp.sum(lax.population_count(t), axis=0), axis=0, keepdims=True)
            accept = (cnt_gt + c) >= topk
            ans_u = jnp.where(accept, ans_u | jnp.int32(INT_MIN if b == 31 else 1 << b), ans_u)
            cnt_gt = jnp.where(accept, cnt_gt, cnt_gt + c)
            eq = jnp.where(accept, t, eq ^ t)
        thr = jnp.maximum(ans_u ^ INT_MIN, INT_MIN + 1)

        n_eq = jnp.sum(jnp.sum(lax.population_count(eq), axis=0), axis=0, keepdims=True)
        any_tie = jnp.max(jnp.where((cnt_gt + n_eq > topk) & (ans_u != 0), 1, 0)) > 0

        def count_tiles(hit_fn):
            def body(kt, acc):
                s0 = pl.multiple_of(kt * tks, tks)
                hit = hit_fn(kref[pl.ds(s0, tks), :], s0)
                return acc + jnp.sum(hit.reshape(tks // 32, 4, 8, tq), axis=0)
            acc = lax.fori_loop(0, n_s, body, jnp.zeros((4, 8, tq), jnp.int32))
            return jnp.sum(acc.reshape(32, tq), axis=0, keepdims=True)

        @pl.when(any_tie)
        def _():
            n_gt = count_tiles(lambda kv, s0: jnp.where(kv > thr, 1, 0).astype(jnp.int32))
            need = topk - n_gt

            def count_eq_below(lim):
                return count_tiles(lambda kv, s0: jnp.where((kv == thr) & ((s0 + rows_s) < lim), 1, 0)
                                   .astype(jnp.int32))

            nbits = int(math.ceil(math.log2(key_ref.shape[0]))) + 1

            def jbit_body(bi, lim):
                cand = lim | jnp.left_shift(jnp.int32(1), nbits - 1 - bi)
                return jnp.where(count_eq_below(cand) <= need, cand, lim)

            lim = lax.fori_loop(0, nbits, jbit_body, jnp.zeros((1, tq), jnp.int32))

            def demote_body(kt, carry):
                s0 = pl.multiple_of(kt * tks, tks)
                kv = kref[pl.ds(s0, tks), :]
                drop = (kv == thr) & ((s0 + rows_s) >= lim)
                kref[pl.ds(s0, tks), :] = jnp.where(drop, INT_MIN, kv)
                return carry

            lax.fori_loop(0, n_s, demote_body, 0)

        thr_s[0:1, :] = thr

    @pl.when(j == 0)
    def _():
        w0 = head_weights(mtc_ref)
        lax.fori_loop(0, (n_tiles(0) + 1) // 2, lambda t, c: (score_pair(t, 0, key_ref, w0), c)[1], 0)
        select(0, key_ref)

    @pl.when(j > 0)
    def _():
        key_ref[...] = keyn_ref[...]

    thr = thr_s[0:1, :]

    acc_ref[...] = jnp.zeros_like(acc_ref)
    nsub = tka // sub
    n_a = n_tiles(j)
    last_tile = n_a - 1
    int_max = jnp.int32(2 ** 31 - 1)
    qslot = j % (2 * grp)

    def logits_to(buf_ref, kt):
        s0 = pl.multiple_of(jnp.minimum(kt, last_tile) * tka, tka)
        buf_ref[...] = _dot(ckv_ref[pl.ds(s0, tka), :], qlat_s[qslot]).astype(buf_ref.dtype)

    def consume(buf_ref, kt, m_run, near):
        ktc = jnp.minimum(kt, last_tile)
        s0 = pl.multiple_of(ktc * tka, tka)
        thr_v = jnp.where(kt <= last_tile, thr, int_max)
        parts = []
        for r in range(nsub):
            kv = key_ref[pl.ds(pl.multiple_of(s0 + r * sub, sub), sub), :]
            madd = jnp.where(kv >= thr_v, 0.0, NEG_BIG)
            if near:
                madd = jnp.concatenate([madd] * nh, axis=1) + bias_ref[jnp.clip(j - (ktc * nsub + r), 0, 2)]
                madd = madd.astype(BF16)
            else:
                madd = jnp.concatenate([madd.astype(BF16)] * nh, axis=1)
            parts.append(buf_ref[r * sub:(r + 1) * sub, :] + madd)
        lg = jnp.concatenate(parts, axis=0)
        tile_max = jnp.max(lg.reshape(tka // 16, 16, nh * tq), axis=0).astype(F32)
        m_new = jnp.maximum(m_run, jnp.max(tile_max, axis=0, keepdims=True))
        p = jnp.exp2(lg - m_new.astype(BF16))
        alpha = jnp.exp2(m_run - m_new)
        ct = ckvt_ref[:, pl.ds(s0, tka)]
        acc_ref[...] = acc_ref[...] * alpha + _dot(ct, p)
        return m_new

    def pair_body(t, m_run, near, with_scores):
        kt = 2 * t
        logits_to(lg1_ref, kt + 1)
        m_run = consume(lg0_ref, kt, m_run, near)
        if with_scores:
            score_tile(kt, j + 1, keyn_ref, wrow_n)
        logits_to(lg0_ref, kt + 2)
        m_run = consume(lg1_ref, kt + 1, m_run, near)
        if with_scores:
            score_tile(kt + 1, j + 1, keyn_ref, wrow_n)
        return m_run

    def quad_body(q, m_run, with_scores):
        m_run = pair_body(2 * q, m_run, False, with_scores)
        return pair_body(2 * q + 1, m_run, False, with_scores)

    m0 = jnp.full((1, nh * tq), NEG_BIG, F32).astype(BF16).astype(F32)
    n_pairs = (n_a + 1) // 2
    far_quads = jnp.maximum((j - 1) // (2 * nsub), 0) // 2
    logits_to(lg0_ref, 0)
    wrow_n = head_weights(mtn_ref)

    @pl.when(j + 1 < nb)
    def _():
        m_run = lax.fori_loop(0, far_quads, functools.partial(quad_body, with_scores=True), m0)
        lax.fori_loop(2 * far_quads, n_pairs, functools.partial(pair_body, near=True, with_scores=True), m_run)
        lax.fori_loop(n_pairs, (n_tiles(j + 1) + 1) // 2,
                      lambda t, c: (score_pair(t, j + 1, keyn_ref, wrow_n), c)[1], 0)

    @pl.when(j + 1 >= nb)
    def _():
        m_run = lax.fori_loop(0, far_quads, functools.partial(quad_body, with_scores=False), m0)
        lax.fori_loop(2 * far_quads, n_pairs, functools.partial(pair_body, near=True, with_scores=False), m_run)

    kvr = acc_ref.shape[0] - SUM_ROWS
    l_fin = acc_ref[kvr:kvr + 1, :]
    olat_s[jg] = (acc_ref[0:kvr, :] * (1.0 / l_fin)).astype(BF16)

    @pl.when(jg == grp - 1)
    def _():
        for h in range(nh):
            ol = jnp.concatenate([olat_s[g, :, h * tq:(h + 1) * tq] for g in range(grp)], axis=1)
            o_ref[:, h * dh:(h + 1) * dh] = _dot(wuv_ref[h], ol).T.astype(o_ref.dtype)

    @pl.when(j + 1 < nb)
    def _():
        select(j + 1, keyn_ref)


def _dsa(cq, misc_t, kidx, ckv, ckv_t, wiq_t, wuq_t, wuk_h, wuv_h, bias_tab, *, batch, topk, scale):
    n, q_rank = cq.shape
    s = n // batch
    kv_rank = ckv.shape[-1]
    idx_dim = kidx.shape[-1]
    nh_idx = wiq_t.shape[0] // idx_dim
    nh, dh, _ = wuv_h.shape
    tq = LANE
    nb = s // tq
    tks = 256
    tka = 256
    grp = 4
    assert s % tks == 0 and s % tka == 0 and nb % grp == 0
    last_blk = batch * nb - 1
    prep = lambda bi, j: jnp.minimum(bi * nb + j + jnp.where(j > 0, 1, 0), last_blk)
    return pl.pallas_call(
        functools.partial(_dsa_kernel, nh_idx=nh_idx, idx_dim=idx_dim, nh=nh, dh=dh, topk=topk,
                          scale=scale, tks=tks, tka=tka, grp=grp),
        grid=(batch, nb),
        in_specs=[
            pl.BlockSpec((grp * tq, q_rank), lambda bi, j: (prep(bi, j) // grp, 0)),
            pl.BlockSpec((MISC_ROWS, tq), lambda bi, j: (0, bi * nb + j)),
            pl.BlockSpec((MISC_ROWS, tq), lambda bi, j: (0, jnp.minimum(bi * nb + j + 1, last_blk))),
            pl.BlockSpec((s, idx_dim), lambda bi, j: (bi, 0)),
            pl.BlockSpec((s, kv_rank), lambda bi, j: (bi, 0)),
            pl.BlockSpec((kv_rank + SUM_ROWS, s), lambda bi, j: (0, bi)),
            _resident(wiq_t.shape), _resident(wuq_t.shape), _resident(wuk_h.shape),
            _resident(wuv_h.shape), _resident(bias_tab.shape),
        ],
        out_specs=pl.BlockSpec((grp * tq, nh * dh), lambda bi, j: ((bi * nb + j) // grp, 0)),
        out_shape=jax.ShapeDtypeStruct((n, nh * dh), BF16),
        scratch_shapes=[
            pltpu.VMEM((s, tq), jnp.int32),
            pltpu.VMEM((s, tq), jnp.int32),
            pltpu.VMEM((32, s // tks, 8, tq), jnp.int32),
            pltpu.VMEM((8, tq), jnp.int32),
            pltpu.VMEM((kv_rank + SUM_ROWS, nh * tq), F32),
            pltpu.VMEM((tka, nh * tq), BF16),
            pltpu.VMEM((tka, nh * tq), BF16),
            pltpu.VMEM((2 * grp, idx_dim, nh_idx * tq), BF16),
            pltpu.VMEM((2 * grp, kv_rank, nh * tq), BF16),
            pltpu.VMEM((grp, kv_rank, nh * tq), BF16),
        ],
        compiler_params=_cparams(("arbitrary", "arbitrary")),
        name="dsa",
    )(cq, misc_t, misc_t, kidx, ckv, ckv_t, wiq_t, wuq_t, wuk_h, wuv_h, bias_tab)


def _t5_bucket_np(n, n_buckets):
    max_exact = n_buckets // 2
    nf = np.maximum(n, 1).astype(np.float32)
    large = max_exact + (np.log(nf / max_exact) / math.log(REL_MAX_DIST / max_exact)
                         * (n_buckets - max_exact)).astype(np.int32)
    large = np.minimum(large, n_buckets - 1)
    return np.where(n < max_exact, n, large)


def _bias_table(rel_bias, tq):
    n_buckets, nh = rel_bias.shape
    assert _t5_bucket_np(np.array([tq + 1]), n_buckets)[0] == n_buckets - 1
    s_loc = np.arange(tq)[:, None]
    t_loc = np.arange(tq)[None, :]
    dist = np.stack([np.maximum(d * tq + t_loc - s_loc, 0) for d in range(3)])
    onehot = np.eye(n_buckets, dtype=np.float32)[_t5_bucket_np(dist, n_buckets)]
    tb = jnp.einsum('dstb,bh->dsht', jnp.asarray(onehot), rel_bias.astype(F32),
                    precision=lax.Precision.HIGHEST)
    far = rel_bias[n_buckets - 1].astype(F32)[None, None, :, None]
    return ((tb - far) * LOG2E).reshape(3, tq, nh * tq)


def kernel(x, ffn1_norm, ffn1_wg, ffn1_wu, ffn1_wd, mix_norm, w_in, conv_w, a_log, dt_bias, dn_out_norm, q_norm, kv_norm, w_uq, w_uk, w_uv, w_iq, idx_k_g, idx_k_b, rel_bias, w_o, ffn2_norm, ffn2_wg, ffn2_wu, ffn2_wd, final_norm):
    b, s, d = x.shape
    depth = ffn1_norm.shape[0]
    nh_dn = a_log.shape[1]
    dk = dn_out_norm.shape[1]
    qk_w = nh_dn * dk
    q_rank = q_norm.shape[1]
    kv_rank = kv_norm.shape[1]
    idx_dim = idx_k_g.shape[1]
    nh_idx = w_iq.shape[2] // idx_dim
    nh_sa, dh_sa = w_uk.shape[2], w_uk.shape[3]
    assert qk_w == d and nh_sa * dh_sa == d and nh_idx == nh_dn and 3 * nh_dn <= MISC_ROWS
    topk = min(TOPK_MAX, s // 4)
    n = b * s

    sizes = (3 * qk_w, qk_w, nh_dn, nh_dn, q_rank, kv_rank, idx_dim, nh_idx, d, d)
    offs = np.concatenate([[0], np.cumsum(sizes)])
    (o_qkv, o_z, o_b, o_a, o_cq, o_ckv, o_ik, o_iw, o_ga, o_gb) = [int(v) for v in offs[:-1]]
    l_cq, l_ckv, l_ik, l_misc, small_cols = _small_layout(q_rank, kv_rank, idx_dim)
    bias_tab = _bias_table(rel_bias, LANE)

    x2d = x.reshape(n, d)
    for l in range(depth):
        x1 = _ffn(x2d, ffn1_norm[l], ffn1_wg[l], ffn1_wu[l], ffn1_wd[l])

        wl = w_in[l]
        w_big = jnp.concatenate([wl[:, o_qkv:o_qkv + 4 * qk_w], wl[:, o_ga:o_ga + 2 * d]],
                                axis=1).astype(BF16)
        w_small = jnp.zeros((d, small_cols), F32)
        w_small = w_small.at[:, l_cq:l_cq + q_rank].set(wl[:, o_cq:o_cq + q_rank])
        w_small = w_small.at[:, l_ckv:l_ckv + kv_rank].set(wl[:, o_ckv:o_ckv + kv_rank])
        w_small = w_small.at[:, l_ik:l_ik + idx_dim].set(wl[:, o_ik:o_ik + idx_dim])
        w_small = w_small.at[:, l_misc:l_misc + nh_idx].set(wl[:, o_iw:o_iw + nh_idx])
        w_small = w_small.at[:, l_misc + nh_dn:l_misc + 2 * nh_dn].set(wl[:, o_b:o_b + nh_dn])
        w_small = w_small.at[:, l_misc + 2 * nh_dn:l_misc + 3 * nh_dn].set(wl[:, o_a:o_a + nh_dn])
        neg_a_pad = jnp.zeros((1, LANE), F32).at[0, 2 * nh_dn:3 * nh_dn].set(-jnp.exp(a_log[l].astype(F32)))
        dtb_pad = jnp.zeros((1, LANE), F32).at[0, 2 * nh_dn:3 * nh_dn].set(dt_bias[l].astype(F32))
        gates, cq, ckv, ckv_t, kidx, misc_t, o_dn = _in_proj_deltanet(
            x1, mix_norm[l], w_big, w_small.astype(BF16), q_norm[l], kv_norm[l], idx_k_g[l],
            idx_k_b[l], neg_a_pad, dtb_pad, conv_w[l].astype(F32), dn_out_norm[l], seq=s,
            q_rank=q_rank, kv_rank=kv_rank, idx_dim=idx_dim, nh=nh_dn, dk=dk,
            w_scale=(nh_idx ** -0.5) * (idx_dim ** -0.5))

        o_sa = _dsa(cq, misc_t, kidx, ckv, ckv_t,
                    w_iq[l].T.astype(BF16), w_uq[l].T.astype(BF16),
                    jnp.transpose(w_uk[l], (1, 0, 2)).astype(BF16),
                    jnp.transpose(w_uv[l], (1, 2, 0)).astype(BF16),
                    bias_tab, batch=b, topk=topk, scale=dh_sa ** -0.5)

        x2d = _ffn(x1, ffn2_norm[l], ffn2_wg[l], ffn2_wu[l], ffn2_wd[l],
                   final_g=final_norm if l == depth - 1 else None,
                   merge=(gates, 0, o_dn, o_sa, w_o[l]))
    return x2d.reshape(b, s, d)
```

```python
import functools
import math

import numpy as np
import jax
import jax.numpy as jnp
from jax import lax
from jax.experimental import pallas as pl
from jax.experimental.pallas import tpu as pltpu

F32 = jnp.float32
BF16 = jnp.bfloat16
EPS = 1e-6
TOPK_MAX = 256
REL_MAX_DIST = 128
DN_CHUNK = 64
LANE = 128
SUM_ROWS = 8
MISC_ROWS = 32
INT_MIN = -(2 ** 31)
NEG_BIG = -1e30
LOG2E = math.log2(math.e)
VMEM_LIMIT = 56 * 1024 * 1024


def _cparams(sem):
    return pltpu.CompilerParams(dimension_semantics=sem, vmem_limit_bytes=VMEM_LIMIT)


def _resident(shape):
    return pl.BlockSpec(shape, lambda *_: (0,) * len(shape), pipeline_mode=pl.Buffered(1))


def _rms(x, g):
    return x * lax.rsqrt(jnp.mean(x * x, axis=-1, keepdims=True) + EPS) * g


def _silu(x):
    return x * jax.nn.sigmoid(x)


def _dot(a, b):
    return jnp.dot(a, b, preferred_element_type=F32)


def _dot_nt(a, b):
    return lax.dot_general(a, b, (((1,), (1,)), ((), ())), preferred_element_type=F32)


def _bdot(a, b):
    return lax.dot_general(a, b, (((2,), (1,)), ((0,), (0,))), preferred_element_type=F32)


def _bdot_nt(a, b):
    return lax.dot_general(a, b, (((2,), (2,)), ((0,), (0,))), preferred_element_type=F32)


def _bdot_tn(a, b):
    return lax.dot_general(a, b, (((1,), (1,)), ((0,), (0,))), preferred_element_type=F32)


def _ffn_kernel(*refs, final, merge, ff_chunk):
    refs = list(refs)
    x_ref = refs.pop(0)
    if merge:
        ga_ref, gb_ref, oa_ref, ob_ref, wo_ref = refs[:5]
        refs = refs[5:]
    g_ref, wg_ref, wu_ref, wd_ref = refs[:4]
    refs = refs[4:]
    if final:
        fg_ref = refs.pop(0)
    o_ref, act_ref = refs

    x = x_ref[...]
    if merge:
        merged = (jax.nn.sigmoid(ga_ref[...].astype(F32)) * oa_ref[...].astype(F32)
                  + jax.nn.sigmoid(gb_ref[...].astype(F32)) * ob_ref[...].astype(F32))
        x = x + _dot(merged.astype(BF16), wo_ref[...])
    h = _rms(x, g_ref[...]).astype(BF16)
    ff = wg_ref.shape[1]
    for c0 in range(0, ff, ff_chunk):
        c1 = min(c0 + ff_chunk, ff)
        a = _dot(h, wg_ref[:, c0:c1])
        u = _dot(h, wu_ref[:, c0:c1])
        act_ref[:, c0:c1] = (_silu(a) * u).astype(BF16)
    y = x + 0.5 * _dot(act_ref[...], wd_ref[...])
    if final:
        y = _rms(y, fg_ref[...])
    o_ref[...] = y


def _ffn(x2d, g, wg, wu, wd, *, final_g=None, merge=None, tm=512, ff_chunk=256):
    n, d = x2d.shape
    ff = wg.shape[1]
    row = lambda cols, blk=0: pl.BlockSpec((tm, cols), lambda i, blk=blk: (i, blk))
    in_specs = [row(d)]
    args = [x2d]
    if merge is not None:
        big2d, gate_blk, o_dn, o_sa, w_o = merge
        in_specs += [row(d, gate_blk), row(d, gate_blk + 1), row(d), row(d), _resident((d, d))]
        args += [big2d, big2d, o_dn, o_sa, w_o.astype(BF16)]
    in_specs += [_resident((1, d)), _resident((d, ff)), _resident((d, ff)), _resident((ff, d))]
    args += [g.reshape(1, d), wg.astype(BF16), wu.astype(BF16), wd.astype(BF16)]
    if final_g is not None:
        in_specs.append(_resident((1, d)))
        args.append(final_g.reshape(1, d))
    return pl.pallas_call(
        functools.partial(_ffn_kernel, final=final_g is not None, merge=merge is not None,
                          ff_chunk=ff_chunk),
        grid=(n // tm,),
        in_specs=in_specs,
        out_specs=row(d),
        out_shape=jax.ShapeDtypeStruct((n, d), F32),
        scratch_shapes=[pltpu.VMEM((tm, ff), BF16)],
        compiler_params=_cparams(("parallel",)),
        name="ffn_merge" if merge is not None else "ffn",
    )(*args)


def _small_layout(q_rank, kv_rank, idx_dim):
    o_cq = 0
    o_ckv = o_cq + q_rank
    o_ik = o_ckv + kv_rank
    o_misc = o_ik + LANE * ((idx_dim + LANE - 1) // LANE)
    return o_cq, o_ckv, o_ik, o_misc, o_misc + LANE


def _tri_inverse(lmat, masks, eye):
    x = eye - lmat * masks[0]
    for mk in masks[1:]:
        lm = (lmat * mk).astype(BF16)
        t = _bdot(x.astype(BF16), lm)
        x = x - _bdot(t.astype(BF16), x.astype(BF16))
    return x


def _dn_consts(c):
    row = lax.broadcasted_iota(jnp.int32, (c, c), 0)
    col = lax.broadcasted_iota(jnp.int32, (c, c), 1)
    tril = row >= col
    strict = row > col
    eye = jnp.where(row == col, 1.0, 0.0).astype(F32)
    tril_f = jnp.where(tril, 1.0, 0.0).astype(F32)
    masks = []
    m = 1
    while m < c:
        mk = strict & ((row // (2 * m)) == (col // (2 * m))) & ((row // m) != (col // m))
        masks.append(jnp.where(mk, 1.0, 0.0).astype(F32))
        m *= 2
    return tril, strict, eye, tril_f, masks


def _dn_conv(in_ref, rb, ts, cw_ref, xpad_ref, act_ref, keep, cols):
    halo = 8
    conv_k = cw_ref.shape[0]
    if keep is not None:
        xpad_ref[0:halo, cols] = xpad_ref[0:halo, cols] * keep
    xpad_ref[halo:halo + ts, cols] = in_ref[rb:rb + ts, cols].astype(F32)
    y = xpad_ref[halo:halo + ts, cols] * cw_ref[conv_k - 1:conv_k, cols]
    for dlt in range(1, conv_k):
        y = y + xpad_ref[halo - dlt:halo - dlt + ts, cols] * cw_ref[conv_k - 1 - dlt:conv_k - dlt, cols]
    act_ref[:, cols] = _silu(y)
    xpad_ref[0:halo, cols] = xpad_ref[ts:ts + halo, cols]


def _dn_stages(in_ref, bg_ref, mt_ref, rb, ts, act_ref, onorm, s, o_ref, consts, *, nh, dk):
    c = DN_CHUNK
    cpi = ts // c
    qk_w = nh * dk
    tril, strict, eye, tril_f, masks = consts

    def heads(r0, base):
        return [act_ref[r0:r0 + c, base + h * dk:base + (h + 1) * dk] for h in range(nh)]

    qs, ks, vs, betas, gccs, gcrs, glasts = [], [], [], [], [], [], []
    for cc in range(cpi):
        r0 = cc * c
        bg = bg_ref[rb + r0:rb + r0 + c, :]
        grow = mt_ref[2 * nh:3 * nh, rb + r0:rb + r0 + c]
        gc_col = jnp.dot(tril_f, bg, precision=lax.Precision.HIGHEST,
                         preferred_element_type=F32)
        gc_row = lax.dot_general(grow, tril_f, (((1,), (1,)), ((), ())),
                                 precision=lax.Precision.HIGHEST,
                                 preferred_element_type=F32)
        qs += heads(r0, 0)
        ks += heads(r0, qk_w)
        vs += heads(r0, 2 * qk_w)
        betas += [bg[:, nh + h:nh + h + 1] for h in range(nh)]
        gccs += [gc_col[:, 2 * nh + h:2 * nh + h + 1] for h in range(nh)]
        gcrs += [gc_row[h:h + 1, :] for h in range(nh)]
        glasts += [gc_row[h:h + 1, c - 1:c] for h in range(nh)]
    q = jnp.stack(qs)
    k = jnp.stack(ks)
    v = jnp.stack(vs)
    beta = jnp.stack(betas)
    gcc = jnp.stack(gccs)
    gcr = jnp.stack(gcrs)
    g_last = jnp.stack(glasts)
    q = q * (lax.rsqrt(jnp.sum(q * q, axis=-1, keepdims=True) + EPS) * (dk ** -0.5))
    k = k * lax.rsqrt(jnp.sum(k * k, axis=-1, keepdims=True) + EPS)
    decay = jnp.where(tril, jnp.exp(jnp.minimum(gcc - gcr, 0.0)), 0.0)
    eg = jnp.exp(gcc)
    kb = k * beta
    vb = v * beta
    aq = _bdot_nt(jnp.concatenate([kb, q], axis=1).astype(BF16), k.astype(BF16))
    lmat = jnp.where(strict, aq[:, :c] * decay, 0.0)
    intra = (aq[:, c:] * decay).astype(BF16)
    tinv = _tri_inverse(lmat, masks, eye)
    uw = _bdot(tinv.astype(BF16), jnp.concatenate([vb, kb * eg], axis=2).astype(BF16))
    u = uw[:, :, :dk]
    wq = jnp.concatenate([uw[:, :, dk:], q * eg], axis=1).astype(BF16)
    kd = (k * jnp.exp(g_last - gcc)).astype(BF16)
    sdec = jnp.exp(g_last)
    for cc in range(cpi):
        sl = slice(cc * nh, (cc + 1) * nh)
        m1 = _bdot(wq[sl], s.astype(BF16))
        vnb = (u[sl] - m1[:, :c]).astype(BF16)
        o = _rms(m1[:, c:] + _bdot(intra[sl], vnb), onorm)
        s = s * sdec[sl] + _bdot_tn(kd[sl], vnb)
        for h in range(nh):
            rows = slice(rb + cc * c, rb + (cc + 1) * c)
            zz = in_ref[rows, 3 * qk_w + h * dk:3 * qk_w + (h + 1) * dk].astype(F32)
            o_ref[rows, h * dk:(h + 1) * dk] = (o[h] * _silu(zz)).astype(o_ref.dtype)
    return s


def _inproj_dn_kernel(x_ref, g_ref, wb_ref, ws_ref, qn_ref, kvn_ref, ikg_ref, ikb_ref, negA_ref, dtb_ref,
                      cw_ref, on_ref,
                      gates_ref, cq_ref, ckv_ref, ckvt_ref, ik_ref, misct_ref, odn_ref,
                      qkvz_s, misc_s, mt_s, dn_in, dn_misc, dn_mt, xpad_ref, act_ref, s_ref,
                      *, q_rank, kv_rank, idx_dim, nh, dk, w_scale, col_chunk, seq, ts):
    o_cq, o_ckv, o_ik, o_misc, _ = _small_layout(q_rank, kv_rank, idx_dim)
    i = pl.program_id(0)
    tm = x_ref.shape[0]
    qkvz_w = qkvz_s.shape[1]

    @pl.when(i == 0)
    def _():
        qkvz_s[...] = jnp.zeros_like(qkvz_s)
        misc_s[...] = jnp.zeros_like(misc_s)
        mt_s[...] = jnp.zeros_like(mt_s)
        xpad_ref[0:8, :] = jnp.zeros((8, xpad_ref.shape[1]), F32)
        s_ref[...] = jnp.zeros_like(s_ref)

    dn_in[...] = qkvz_s[...]
    dn_misc[...] = misc_s[...]
    dn_mt[...] = mt_s[...]
    keep = jnp.where(((i - 1) * tm) % seq == 0, 0.0, 1.0)
    consts = _dn_consts(DN_CHUNK)
    onorm = on_ref[...]

    h = _rms(x_ref[...], g_ref[...]).astype(BF16)
    cols = wb_ref.shape[1]

    def proj_chunk(c0):
        c1 = min(c0 + col_chunk, cols)
        p = _dot(h, wb_ref[:, c0:c1]).astype(BF16)
        if c1 <= qkvz_w:
            qkvz_s[:, c0:c1] = p
        else:
            assert c0 >= qkvz_w
            gates_ref[:, c0 - qkvz_w:c1 - qkvz_w] = p

    def proj_small():
        p = _dot(h, ws_ref[...])
        cq_ref[...] = _rms(p[:, o_cq:o_cq + q_rank], qn_ref[...]).astype(cq_ref.dtype)
        ckv = _rms(p[:, o_ckv:o_ckv + kv_rank], kvn_ref[...])
        ckv_ref[...] = ckv.astype(ckv_ref.dtype)
        ones = jnp.ones((SUM_ROWS, ckv.shape[0]), F32)
        ckvt_ref[...] = jnp.concatenate([ckv.T, ones], axis=0).astype(ckvt_ref.dtype)
        ik = p[:, o_ik:o_ik + idx_dim]
        mu = jnp.mean(ik, axis=-1, keepdims=True)
        var = jnp.mean(jnp.square(ik - mu), axis=-1, keepdims=True)
        ik_ref[...] = ((ik - mu) * lax.rsqrt(var + EPS) * ikg_ref[...] + ikb_ref[...]).astype(ik_ref.dtype)
        m = p[:, o_misc:o_misc + LANE]
        lane = lax.broadcasted_iota(jnp.int32, m.shape, 1)
        beta = jax.nn.sigmoid(m)
        sp_in = m + dtb_ref[...]
        softplus = jnp.maximum(sp_in, 0.0) + jnp.log(1.0 + jnp.exp(-jnp.abs(sp_in)))
        gdec = negA_ref[...] * softplus
        misc = jnp.where(lane < nh, m * w_scale,
                         jnp.where(lane < 2 * nh, beta, jnp.where(lane < 3 * nh, gdec, 0.0)))
        misc_s[...] = misc
        mt = misc.T[0:MISC_ROWS, :]
        mt_s[...] = mt
        misct_ref[...] = mt

    chunks = list(range(0, cols, col_chunk))
    n_sub = tm // ts
    s = s_ref[...] * keep
    ci = 0
    for st in range(n_sub):
        for k0 in range(0, cw_ref.shape[1], col_chunk):
            _dn_conv(dn_in, st * ts, ts, cw_ref, xpad_ref, act_ref, keep if st == 0 else None,
                     slice(k0, k0 + col_chunk))
            if ci < len(chunks):
                proj_chunk(chunks[ci])
                ci += 1
        s = _dn_stages(dn_in, dn_misc, dn_mt, st * ts, ts, act_ref, onorm, s, odn_ref, consts, nh=nh, dk=dk)
    for c0 in chunks[ci:]:
        proj_chunk(c0)
    proj_small()
    s_ref[...] = s


def _in_proj_deltanet(x2d, g, w_big, w_small, q_norm, kv_norm, ik_g, ik_b, neg_a_pad, dtb_pad, conv_w,
                      out_norm, *, seq, q_rank, kv_rank, idx_dim, nh, dk, w_scale, tm=512, ts=256,
                      col_chunk=512):
    n, d = x2d.shape
    cols = w_big.shape[1]
    qk_w = nh * dk
    qkvz_w = 4 * qk_w
    gate_w = cols - qkvz_w
    nt = n // tm
    assert seq % tm == 0 and tm % ts == 0 and qkvz_w % col_chunk == 0
    cur = lambda i: jnp.minimum(i, nt - 1)
    prev = lambda i: jnp.maximum(i - 1, 0)
    row = lambda c: pl.BlockSpec((tm, c), lambda i: (cur(i), 0))
    colblk = lambda r: pl.BlockSpec((r, tm), lambda i: (0, cur(i)))
    return pl.pallas_call(
        functools.partial(_inproj_dn_kernel, q_rank=q_rank, kv_rank=kv_rank, idx_dim=idx_dim, nh=nh,
                          dk=dk, w_scale=w_scale, col_chunk=col_chunk, seq=seq, ts=ts),
        grid=(nt + 1,),
        in_specs=[
            row(d), _resident((1, d)), _resident((d, cols)), _resident((d, w_small.shape[1])),
            _resident((1, q_rank)), _resident((1, kv_rank)), _resident((1, idx_dim)),
            _resident((1, idx_dim)), _resident((1, LANE)), _resident((1, LANE)),
            _resident(conv_w.shape), _resident((1, dk)),
        ],
        out_specs=[row(gate_w), row(q_rank), row(kv_rank), colblk(kv_rank + SUM_ROWS), row(idx_dim),
                   colblk(MISC_ROWS), pl.BlockSpec((tm, qk_w), lambda i: (prev(i), 0))],
        out_shape=[
            jax.ShapeDtypeStruct((n, gate_w), BF16),
            jax.ShapeDtypeStruct((n, q_rank), BF16),
            jax.ShapeDtypeStruct((n, kv_rank), BF16),
            jax.ShapeDtypeStruct((kv_rank + SUM_ROWS, n), BF16),
            jax.ShapeDtypeStruct((n, idx_dim), BF16),
            jax.ShapeDtypeStruct((MISC_ROWS, n), F32),
            jax.ShapeDtypeStruct((n, qk_w), BF16),
        ],
        scratch_shapes=[
            pltpu.VMEM((tm, qkvz_w), BF16), pltpu.VMEM((tm, LANE), F32), pltpu.VMEM((MISC_ROWS, tm), F32),
            pltpu.VMEM((tm, qkvz_w), BF16), pltpu.VMEM((tm, LANE), F32), pltpu.VMEM((MISC_ROWS, tm), F32),
            pltpu.VMEM((ts + 8, 3 * qk_w), F32),
            pltpu.VMEM((ts, 3 * qk_w), F32),
            pltpu.VMEM((nh, dk, dk), F32),
        ],
        compiler_params=_cparams(("arbitrary",)),
        name="in_proj_deltanet",
    )(x2d, g.reshape(1, d), w_big, w_small, q_norm.reshape(1, -1), kv_norm.reshape(1, -1),
      ik_g.reshape(1, -1), ik_b.reshape(1, -1), neg_a_pad, dtb_pad, conv_w, out_norm.reshape(1, dk))


def _sortable(x):
    i = pltpu.bitcast(x, jnp.int32)
    return i ^ ((i >> 31) & jnp.int32(0x7FFFFFFF))


def _bit_transpose32(words):
    a = list(words)
    assert len(a) == 32
    j, m = 16, 0x0000FFFF
    while j:
        for k in range(32):
            if not k & j:
                t = (lax.shift_right_logical(a[k], jnp.int32(j)) ^ a[k + j]) & jnp.int32(m)
                a[k + j] = a[k + j] ^ t
                a[k] = a[k] ^ (t << j)
        j >>= 1
        m = (m ^ (m << j)) & 0xFFFFFFFF if j else m
    return a


def _dsa_kernel(cq_ref, mtc_ref, mtn_ref, ki_ref, ckv_ref, ckvt_ref, wiq_ref, wuq_ref, wuk_ref, wuv_ref,
                bias_ref, o_ref, key_ref, keyn_ref, planes_ref, thr_s, acc_ref, lg0_ref, lg1_ref,
                qidx_s, qlat_s, olat_s, *, nh_idx, idx_dim, nh, dh, topk, scale, tks, tka, grp):
    tq = LANE
    sub = LANE
    j = pl.program_id(1)
    nb = pl.num_programs(1)
    jg = j % grp
    rows = lax.broadcasted_iota(jnp.int32, (sub, tq), 0)
    cols = lax.broadcasted_iota(jnp.int32, (sub, tq), 1)
    rows_s = lax.broadcasted_iota(jnp.int32, (tks, tq), 0)
    nt = planes_ref.shape[1]
    assert nt % 2 == 0 and tks == tka

    def n_tiles(blk):
        return (blk * tq + tq + tks - 1) // tks

    def project_group(first_slot):
        cqg = cq_ref[...]
        qi = _dot_nt(wiq_ref[...], cqg).astype(BF16)
        qg = _dot_nt(wuq_ref[...], cqg).astype(BF16)
        for g in range(grp):
            qidx_s[first_slot + g] = jnp.concatenate(
                [qi[h * idx_dim:(h + 1) * idx_dim, g * tq:(g + 1) * tq] for h in range(nh_idx)], axis=1)
        for h in range(nh):
            ql = (_dot(wuk_ref[h], qg[h * dh:(h + 1) * dh, :]) * (scale * LOG2E)).astype(BF16)
            for g in range(grp):
                qlat_s[first_slot + g, :, h * tq:(h + 1) * tq] = ql[:, g * tq:(g + 1) * tq]

    @pl.when((pl.program_id(0) == 0) & (j == 0))
    def _():
        planes_ref[...] = jnp.zeros_like(planes_ref)
        keyn_ref[...] = jnp.zeros_like(keyn_ref)

    @pl.when(j == 0)
    def _():
        project_group(0)

    @pl.when((j > 0) & ((j + 1) % grp == 0) & (j + 1 < nb))
    def _():
        project_group((((j + 1) // grp) % 2) * grp)

    def head_weights(mt_ref):
        return jnp.concatenate([mt_ref[h:h + 1, :] for h in range(nh_idx)], axis=1)

    def score_tile(kt, blk, kref, wrow_all):
        slot = blk % (2 * grp)
        qpos = blk * tq + cols
        words = []
        t0 = pl.multiple_of(kt * tks, tks)
        d_all = _dot(ki_ref[pl.ds(t0, tks), :], qidx_s[slot])
        for r in range(tks // sub):
            s0 = pl.multiple_of(kt * tks + r * sub, sub)
            d = jnp.maximum(d_all[r * sub:(r + 1) * sub], 0.0) * wrow_all
            parts = [d[:, h * tq:(h + 1) * tq] for h in range(nh_idx)]
            while len(parts) > 1:
                parts = [parts[a] + parts[a + 1] for a in range(0, len(parts), 2)]
            causal = (s0 + rows) <= qpos
            key = jnp.where(causal, _sortable(parts[0]), INT_MIN)
            kref[pl.ds(s0, sub), :] = key
            ukey = key ^ INT_MIN
            words += [ukey[8 * w:8 * (w + 1), :] for w in range(sub // 8)]
        for b, plane in enumerate(_bit_transpose32(words)):
            planes_ref[b, kt] = plane

    def score_pair(t, blk, kref, wrow_all):
        score_tile(2 * t, blk, kref, wrow_all)
        score_tile(2 * t + 1, blk, kref, wrow_all)

    def select(blk, kref):
        n_s = n_tiles(blk)
        tile_ids = lax.broadcasted_iota(jnp.int32, (nt, 8, tq), 0)
        eq0 = jnp.where(tile_ids < n_s, -1, 0).astype(jnp.int32)

        ans_u = cnt_gt = jnp.zeros((1, tq), jnp.int32)
        eq = eq0
        for b in range(31, -1, -1):
            t = eq & planes_ref[b]
            c = jnp.sum(jnp.sum(lax.population_count(t), axis=0), axis=0, keepdims=True)
            accept = (cnt_gt + c) >= topk
            ans_u = jnp.where(accept, ans_u | jnp.int32(INT_MIN if b == 31 else 1 << b), ans_u)
            cnt_gt = jnp.where(accept, cnt_gt, cnt_gt + c)
            eq = jnp.where(accept, t, eq ^ t)
        thr = jnp.maximum(ans_u ^ INT_MIN, INT_MIN + 1)

        n_eq = jnp.sum(jnp.sum(lax.population_count(eq), axis=0), axis=0, keepdims=True)
        any_tie = jnp.max(jnp.where((cnt_gt + n_eq > topk) & (ans_u != 0), 1, 0)) > 0

        def count_tiles(hit_fn):
            def body(kt, acc):
                s0 = pl.multiple_of(kt * tks, tks)
                hit = hit_fn(kref[pl.ds(s0, tks), :], s0)
                return acc + jnp.sum(hit.reshape(tks // 32, 4, 8, tq), axis=0)
            acc = lax.fori_loop(0, n_s, body, jnp.zeros((4, 8, tq), jnp.int32))
            return jnp.sum(acc.reshape(32, tq), axis=0, keepdims=True)

        @pl.when(any_tie)
        def _():
            n_gt = count_tiles(lambda kv, s0: jnp.where(kv > thr, 1, 0).astype(jnp.int32))
            need = topk - n_gt

            def count_eq_below(lim):
                return count_tiles(lambda kv, s0: jnp.where((kv == thr) & ((s0 + rows_s) < lim), 1, 0)
                                   .astype(jnp.int32))

            nbits = int(math.ceil(math.log2(key_ref.shape[0]))) + 1

            def jbit_body(bi, lim):
                cand = lim | jnp.left_shift(jnp.int32(1), nbits - 1 - bi)
                return jnp.where(count_eq_below(cand) <= need, cand, lim)

            lim = lax.fori_loop(0, nbits, jbit_body, jnp.zeros((1, tq), jnp.int32))

            def demote_body(kt, carry):
                s0 = pl.multiple_of(kt * tks, tks)
                kv = kref[pl.ds(s0, tks), :]
                drop = (kv == thr) & ((s0 + rows_s) >= lim)
                kref[pl.ds(s0, tks), :] = jnp.where(drop, INT_MIN, kv)
                return carry

            lax.fori_loop(0, n_s, demote_body, 0)

        thr_s[0:1, :] = thr

    @pl.when(j == 0)
    def _():
        w0 = head_weights(mtc_ref)
        lax.fori_loop(0, (n_tiles(0) + 1) // 2, lambda t, c: (score_pair(t, 0, key_ref, w0), c)[1], 0)
        select(0, key_ref)

    @pl.when(j > 0)
    def _():
        key_ref[...] = keyn_ref[...]

    thr = thr_s[0:1, :]

    acc_ref[...] = jnp.zeros_like(acc_ref)
    nsub = tka // sub
    n_a = n_tiles(j)
    last_tile = n_a - 1
    int_max = jnp.int32(2 ** 31 - 1)
    qslot = j % (2 * grp)

    def logits_to(buf_ref, kt):
        s0 = pl.multiple_of(jnp.minimum(kt, last_tile) * tka, tka)
        buf_ref[...] = _dot(ckv_ref[pl.ds(s0, tka), :], qlat_s[qslot]).astype(buf_ref.dtype)

    def consume(buf_ref, kt, m_run, near):
        ktc = jnp.minimum(kt, last_tile)
        s0 = pl.multiple_of(ktc * tka, tka)
        thr_v = jnp.where(kt <= last_tile, thr, int_max)
        parts = []
        for r in range(nsub):
            kv = key_ref[pl.ds(pl.multiple_of(s0 + r * sub, sub), sub), :]
            madd = jnp.where(kv >= thr_v, 0.0, NEG_BIG).astype(BF16)
            madd = jnp.concatenate([madd] * nh, axis=1)
            if near:
                madd = madd + bias_ref[jnp.clip(j - (ktc * nsub + r), 0, 2)]
            parts.append(buf_ref[r * sub:(r + 1) * sub, :] + madd)
        lg = jnp.concatenate(parts, axis=0)
        tile_max = jnp.max(lg.reshape(tka // 16, 16, nh * tq), axis=0).astype(F32)
        m_new = jnp.maximum(m_run, jnp.max(tile_max, axis=0, keepdims=True))
        p = jnp.exp2(lg - m_new.astype(BF16))
        alpha = jnp.exp2(m_run - m_new)
        ct = ckvt_ref[:, pl.ds(s0, tka)]
        acc_ref[...] = acc_ref[...] * alpha + _dot(ct, p)
        return m_new

    def pair_body(t, m_run, near, with_scores):
        kt = 2 * t
        logits_to(lg1_ref, kt + 1)
        m_run = consume(lg0_ref, kt, m_run, near)
        if with_scores:
            score_tile(kt, j + 1, keyn_ref, wrow_n)
        logits_to(lg0_ref, kt + 2)
        m_run = consume(lg1_ref, kt + 1, m_run, near)
        if with_scores:
            score_tile(kt + 1, j + 1, keyn_ref, wrow_n)
        return m_run

    def quad_body(q, m_run, with_scores):
        m_run = pair_body(2 * q, m_run, False, with_scores)
        return pair_body(2 * q + 1, m_run, False, with_scores)

    m0 = jnp.full((1, nh * tq), NEG_BIG, F32).astype(BF16).astype(F32)
    n_pairs = (n_a + 1) // 2
    far_quads = jnp.maximum((j - 1) // (2 * nsub), 0) // 2
    logits_to(lg0_ref, 0)
    wrow_n = head_weights(mtn_ref)

    @pl.when(j + 1 < nb)
    def _():
        m_run = lax.fori_loop(0, far_quads, functools.partial(quad_body, with_scores=True), m0)
        lax.fori_loop(2 * far_quads, n_pairs, functools.partial(pair_body, near=True, with_scores=True), m_run)
        lax.fori_loop(n_pairs, (n_tiles(j + 1) + 1) // 2,
                      lambda t, c: (score_pair(t, j + 1, keyn_ref, wrow_n), c)[1], 0)

    @pl.when(j + 1 >= nb)
    def _():
        m_run = lax.fori_loop(0, far_quads, functools.partial(quad_body, with_scores=False), m0)
        lax.fori_loop(2 * far_quads, n_pairs, functools.partial(pair_body, near=True, with_scores=False), m_run)

    kvr = acc_ref.shape[0] - SUM_ROWS
    l_fin = acc_ref[kvr:kvr + 1, :]
    olat_s[jg] = (acc_ref[0:kvr, :] * (1.0 / l_fin)).astype(BF16)

    @pl.when(jg == grp - 1)
    def _():
        for h in range(nh):
            ol = jnp.concatenate([olat_s[g, :, h * tq:(h + 1) * tq] for g in range(grp)], axis=1)
            o_ref[:, h * dh:(h + 1) * dh] = _dot(wuv_ref[h], ol).T.astype(o_ref.dtype)

    @pl.when(j + 1 < nb)
    def _():
        select(j + 1, keyn_ref)


def _dsa(cq, misc_t, kidx, ckv, ckv_t, wiq_t, wuq_t, wuk_h, wuv_h, bias_tab, *, batch, topk, scale):
    n, q_rank = cq.shape
    s = n // batch
    kv_rank = ckv.shape[-1]
    idx_dim = kidx.shape[-1]
    nh_idx = wiq_t.shape[0] // idx_dim
    nh, dh, _ = wuv_h.shape
    tq = LANE
    nb = s // tq
    tks = 256
    tka = 256
    grp = 4
    assert s % tks == 0 and s % tka == 0 and nb % grp == 0
    last_blk = batch * nb - 1
    prep = lambda bi, j: jnp.minimum(bi * nb + j + jnp.where(j > 0, 1, 0), last_blk)
    return pl.pallas_call(
        functools.partial(_dsa_kernel, nh_idx=nh_idx, idx_dim=idx_dim, nh=nh, dh=dh, topk=topk,
                          scale=scale, tks=tks, tka=tka, grp=grp),
        grid=(batch, nb),
        in_specs=[
            pl.BlockSpec((grp * tq, q_rank), lambda bi, j: (prep(bi, j) // grp, 0)),
            pl.BlockSpec((MISC_ROWS, tq), lambda bi, j: (0, bi * nb + j)),
            pl.BlockSpec((MISC_ROWS, tq), lambda bi, j: (0, jnp.minimum(bi * nb + j + 1, last_blk))),
            pl.BlockSpec((s, idx_dim), lambda bi, j: (bi, 0)),
            pl.BlockSpec((s, kv_rank), lambda bi, j: (bi, 0)),
            pl.BlockSpec((kv_rank + SUM_ROWS, s), lambda bi, j: (0, bi)),
            _resident(wiq_t.shape), _resident(wuq_t.shape), _resident(wuk_h.shape),
            _resident(wuv_h.shape), _resident(bias_tab.shape),
        ],
        out_specs=pl.BlockSpec((grp * tq, nh * dh), lambda bi, j: ((bi * nb + j) // grp, 0)),
        out_shape=jax.ShapeDtypeStruct((n, nh * dh), BF16),
        scratch_shapes=[
            pltpu.VMEM((s, tq), jnp.int32),
            pltpu.VMEM((s, tq), jnp.int32),
            pltpu.VMEM((32, s // tks, 8, tq), jnp.int32),
            pltpu.VMEM((8, tq), jnp.int32),
            pltpu.VMEM((kv_rank + SUM_ROWS, nh * tq), F32),
            pltpu.VMEM((tka, nh * tq), BF16),
            pltpu.VMEM((tka, nh * tq), BF16),
            pltpu.VMEM((2 * grp, idx_dim, nh_idx * tq), BF16),
            pltpu.VMEM((2 * grp, kv_rank, nh * tq), BF16),
            pltpu.VMEM((grp, kv_rank, nh * tq), BF16),
        ],
        compiler_params=_cparams(("arbitrary", "arbitrary")),
        name="dsa",
    )(cq, misc_t, misc_t, kidx, ckv, ckv_t, wiq_t, wuq_t, wuk_h, wuv_h, bias_tab)


def _t5_bucket_np(n, n_buckets):
    max_exact = n_buckets // 2
    nf = np.maximum(n, 1).astype(np.float32)
    large = max_exact + (np.log(nf / max_exact) / math.log(REL_MAX_DIST / max_exact)
                         * (n_buckets - max_exact)).astype(np.int32)
    large = np.minimum(large, n_buckets - 1)
    return np.where(n < max_exact, n, large)


def _bias_table(rel_bias, tq):
    n_buckets, nh = rel_bias.shape
    assert _t5_bucket_np(np.array([tq + 1]), n_buckets)[0] == n_buckets - 1
    s_loc = np.arange(tq)[:, None]
    t_loc = np.arange(tq)[None, :]
    dist = np.stack([np.maximum(d * tq + t_loc - s_loc, 0) for d in range(3)])
    onehot = np.eye(n_buckets, dtype=np.float32)[_t5_bucket_np(dist, n_buckets)]
    tb = jnp.einsum('dstb,bh->dsht', jnp.asarray(onehot), rel_bias.astype(F32),
                    precision=lax.Precision.HIGHEST)
    far = rel_bias[n_buckets - 1].astype(F32)[None, None, :, None]
    return ((tb - far) * LOG2E).reshape(3, tq, nh * tq).astype(BF16)


def kernel(x, ffn1_norm, ffn1_wg, ffn1_wu, ffn1_wd, mix_norm, w_in, conv_w, a_log, dt_bias, dn_out_norm, q_norm, kv_norm, w_uq, w_uk, w_uv, w_iq, idx_k_g, idx_k_b, rel_bias, w_o, ffn2_norm, ffn2_wg, ffn2_wu, ffn2_wd, final_norm):
    b, s, d = x.shape
    depth = ffn1_norm.shape[0]
    nh_dn = a_log.shape[1]
    dk = dn_out_norm.shape[1]
    qk_w = nh_dn * dk
    q_rank = q_norm.shape[1]
    kv_rank = kv_norm.shape[1]
    idx_dim = idx_k_g.shape[1]
    nh_idx = w_iq.shape[2] // idx_dim
    nh_sa, dh_sa = w_uk.shape[2], w_uk.shape[3]
    assert qk_w == d and nh_sa * dh_sa == d and nh_idx == nh_dn and 3 * nh_dn <= MISC_ROWS
    topk = min(TOPK_MAX, s // 4)
    n = b * s

    sizes = (3 * qk_w, qk_w, nh_dn, nh_dn, q_rank, kv_rank, idx_dim, nh_idx, d, d)
    offs = np.concatenate([[0], np.cumsum(sizes)])
    (o_qkv, o_z, o_b, o_a, o_cq, o_ckv, o_ik, o_iw, o_ga, o_gb) = [int(v) for v in offs[:-1]]
    l_cq, l_ckv, l_ik, l_misc, small_cols = _small_layout(q_rank, kv_rank, idx_dim)
    bias_tab = _bias_table(rel_bias, LANE)

    x2d = x.reshape(n, d)
    for l in range(depth):
        x1 = _ffn(x2d, ffn1_norm[l], ffn1_wg[l], ffn1_wu[l], ffn1_wd[l])

        wl = w_in[l]
        w_big = jnp.concatenate([wl[:, o_qkv:o_qkv + 4 * qk_w], wl[:, o_ga:o_ga + 2 * d]],
                                axis=1).astype(BF16)
        w_small = jnp.zeros((d, small_cols), F32)
        w_small = w_small.at[:, l_cq:l_cq + q_rank].set(wl[:, o_cq:o_cq + q_rank])
        w_small = w_small.at[:, l_ckv:l_ckv + kv_rank].set(wl[:, o_ckv:o_ckv + kv_rank])
        w_small = w_small.at[:, l_ik:l_ik + idx_dim].set(wl[:, o_ik:o_ik + idx_dim])
        w_small = w_small.at[:, l_misc:l_misc + nh_idx].set(wl[:, o_iw:o_iw + nh_idx])
        w_small = w_small.at[:, l_misc + nh_dn:l_misc + 2 * nh_dn].set(wl[:, o_b:o_b + nh_dn])
        w_small = w_small.at[:, l_misc + 2 * nh_dn:l_misc + 3 * nh_dn].set(wl[:, o_a:o_a + nh_dn])
        neg_a_pad = jnp.zeros((1, LANE), F32).at[0, 2 * nh_dn:3 * nh_dn].set(-jnp.exp(a_log[l].astype(F32)))
        dtb_pad = jnp.zeros((1, LANE), F32).at[0, 2 * nh_dn:3 * nh_dn].set(dt_bias[l].astype(F32))
        gates, cq, ckv, ckv_t, kidx, misc_t, o_dn = _in_proj_deltanet(
            x1, mix_norm[l], w_big, w_small.astype(BF16), q_norm[l], kv_norm[l], idx_k_g[l],
            idx_k_b[l], neg_a_pad, dtb_pad, conv_w[l].astype(F32), dn_out_norm[l], seq=s,
            q_rank=q_rank, kv_rank=kv_rank, idx_dim=idx_dim, nh=nh_dn, dk=dk,
            w_scale=(nh_idx ** -0.5) * (idx_dim ** -0.5))

        o_sa = _dsa(cq, misc_t, kidx, ckv, ckv_t,
                    w_iq[l].T.astype(BF16), w_uq[l].T.astype(BF16),
                    jnp.transpose(w_uk[l], (1, 0, 2)).astype(BF16),
                    jnp.transpose(w_uv[l], (1, 2, 0)).astype(BF16),
                    bias_tab, batch=b, topk=topk, scale=dh_sa ** -0.5)

        x2d = _ffn(x1, ffn2_norm[l], ffn2_wg[l], ffn2_wu[l], ffn2_wd[l],
                   final_g=final_norm if l == depth - 1 else None,
                   merge=(gates, 0, o_dn, o_sa, w_o[l]))
    return x2d.reshape(b, s, d)
```

```python
import functools
import math

import numpy as np
import jax
import jax.numpy as jnp
from jax import lax
from jax.experimental import pallas as pl
from jax.experimental.pallas import tpu as pltpu

F32 = jnp.float32
BF16 = jnp.bfloat16
EPS = 1e-6
TOPK_MAX = 256
REL_MAX_DIST = 128
DN_CHUNK = 64
LANE = 128
SUM_ROWS = 8
MISC_ROWS = 32
INT_MIN = -(2 ** 31)
NEG_BIG = -1e30
LOG2E = math.log2(math.e)
VMEM_LIMIT = 56 * 1024 * 1024


def _cparams(sem):
    return pltpu.CompilerParams(dimension_semantics=sem, vmem_limit_bytes=VMEM_LIMIT)


def _resident(shape):
    return pl.BlockSpec(shape, lambda *_: (0,) * len(shape), pipeline_mode=pl.Buffered(1))


def _rms(x, g):
    return x * lax.rsqrt(jnp.mean(x * x, axis=-1, keepdims=True) + EPS) * g


def _silu(x):
    return x * jax.nn.sigmoid(x)


def _dot(a, b):
    return jnp.dot(a, b, preferred_element_type=F32)


def _dot_nt(a, b):
    return lax.dot_general(a, b, (((1,), (1,)), ((), ())), preferred_element_type=F32)


def _bdot(a, b):
    return lax.dot_general(a, b, (((2,), (1,)), ((0,), (0,))), preferred_element_type=F32)


def _bdot_nt(a, b):
    return lax.dot_general(a, b, (((2,), (2,)), ((0,), (0,))), preferred_element_type=F32)


def _bdot_tn(a, b):
    return lax.dot_general(a, b, (((1,), (1,)), ((0,), (0,))), preferred_element_type=F32)


def _ffn_kernel(*refs, final, merge, ff_chunk):
    refs = list(refs)
    x_ref = refs.pop(0)
    if merge:
        ga_ref, gb_ref, oa_ref, ob_ref, wo_ref = refs[:5]
        refs = refs[5:]
    g_ref, wg_ref, wu_ref, wd_ref = refs[:4]
    refs = refs[4:]
    if final:
        fg_ref = refs.pop(0)
    o_ref, act_ref = refs

    x = x_ref[...]
    if merge:
        merged = (jax.nn.sigmoid(ga_ref[...].astype(F32)) * oa_ref[...].astype(F32)
                  + jax.nn.sigmoid(gb_ref[...].astype(F32)) * ob_ref[...].astype(F32))
        x = x + _dot(merged.astype(BF16), wo_ref[...])
    h = _rms(x, g_ref[...]).astype(BF16)
    ff = wg_ref.shape[1]
    for c0 in range(0, ff, ff_chunk):
        c1 = min(c0 + ff_chunk, ff)
        a = _dot(h, wg_ref[:, c0:c1])
        u = _dot(h, wu_ref[:, c0:c1])
        act_ref[:, c0:c1] = (_silu(a) * u).astype(BF16)
    y = x + 0.5 * _dot(act_ref[...], wd_ref[...])
    if final:
        y = _rms(y, fg_ref[...])
    o_ref[...] = y


def _ffn(x2d, g, wg, wu, wd, *, final_g=None, merge=None, tm=512, ff_chunk=256):
    n, d = x2d.shape
    ff = wg.shape[1]
    row = lambda cols, blk=0: pl.BlockSpec((tm, cols), lambda i, blk=blk: (i, blk))
    in_specs = [row(d)]
    args = [x2d]
    if merge is not None:
        big2d, gate_blk, o_dn, o_sa, w_o = merge
        in_specs += [row(d, gate_blk), row(d, gate_blk + 1), row(d), row(d), _resident((d, d))]
        args += [big2d, big2d, o_dn, o_sa, w_o.astype(BF16)]
    in_specs += [_resident((1, d)), _resident((d, ff)), _resident((d, ff)), _resident((ff, d))]
    args += [g.reshape(1, d), wg.astype(BF16), wu.astype(BF16), wd.astype(BF16)]
    if final_g is not None:
        in_specs.append(_resident((1, d)))
        args.append(final_g.reshape(1, d))
    return pl.pallas_call(
        functools.partial(_ffn_kernel, final=final_g is not None, merge=merge is not None,
                          ff_chunk=ff_chunk),
        grid=(n // tm,),
        in_specs=in_specs,
        out_specs=row(d),
        out_shape=jax.ShapeDtypeStruct((n, d), F32),
        scratch_shapes=[pltpu.VMEM((tm, ff), BF16)],
        compiler_params=_cparams(("parallel",)),
        name="ffn_merge" if merge is not None else "ffn",
    )(*args)


def _small_layout(q_rank, kv_rank, idx_dim):
    o_cq = 0
    o_ckv = o_cq + q_rank
    o_ik = o_ckv + kv_rank
    o_misc = o_ik + LANE * ((idx_dim + LANE - 1) // LANE)
    return o_cq, o_ckv, o_ik, o_misc, o_misc + LANE


def _tri_inverse(lmat, masks, eye):
    x = eye - lmat * masks[0]
    for mk in masks[1:]:
        lm = (lmat * mk).astype(BF16)
        t = _bdot(x.astype(BF16), lm)
        x = x - _bdot(t.astype(BF16), x.astype(BF16))
    return x


def _dn_consts(c):
    row = lax.broadcasted_iota(jnp.int32, (c, c), 0)
    col = lax.broadcasted_iota(jnp.int32, (c, c), 1)
    tril = row >= col
    strict = row > col
    eye = jnp.where(row == col, 1.0, 0.0).astype(F32)
    tril_f = jnp.where(tril, 1.0, 0.0).astype(F32)
    masks = []
    m = 1
    while m < c:
        mk = strict & ((row // (2 * m)) == (col // (2 * m))) & ((row // m) != (col // m))
        masks.append(jnp.where(mk, 1.0, 0.0).astype(F32))
        m *= 2
    return tril, strict, eye, tril_f, masks


def _dn_conv(in_ref, rb, ts, cw_ref, xpad_ref, act_ref, keep, cols):
    halo = 8
    conv_k = cw_ref.shape[0]
    if keep is not None:
        xpad_ref[0:halo, cols] = xpad_ref[0:halo, cols] * keep
    xpad_ref[halo:halo + ts, cols] = in_ref[rb:rb + ts, cols].astype(F32)
    y = xpad_ref[halo:halo + ts, cols] * cw_ref[conv_k - 1:conv_k, cols]
    for dlt in range(1, conv_k):
        y = y + xpad_ref[halo - dlt:halo - dlt + ts, cols] * cw_ref[conv_k - 1 - dlt:conv_k - dlt, cols]
    act_ref[:, cols] = _silu(y)
    xpad_ref[0:halo, cols] = xpad_ref[ts:ts + halo, cols]


def _dn_stages(in_ref, bg_ref, mt_ref, rb, ts, act_ref, onorm, s, o_ref, consts, *, nh, dk):
    c = DN_CHUNK
    cpi = ts // c
    qk_w = nh * dk
    tril, strict, eye, tril_f, masks = consts

    def heads(r0, base):
        return [act_ref[r0:r0 + c, base + h * dk:base + (h + 1) * dk] for h in range(nh)]

    qs, ks, vs, betas, gccs, gcrs, glasts = [], [], [], [], [], [], []
    for cc in range(cpi):
        r0 = cc * c
        bg = bg_ref[rb + r0:rb + r0 + c, :]
        grow = mt_ref[2 * nh:3 * nh, rb + r0:rb + r0 + c]
        gc_col = jnp.dot(tril_f, bg, precision=lax.Precision.HIGHEST,
                         preferred_element_type=F32)
        gc_row = lax.dot_general(grow, tril_f, (((1,), (1,)), ((), ())),
                                 precision=lax.Precision.HIGHEST,
                                 preferred_element_type=F32)
        qs += heads(r0, 0)
        ks += heads(r0, qk_w)
        vs += heads(r0, 2 * qk_w)
        betas += [bg[:, nh + h:nh + h + 1] for h in range(nh)]
        gccs += [gc_col[:, 2 * nh + h:2 * nh + h + 1] for h in range(nh)]
        gcrs += [gc_row[h:h + 1, :] for h in range(nh)]
        glasts += [gc_row[h:h + 1, c - 1:c] for h in range(nh)]
    q = jnp.stack(qs)
    k = jnp.stack(ks)
    v = jnp.stack(vs)
    beta = jnp.stack(betas)
    gcc = jnp.stack(gccs)
    gcr = jnp.stack(gcrs)
    g_last = jnp.stack(glasts)
    q = q * (lax.rsqrt(jnp.sum(q * q, axis=-1, keepdims=True) + EPS) * (dk ** -0.5))
    k = k * lax.rsqrt(jnp.sum(k * k, axis=-1, keepdims=True) + EPS)
    decay = jnp.where(tril, jnp.exp(jnp.minimum(gcc - gcr, 0.0)), 0.0)
    eg = jnp.exp(gcc)
    kb = k * beta
    vb = v * beta
    aq = _bdot_nt(jnp.concatenate([kb, q], axis=1).astype(BF16), k.astype(BF16))
    lmat = jnp.where(strict, aq[:, :c] * decay, 0.0)
    intra = (aq[:, c:] * decay).astype(BF16)
    tinv = _tri_inverse(lmat, masks, eye)
    uw = _bdot(tinv.astype(BF16), jnp.concatenate([vb, kb * eg], axis=2).astype(BF16))
    u = uw[:, :, :dk]
    wq = jnp.concatenate([uw[:, :, dk:], q * eg], axis=1).astype(BF16)
    kd = (k * jnp.exp(g_last - gcc)).astype(BF16)
    sdec = jnp.exp(g_last)
    for cc in range(cpi):
        sl = slice(cc * nh, (cc + 1) * nh)
        m1 = _bdot(wq[sl], s.astype(BF16))
        vnb = (u[sl] - m1[:, :c]).astype(BF16)
        o = _rms(m1[:, c:] + _bdot(intra[sl], vnb), onorm)
        s = s * sdec[sl] + _bdot_tn(kd[sl], vnb)
        for h in range(nh):
            rows = slice(rb + cc * c, rb + (cc + 1) * c)
            zz = in_ref[rows, 3 * qk_w + h * dk:3 * qk_w + (h + 1) * dk].astype(F32)
            o_ref[rows, h * dk:(h + 1) * dk] = (o[h] * _silu(zz)).astype(o_ref.dtype)
    return s


def _inproj_dn_kernel(x_ref, g_ref, wb_ref, ws_ref, qn_ref, kvn_ref, ikg_ref, ikb_ref, negA_ref, dtb_ref,
                      cw_ref, on_ref,
                      gates_ref, cq_ref, ckv_ref, ckvt_ref, ik_ref, misct_ref, odn_ref,
                      qkvz_s, misc_s, mt_s, dn_in, dn_misc, dn_mt, xpad_ref, act_ref, s_ref,
                      *, q_rank, kv_rank, idx_dim, nh, dk, w_scale, col_chunk, seq, ts):
    o_cq, o_ckv, o_ik, o_misc, _ = _small_layout(q_rank, kv_rank, idx_dim)
    i = pl.program_id(0)
    tm = x_ref.shape[0]
    qkvz_w = qkvz_s.shape[1]

    @pl.when(i == 0)
    def _():
        qkvz_s[...] = jnp.zeros_like(qkvz_s)
        misc_s[...] = jnp.zeros_like(misc_s)
        mt_s[...] = jnp.zeros_like(mt_s)
        xpad_ref[0:8, :] = jnp.zeros((8, xpad_ref.shape[1]), F32)
        s_ref[...] = jnp.zeros_like(s_ref)

    dn_in[...] = qkvz_s[...]
    dn_misc[...] = misc_s[...]
    dn_mt[...] = mt_s[...]
    keep = jnp.where(((i - 1) * tm) % seq == 0, 0.0, 1.0)
    consts = _dn_consts(DN_CHUNK)
    onorm = on_ref[...]

    h = _rms(x_ref[...], g_ref[...]).astype(BF16)
    cols = wb_ref.shape[1]

    def proj_chunk(c0):
        c1 = min(c0 + col_chunk, cols)
        p = _dot(h, wb_ref[:, c0:c1]).astype(BF16)
        if c1 <= qkvz_w:
            qkvz_s[:, c0:c1] = p
        else:
            assert c0 >= qkvz_w
            gates_ref[:, c0 - qkvz_w:c1 - qkvz_w] = p

    def proj_small():
        p = _dot(h, ws_ref[...])
        cq_ref[...] = _rms(p[:, o_cq:o_cq + q_rank], qn_ref[...]).astype(cq_ref.dtype)
        ckv = _rms(p[:, o_ckv:o_ckv + kv_rank], kvn_ref[...])
        ckv_ref[...] = ckv.astype(ckv_ref.dtype)
        ones = jnp.ones((SUM_ROWS, ckv.shape[0]), F32)
        ckvt_ref[...] = jnp.concatenate([ckv.T, ones], axis=0).astype(ckvt_ref.dtype)
        ik = p[:, o_ik:o_ik + idx_dim]
        mu = jnp.mean(ik, axis=-1, keepdims=True)
        var = jnp.mean(jnp.square(ik - mu), axis=-1, keepdims=True)
        ik_ref[...] = ((ik - mu) * lax.rsqrt(var + EPS) * ikg_ref[...] + ikb_ref[...]).astype(ik_ref.dtype)
        m = p[:, o_misc:o_misc + LANE]
        lane = lax.broadcasted_iota(jnp.int32, m.shape, 1)
        beta = jax.nn.sigmoid(m)
        sp_in = m + dtb_ref[...]
        softplus = jnp.maximum(sp_in, 0.0) + jnp.log(1.0 + jnp.exp(-jnp.abs(sp_in)))
        gdec = negA_ref[...] * softplus
        misc = jnp.where(lane < nh, m * w_scale,
                         jnp.where(lane < 2 * nh, beta, jnp.where(lane < 3 * nh, gdec, 0.0)))
        misc_s[...] = misc
        mt = misc.T[0:MISC_ROWS, :]
        mt_s[...] = mt
        misct_ref[...] = mt

    chunks = list(range(0, cols, col_chunk))
    n_sub = tm // ts
    s = s_ref[...] * keep
    ci = 0
    for st in range(n_sub):
        for k0 in range(0, cw_ref.shape[1], col_chunk):
            _dn_conv(dn_in, st * ts, ts, cw_ref, xpad_ref, act_ref, keep if st == 0 else None,
                     slice(k0, k0 + col_chunk))
            if ci < len(chunks):
                proj_chunk(chunks[ci])
                ci += 1
        s = _dn_stages(dn_in, dn_misc, dn_mt, st * ts, ts, act_ref, onorm, s, odn_ref, consts, nh=nh, dk=dk)
    for c0 in chunks[ci:]:
        proj_chunk(c0)
    proj_small()
    s_ref[...] = s


def _in_proj_deltanet(x2d, g, w_big, w_small, q_norm, kv_norm, ik_g, ik_b, neg_a_pad, dtb_pad, conv_w,
                      out_norm, *, seq, q_rank, kv_rank, idx_dim, nh, dk, w_scale, tm=512, ts=256,
                      col_chunk=512):
    n, d = x2d.shape
    cols = w_big.shape[1]
    qk_w = nh * dk
    qkvz_w = 4 * qk_w
    gate_w = cols - qkvz_w
    nt = n // tm
    assert seq % tm == 0 and tm % ts == 0 and qkvz_w % col_chunk == 0
    cur = lambda i: jnp.minimum(i, nt - 1)
    prev = lambda i: jnp.maximum(i - 1, 0)
    row = lambda c: pl.BlockSpec((tm, c), lambda i: (cur(i), 0))
    colblk = lambda r: pl.BlockSpec((r, tm), lambda i: (0, cur(i)))
    return pl.pallas_call(
        functools.partial(_inproj_dn_kernel, q_rank=q_rank, kv_rank=kv_rank, idx_dim=idx_dim, nh=nh,
                          dk=dk, w_scale=w_scale, col_chunk=col_chunk, seq=seq, ts=ts),
        grid=(nt + 1,),
        in_specs=[
            row(d), _resident((1, d)), _resident((d, cols)), _resident((d, w_small.shape[1])),
            _resident((1, q_rank)), _resident((1, kv_rank)), _resident((1, idx_dim)),
            _resident((1, idx_dim)), _resident((1, LANE)), _resident((1, LANE)),
            _resident(conv_w.shape), _resident((1, dk)),
        ],
        out_specs=[row(gate_w), row(q_rank), row(kv_rank), colblk(kv_rank + SUM_ROWS), row(idx_dim),
                   colblk(MISC_ROWS), pl.BlockSpec((tm, qk_w), lambda i: (prev(i), 0))],
        out_shape=[
            jax.ShapeDtypeStruct((n, gate_w), BF16),
            jax.ShapeDtypeStruct((n, q_rank), BF16),
            jax.ShapeDtypeStruct((n, kv_rank), BF16),
            jax.ShapeDtypeStruct((kv_rank + SUM_ROWS, n), BF16),
            jax.ShapeDtypeStruct((n, idx_dim), BF16),
            jax.ShapeDtypeStruct((MISC_ROWS, n), F32),
            jax.ShapeDtypeStruct((n, qk_w), BF16),
        ],
        scratch_shapes=[
            pltpu.VMEM((tm, qkvz_w), BF16), pltpu.VMEM((tm, LANE), F32), pltpu.VMEM((MISC_ROWS, tm), F32),
            pltpu.VMEM((tm, qkvz_w), BF16), pltpu.VMEM((tm, LANE), F32), pltpu.VMEM((MISC_ROWS, tm), F32),
            pltpu.VMEM((ts + 8, 3 * qk_w), F32),
            pltpu.VMEM((ts, 3 * qk_w), F32),
            pltpu.VMEM((nh, dk, dk), F32),
        ],
        compiler_params=_cparams(("arbitrary",)),
        name="in_proj_deltanet",
    )(x2d, g.reshape(1, d), w_big, w_small, q_norm.reshape(1, -1), kv_norm.reshape(1, -1),
      ik_g.reshape(1, -1), ik_b.reshape(1, -1), neg_a_pad, dtb_pad, conv_w, out_norm.reshape(1, dk))


def _sortable(x):
    i = pltpu.bitcast(x, jnp.int32)
    return i ^ ((i >> 31) & jnp.int32(0x7FFFFFFF))


def _bit_transpose32(words):
    a = list(words)
    assert len(a) == 32
    j, m = 16, 0x0000FFFF
    while j:
        for k in range(32):
            if not k & j:
                t = (lax.shift_right_logical(a[k], jnp.int32(j)) ^ a[k + j]) & jnp.int32(m)
                a[k + j] = a[k + j] ^ t
                a[k] = a[k] ^ (t << j)
        j >>= 1
        m = (m ^ (m << j)) & 0xFFFFFFFF if j else m
    return a


def _dsa_kernel(cq_ref, mtc_ref, mtn_ref, ki_ref, ckv_ref, ckvt_ref, wiq_ref, wuq_ref, wuk_ref, wuv_ref,
                bias_ref, o_ref, key_ref, keyn_ref, planes_ref, thr_s, acc_ref, lg0_ref, lg1_ref, lg2_ref, lg3_ref,
                qidx_s, qlat_s, olat_s, *, nh_idx, idx_dim, nh, dh, topk, scale, tks, tka, grp):
    tq = LANE
    sub = LANE
    j = pl.program_id(1)
    nb = pl.num_programs(1)
    jg = j % grp
    rows = lax.broadcasted_iota(jnp.int32, (sub, tq), 0)
    cols = lax.broadcasted_iota(jnp.int32, (sub, tq), 1)
    rows_s = lax.broadcasted_iota(jnp.int32, (tks, tq), 0)
    nt = planes_ref.shape[1]
    assert nt % 2 == 0 and tks == tka

    def n_tiles(blk):
        return (blk * tq + tq + tks - 1) // tks

    def project_group(first_slot):
        cqg = cq_ref[...]
        qi = _dot_nt(wiq_ref[...], cqg).astype(BF16)
        qg = _dot_nt(wuq_ref[...], cqg).astype(BF16)
        for g in range(grp):
            qidx_s[first_slot + g] = jnp.concatenate(
                [qi[h * idx_dim:(h + 1) * idx_dim, g * tq:(g + 1) * tq] for h in range(nh_idx)], axis=1)
        for h in range(nh):
            ql = (_dot(wuk_ref[h], qg[h * dh:(h + 1) * dh, :]) * (scale * LOG2E)).astype(BF16)
            for g in range(grp):
                qlat_s[first_slot + g, :, h * tq:(h + 1) * tq] = ql[:, g * tq:(g + 1) * tq]

    @pl.when((pl.program_id(0) == 0) & (j == 0))
    def _():
        planes_ref[...] = jnp.zeros_like(planes_ref)
        keyn_ref[...] = jnp.zeros_like(keyn_ref)

    @pl.when(j == 0)
    def _():
        project_group(0)

    @pl.when((j > 0) & ((j + 1) % grp == 0) & (j + 1 < nb))
    def _():
        project_group((((j + 1) // grp) % 2) * grp)

    def head_weights(mt_ref):
        return jnp.concatenate([mt_ref[h:h + 1, :] for h in range(nh_idx)], axis=1)

    def score_tile(kt, blk, kref, wrow_all):
        slot = blk % (2 * grp)
        qpos = blk * tq + cols
        words = []
        t0 = pl.multiple_of(kt * tks, tks)
        d_all = _dot(ki_ref[pl.ds(t0, tks), :], qidx_s[slot])
        for r in range(tks // sub):
            s0 = pl.multiple_of(kt * tks + r * sub, sub)
            d = jnp.maximum(d_all[r * sub:(r + 1) * sub], 0.0) * wrow_all
            parts = [d[:, h * tq:(h + 1) * tq] for h in range(nh_idx)]
            while len(parts) > 1:
                parts = [parts[a] + parts[a + 1] for a in range(0, len(parts), 2)]
            causal = (s0 + rows) <= qpos
            key = jnp.where(causal, _sortable(parts[0]), INT_MIN)
            kref[pl.ds(s0, sub), :] = key
            ukey = key ^ INT_MIN
            words += [ukey[8 * w:8 * (w + 1), :] for w in range(sub // 8)]
        for b, plane in enumerate(_bit_transpose32(words)):
            planes_ref[b, kt] = plane

    def score_pair(t, blk, kref, wrow_all):
        score_tile(2 * t, blk, kref, wrow_all)
        score_tile(2 * t + 1, blk, kref, wrow_all)

    def select(blk, kref):
        n_s = n_tiles(blk)
        tile_ids = lax.broadcasted_iota(jnp.int32, (nt, 8, tq), 0)
        eq0 = jnp.where(tile_ids < n_s, -1, 0).astype(jnp.int32)

        ans_u = cnt_gt = jnp.zeros((1, tq), jnp.int32)
        eq = eq0
        for b in range(31, -1, -1):
            t = eq & planes_ref[b]
            c = jnp.sum(jnp.sum(lax.population_count(t), axis=0), axis=0, keepdims=True)
            accept = (cnt_gt + c) >= topk
            ans_u = jnp.where(accept, ans_u | jnp.int32(INT_MIN if b == 31 else 1 << b), ans_u)
            cnt_gt = jnp.where(accept, cnt_gt, cnt_gt + c)
            eq = jnp.where(accept, t, eq ^ t)
        thr = jnp.maximum(ans_u ^ INT_MIN, INT_MIN + 1)

        n_eq = jnp.sum(jnp.sum(lax.population_count(eq), axis=0), axis=0, keepdims=True)
        any_tie = jnp.max(jnp.where((cnt_gt + n_eq > topk) & (ans_u != 0), 1, 0)) > 0

        def count_tiles(hit_fn):
            def body(kt, acc):
                s0 = pl.multiple_of(kt * tks, tks)
                hit = hit_fn(kref[pl.ds(s0, tks), :], s0)
                return acc + jnp.sum(hit.reshape(tks // 32, 4, 8, tq), axis=0)
            acc = lax.fori_loop(0, n_s, body, jnp.zeros((4, 8, tq), jnp.int32))
            return jnp.sum(acc.reshape(32, tq), axis=0, keepdims=True)

        @pl.when(any_tie)
        def _():
            n_gt = count_tiles(lambda kv, s0: jnp.where(kv > thr, 1, 0).astype(jnp.int32))
            need = topk - n_gt

            def count_eq_below(lim):
                return count_tiles(lambda kv, s0: jnp.where((kv == thr) & ((s0 + rows_s) < lim), 1, 0)
                                   .astype(jnp.int32))

            nbits = int(math.ceil(math.log2(key_ref.shape[0]))) + 1

            def jbit_body(bi, lim):
                cand = lim | jnp.left_shift(jnp.int32(1), nbits - 1 - bi)
                return jnp.where(count_eq_below(cand) <= need, cand, lim)

            lim = lax.fori_loop(0, nbits, jbit_body, jnp.zeros((1, tq), jnp.int32))

            def demote_body(kt, carry):
                s0 = pl.multiple_of(kt * tks, tks)
                kv = kref[pl.ds(s0, tks), :]
                drop = (kv == thr) & ((s0 + rows_s) >= lim)
                kref[pl.ds(s0, tks), :] = jnp.where(drop, INT_MIN, kv)
                return carry

            lax.fori_loop(0, n_s, demote_body, 0)

        thr_s[0:1, :] = thr

    @pl.when(j == 0)
    def _():
        w0 = head_weights(mtc_ref)
        lax.fori_loop(0, (n_tiles(0) + 1) // 2, lambda t, c: (score_pair(t, 0, key_ref, w0), c)[1], 0)
        select(0, key_ref)

    @pl.when(j > 0)
    def _():
        key_ref[...] = keyn_ref[...]

    thr = thr_s[0:1, :]

    acc_ref[...] = jnp.zeros_like(acc_ref)
    nsub = tka // sub
    n_a = n_tiles(j)
    last_tile = n_a - 1
    int_max = jnp.int32(2 ** 31 - 1)
    qslot = j % (2 * grp)

    def logits_to(buf_ref, kt):
        s0 = pl.multiple_of(jnp.minimum(kt, last_tile) * tka, tka)
        buf_ref[...] = _dot(ckv_ref[pl.ds(s0, tka), :], qlat_s[qslot]).astype(buf_ref.dtype)

    def consume(buf_ref, kt, m_run, near):
        ktc = jnp.minimum(kt, last_tile)
        s0 = pl.multiple_of(ktc * tka, tka)
        thr_v = jnp.where(kt <= last_tile, thr, int_max)
        parts = []
        for r in range(nsub):
            kv = key_ref[pl.ds(pl.multiple_of(s0 + r * sub, sub), sub), :]
            madd = jnp.where(kv >= thr_v, 0.0, NEG_BIG).astype(BF16)
            madd = jnp.concatenate([madd] * nh, axis=1)
            if near:
                madd = madd + bias_ref[jnp.clip(j - (ktc * nsub + r), 0, 2)]
            parts.append(buf_ref[r * sub:(r + 1) * sub, :] + madd)
        lg = jnp.concatenate(parts, axis=0)
        tile_max = jnp.max(lg.reshape(tka // 16, 16, nh * tq), axis=0).astype(F32)
        m_new = jnp.maximum(m_run, jnp.max(tile_max, axis=0, keepdims=True))
        p = jnp.exp2(lg - m_new.astype(BF16))
        alpha = jnp.exp2(m_run - m_new)
        ct = ckvt_ref[:, pl.ds(s0, tka)]
        acc_ref[...] = acc_ref[...] * alpha + _dot(ct, p)
        return m_new

    def pair_body(t, m_run, near, with_scores):
        kt = 2 * t
        logits_to(lg1_ref, kt + 1)
        m_run = consume(lg0_ref, kt, m_run, near)
        if with_scores:
            score_tile(kt, j + 1, keyn_ref, wrow_n)
        logits_to(lg0_ref, kt + 2)
        m_run = consume(lg1_ref, kt + 1, m_run, near)
        if with_scores:
            score_tile(kt + 1, j + 1, keyn_ref, wrow_n)
        return m_run

    def logits2_to(buf_a, buf_b, kt):
        s0 = pl.multiple_of(jnp.minimum(kt, nt - 2) * tka, tka)
        lg = _dot(ckv_ref[pl.ds(s0, 2 * tka), :], qlat_s[qslot]).astype(buf_a.dtype)
        buf_a[...] = lg[0:tka]
        buf_b[...] = lg[tka:2 * tka]

    def quad_body(q, m_run, with_scores):
        kt = 4 * q
        logits2_to(lg2_ref, lg3_ref, kt + 2)
        m_run = consume(lg0_ref, kt, m_run, False)
        if with_scores:
            score_tile(kt, j + 1, keyn_ref, wrow_n)
        m_run = consume(lg1_ref, kt + 1, m_run, False)
        if with_scores:
            score_tile(kt + 1, j + 1, keyn_ref, wrow_n)
        logits2_to(lg0_ref, lg1_ref, kt + 4)
        m_run = consume(lg2_ref, kt + 2, m_run, False)
        if with_scores:
            score_tile(kt + 2, j + 1, keyn_ref, wrow_n)
        m_run = consume(lg3_ref, kt + 3, m_run, False)
        if with_scores:
            score_tile(kt + 3, j + 1, keyn_ref, wrow_n)
        return m_run

    m0 = jnp.full((1, nh * tq), NEG_BIG, F32).astype(BF16).astype(F32)
    n_pairs = (n_a + 1) // 2
    far_quads = jnp.maximum((j - 1) // (2 * nsub), 0) // 2
    logits2_to(lg0_ref, lg1_ref, 0)
    wrow_n = head_weights(mtn_ref)

    @pl.when(j + 1 < nb)
    def _():
        m_run = lax.fori_loop(0, far_quads, functools.partial(quad_body, with_scores=True), m0)
        lax.fori_loop(2 * far_quads, n_pairs, functools.partial(pair_body, near=True, with_scores=True), m_run)
        lax.fori_loop(n_pairs, (n_tiles(j + 1) + 1) // 2,
                      lambda t, c: (score_pair(t, j + 1, keyn_ref, wrow_n), c)[1], 0)

    @pl.when(j + 1 >= nb)
    def _():
        m_run = lax.fori_loop(0, far_quads, functools.partial(quad_body, with_scores=False), m0)
        lax.fori_loop(2 * far_quads, n_pairs, functools.partial(pair_body, near=True, with_scores=False), m_run)

    kvr = acc_ref.shape[0] - SUM_ROWS
    l_fin = acc_ref[kvr:kvr + 1, :]
    olat_s[jg] = (acc_ref[0:kvr, :] * (1.0 / l_fin)).astype(BF16)

    @pl.when(jg == grp - 1)
    def _():
        for h in range(nh):
            ol = jnp.concatenate([olat_s[g, :, h * tq:(h + 1) * tq] for g in range(grp)], axis=1)
            o_ref[:, h * dh:(h + 1) * dh] = _dot(wuv_ref[h], ol).T.astype(o_ref.dtype)

    @pl.when(j + 1 < nb)
    def _():
        select(j + 1, keyn_ref)


def _dsa(cq, misc_t, kidx, ckv, ckv_t, wiq_t, wuq_t, wuk_h, wuv_h, bias_tab, *, batch, topk, scale):
    n, q_rank = cq.shape
    s = n // batch
    kv_rank = ckv.shape[-1]
    idx_dim = kidx.shape[-1]
    nh_idx = wiq_t.shape[0] // idx_dim
    nh, dh, _ = wuv_h.shape
    tq = LANE
    nb = s // tq
    tks = 256
    tka = 256
    grp = 4
    assert s % tks == 0 and s % tka == 0 and nb % grp == 0
    last_blk = batch * nb - 1
    prep = lambda bi, j: jnp.minimum(bi * nb + j + jnp.where(j > 0, 1, 0), last_blk)
    return pl.pallas_call(
        functools.partial(_dsa_kernel, nh_idx=nh_idx, idx_dim=idx_dim, nh=nh, dh=dh, topk=topk,
                          scale=scale, tks=tks, tka=tka, grp=grp),
        grid=(batch, nb),
        in_specs=[
            pl.BlockSpec((grp * tq, q_rank), lambda bi, j: (prep(bi, j) // grp, 0)),
            pl.BlockSpec((MISC_ROWS, tq), lambda bi, j: (0, bi * nb + j)),
            pl.BlockSpec((MISC_ROWS, tq), lambda bi, j: (0, jnp.minimum(bi * nb + j + 1, last_blk))),
            pl.BlockSpec((s, idx_dim), lambda bi, j: (bi, 0)),
            pl.BlockSpec((s, kv_rank), lambda bi, j: (bi, 0)),
            pl.BlockSpec((kv_rank + SUM_ROWS, s), lambda bi, j: (0, bi)),
            _resident(wiq_t.shape), _resident(wuq_t.shape), _resident(wuk_h.shape),
            _resident(wuv_h.shape), _resident(bias_tab.shape),
        ],
        out_specs=pl.BlockSpec((grp * tq, nh * dh), lambda bi, j: ((bi * nb + j) // grp, 0)),
        out_shape=jax.ShapeDtypeStruct((n, nh * dh), BF16),
        scratch_shapes=[
            pltpu.VMEM((s, tq), jnp.int32),
            pltpu.VMEM((s, tq), jnp.int32),
            pltpu.VMEM((32, s // tks, 8, tq), jnp.int32),
            pltpu.VMEM((8, tq), jnp.int32),
            pltpu.VMEM((kv_rank + SUM_ROWS, nh * tq), F32),
            pltpu.VMEM((tka, nh * tq), BF16),
            pltpu.VMEM((tka, nh * tq), BF16),
            pltpu.VMEM((tka, nh * tq), BF16),
            pltpu.VMEM((tka, nh * tq), BF16),
            pltpu.VMEM((2 * grp, idx_dim, nh_idx * tq), BF16),
            pltpu.VMEM((2 * grp, kv_rank, nh * tq), BF16),
            pltpu.VMEM((grp, kv_rank, nh * tq), BF16),
        ],
        compiler_params=_cparams(("arbitrary", "arbitrary")),
        name="dsa",
    )(cq, misc_t, misc_t, kidx, ckv, ckv_t, wiq_t, wuq_t, wuk_h, wuv_h, bias_tab)


def _t5_bucket_np(n, n_buckets):
    max_exact = n_buckets // 2
    nf = np.maximum(n, 1).astype(np.float32)
    large = max_exact + (np.log(nf / max_exact) / math.log(REL_MAX_DIST / max_exact)
                         * (n_buckets - max_exact)).astype(np.int32)
    large = np.minimum(large, n_buckets - 1)
    return np.where(n < max_exact, n, large)


def _bias_table(rel_bias, tq):
    n_buckets, nh = rel_bias.shape
    assert _t5_bucket_np(np.array([tq + 1]), n_buckets)[0] == n_buckets - 1
    s_loc = np.arange(tq)[:, None]
    t_loc = np.arange(tq)[None, :]
    dist = np.stack([np.maximum(d * tq + t_loc - s_loc, 0) for d in range(3)])
    onehot = np.eye(n_buckets, dtype=np.float32)[_t5_bucket_np(dist, n_buckets)]
    tb = jnp.einsum('dstb,bh->dsht', jnp.asarray(onehot), rel_bias.astype(F32),
                    precision=lax.Precision.HIGHEST)
    far = rel_bias[n_buckets - 1].astype(F32)[None, None, :, None]
    return ((tb - far) * LOG2E).reshape(3, tq, nh * tq).astype(BF16)


def kernel(x, ffn1_norm, ffn1_wg, ffn1_wu, ffn1_wd, mix_norm, w_in, conv_w, a_log, dt_bias, dn_out_norm, q_norm, kv_norm, w_uq, w_uk, w_uv, w_iq, idx_k_g, idx_k_b, rel_bias, w_o, ffn2_norm, ffn2_wg, ffn2_wu, ffn2_wd, final_norm):
    b, s, d = x.shape
    depth = ffn1_norm.shape[0]
    nh_dn = a_log.shape[1]
    dk = dn_out_norm.shape[1]
    qk_w = nh_dn * dk
    q_rank = q_norm.shape[1]
    kv_rank = kv_norm.shape[1]
    idx_dim = idx_k_g.shape[1]
    nh_idx = w_iq.shape[2] // idx_dim
    nh_sa, dh_sa = w_uk.shape[2], w_uk.shape[3]
    assert qk_w == d and nh_sa * dh_sa == d and nh_idx == nh_dn and 3 * nh_dn <= MISC_ROWS
    topk = min(TOPK_MAX, s // 4)
    n = b * s

    sizes = (3 * qk_w, qk_w, nh_dn, nh_dn, q_rank, kv_rank, idx_dim, nh_idx, d, d)
    offs = np.concatenate([[0], np.cumsum(sizes)])
    (o_qkv, o_z, o_b, o_a, o_cq, o_ckv, o_ik, o_iw, o_ga, o_gb) = [int(v) for v in offs[:-1]]
    l_cq, l_ckv, l_ik, l_misc, small_cols = _small_layout(q_rank, kv_rank, idx_dim)
    bias_tab = _bias_table(rel_bias, LANE)

    x2d = x.reshape(n, d)
    for l in range(depth):
        x1 = _ffn(x2d, ffn1_norm[l], ffn1_wg[l], ffn1_wu[l], ffn1_wd[l])

        wl = w_in[l]
        w_big = jnp.concatenate([wl[:, o_qkv:o_qkv + 4 * qk_w], wl[:, o_ga:o_ga + 2 * d]],
                                axis=1).astype(BF16)
        w_small = jnp.zeros((d, small_cols), F32)
        w_small = w_small.at[:, l_cq:l_cq + q_rank].set(wl[:, o_cq:o_cq + q_rank])
        w_small = w_small.at[:, l_ckv:l_ckv + kv_rank].set(wl[:, o_ckv:o_ckv + kv_rank])
        w_small = w_small.at[:, l_ik:l_ik + idx_dim].set(wl[:, o_ik:o_ik + idx_dim])
        w_small = w_small.at[:, l_misc:l_misc + nh_idx].set(wl[:, o_iw:o_iw + nh_idx])
        w_small = w_small.at[:, l_misc + nh_dn:l_misc + 2 * nh_dn].set(wl[:, o_b:o_b + nh_dn])
        w_small = w_small.at[:, l_misc + 2 * nh_dn:l_misc + 3 * nh_dn].set(wl[:, o_a:o_a + nh_dn])
        neg_a_pad = jnp.zeros((1, LANE), F32).at[0, 2 * nh_dn:3 * nh_dn].set(-jnp.exp(a_log[l].astype(F32)))
        dtb_pad = jnp.zeros((1, LANE), F32).at[0, 2 * nh_dn:3 * nh_dn].set(dt_bias[l].astype(F32))
        gates, cq, ckv, ckv_t, kidx, misc_t, o_dn = _in_proj_deltanet(
            x1, mix_norm[l], w_big, w_small.astype(BF16), q_norm[l], kv_norm[l], idx_k_g[l],
            idx_k_b[l], neg_a_pad, dtb_pad, conv_w[l].astype(F32), dn_out_norm[l], seq=s,
            q_rank=q_rank, kv_rank=kv_rank, idx_dim=idx_dim, nh=nh_dn, dk=dk,
            w_scale=(nh_idx ** -0.5) * (idx_dim ** -0.5))

        o_sa = _dsa(cq, misc_t, kidx, ckv, ckv_t,
                    w_iq[l].T.astype(BF16), w_uq[l].T.astype(BF16),
                    jnp.transpose(w_uk[l], (1, 0, 2)).astype(BF16),
                    jnp.transpose(w_uv[l], (1, 2, 0)).astype(BF16),
                    bias_tab, batch=b, topk=topk, scale=dh_sa ** -0.5)

        x2d = _ffn(x1, ffn2_norm[l], ffn2_wg[l], ffn2_wu[l], ffn2_wd[l],
                   final_g=final_norm if l == depth - 1 else None,
                   merge=(gates, 0, o_dn, o_sa, w_o[l]))
    return x2d.reshape(b, s, d)
```

```python
import functools
import math

import numpy as np
import jax
import jax.numpy as jnp
from jax import lax
from jax.experimental import pallas as pl
from jax.experimental.pallas import tpu as pltpu

F32 = jnp.float32
BF16 = jnp.bfloat16
EPS = 1e-6
TOPK_MAX = 256
REL_MAX_DIST = 128
DN_CHUNK = 64
LANE = 128
SUM_ROWS = 8
MISC_ROWS = 32
INT_MIN = -(2 ** 31)
NEG_BIG = -1e30
LOG2E = math.log2(math.e)
VMEM_LIMIT = 56 * 1024 * 1024


def _cparams(sem):
    return pltpu.CompilerParams(dimension_semantics=sem, vmem_limit_bytes=VMEM_LIMIT)


def _resident(shape):
    return pl.BlockSpec(shape, lambda *_: (0,) * len(shape), pipeline_mode=pl.Buffered(1))


def _rms(x, g):
    return x * lax.rsqrt(jnp.mean(x * x, axis=-1, keepdims=True) + EPS) * g


def _silu(x):
    return x * jax.nn.sigmoid(x)


def _dot(a, b):
    return jnp.dot(a, b, preferred_element_type=F32)


def _dot_nt(a, b):
    return lax.dot_general(a, b, (((1,), (1,)), ((), ())), preferred_element_type=F32)


def _bdot(a, b):
    return lax.dot_general(a, b, (((2,), (1,)), ((0,), (0,))), preferred_element_type=F32)


def _bdot_nt(a, b):
    return lax.dot_general(a, b, (((2,), (2,)), ((0,), (0,))), preferred_element_type=F32)


def _bdot_tn(a, b):
    return lax.dot_general(a, b, (((1,), (1,)), ((0,), (0,))), preferred_element_type=F32)


def _ffn_kernel(*refs, final, merge, ff_chunk):
    refs = list(refs)
    x_ref = refs.pop(0)
    if merge:
        ga_ref, gb_ref, oa_ref, ob_ref, wo_ref = refs[:5]
        refs = refs[5:]
    g_ref, wg_ref, wu_ref, wd_ref = refs[:4]
    refs = refs[4:]
    if final:
        fg_ref = refs.pop(0)
    o_ref, act_ref = refs

    x = x_ref[...]
    if merge:
        merged = (jax.nn.sigmoid(ga_ref[...].astype(F32)) * oa_ref[...].astype(F32)
                  + jax.nn.sigmoid(gb_ref[...].astype(F32)) * ob_ref[...].astype(F32))
        x = x + _dot(merged.astype(BF16), wo_ref[...])
    h = _rms(x, g_ref[...]).astype(BF16)
    ff = wg_ref.shape[1]
    for c0 in range(0, ff, ff_chunk):
        c1 = min(c0 + ff_chunk, ff)
        a = _dot(h, wg_ref[:, c0:c1])
        u = _dot(h, wu_ref[:, c0:c1])
        act_ref[:, c0:c1] = (_silu(a) * u).astype(BF16)
    y = x + 0.5 * _dot(act_ref[...], wd_ref[...])
    if final:
        y = _rms(y, fg_ref[...])
    o_ref[...] = y


def _ffn(x2d, g, wg, wu, wd, *, final_g=None, merge=None, tm=512, ff_chunk=256):
    n, d = x2d.shape
    ff = wg.shape[1]
    row = lambda cols, blk=0: pl.BlockSpec((tm, cols), lambda i, blk=blk: (i, blk))
    in_specs = [row(d)]
    args = [x2d]
    if merge is not None:
        big2d, gate_blk, o_dn, o_sa, w_o = merge
        in_specs += [row(d, gate_blk), row(d, gate_blk + 1), row(d), row(d), _resident((d, d))]
        args += [big2d, big2d, o_dn, o_sa, w_o.astype(BF16)]
    in_specs += [_resident((1, d)), _resident((d, ff)), _resident((d, ff)), _resident((ff, d))]
    args += [g.reshape(1, d), wg.astype(BF16), wu.astype(BF16), wd.astype(BF16)]
    if final_g is not None:
        in_specs.append(_resident((1, d)))
        args.append(final_g.reshape(1, d))
    return pl.pallas_call(
        functools.partial(_ffn_kernel, final=final_g is not None, merge=merge is not None,
                          ff_chunk=ff_chunk),
        grid=(n // tm,),
        in_specs=in_specs,
        out_specs=row(d),
        out_shape=jax.ShapeDtypeStruct((n, d), F32),
        scratch_shapes=[pltpu.VMEM((tm, ff), BF16)],
        compiler_params=_cparams(("parallel",)),
        name="ffn_merge" if merge is not None else "ffn",
    )(*args)


def _small_layout(q_rank, kv_rank, idx_dim):
    o_cq = 0
    o_ckv = o_cq + q_rank
    o_ik = o_ckv + kv_rank
    o_misc = o_ik + LANE * ((idx_dim + LANE - 1) // LANE)
    return o_cq, o_ckv, o_ik, o_misc, o_misc + LANE


def _tri_inverse(lmat, masks, eye):
    x = eye - lmat * masks[0]
    for mk in masks[1:]:
        lm = (lmat * mk).astype(BF16)
        t = _bdot(x.astype(BF16), lm)
        x = x - _bdot(t.astype(BF16), x.astype(BF16))
    return x


def _dn_consts(c):
    row = lax.broadcasted_iota(jnp.int32, (c, c), 0)
    col = lax.broadcasted_iota(jnp.int32, (c, c), 1)
    tril = row >= col
    strict = row > col
    eye = jnp.where(row == col, 1.0, 0.0).astype(F32)
    tril_f = jnp.where(tril, 1.0, 0.0).astype(F32)
    masks = []
    m = 1
    while m < c:
        mk = strict & ((row // (2 * m)) == (col // (2 * m))) & ((row // m) != (col // m))
        masks.append(jnp.where(mk, 1.0, 0.0).astype(F32))
        m *= 2
    return tril, strict, eye, tril_f, masks


def _dn_conv(in_ref, rb, ts, cw_ref, xpad_ref, act_ref, keep, cols):
    halo = 8
    conv_k = cw_ref.shape[0]
    if keep is not None:
        xpad_ref[0:halo, cols] = xpad_ref[0:halo, cols] * keep
    xpad_ref[halo:halo + ts, cols] = in_ref[rb:rb + ts, cols].astype(F32)
    y = xpad_ref[halo:halo + ts, cols] * cw_ref[conv_k - 1:conv_k, cols]
    for dlt in range(1, conv_k):
        y = y + xpad_ref[halo - dlt:halo - dlt + ts, cols] * cw_ref[conv_k - 1 - dlt:conv_k - dlt, cols]
    act_ref[:, cols] = _silu(y)
    xpad_ref[0:halo, cols] = xpad_ref[ts:ts + halo, cols]


def _dn_stages(in_ref, bg_ref, mt_ref, rb, ts, act_ref, onorm, s, o_ref, consts, *, nh, dk):
    c = DN_CHUNK
    cpi = ts // c
    qk_w = nh * dk
    tril, strict, eye, tril_f, masks = consts

    def heads(r0, base):
        return [act_ref[r0:r0 + c, base + h * dk:base + (h + 1) * dk] for h in range(nh)]

    qs, ks, vs, betas, gccs, gcrs, glasts = [], [], [], [], [], [], []
    for cc in range(cpi):
        r0 = cc * c
        bg = bg_ref[rb + r0:rb + r0 + c, :]
        grow = mt_ref[2 * nh:3 * nh, rb + r0:rb + r0 + c]
        gc_col = jnp.dot(tril_f, bg, precision=lax.Precision.HIGHEST,
                         preferred_element_type=F32)
        gc_row = lax.dot_general(grow, tril_f, (((1,), (1,)), ((), ())),
                                 precision=lax.Precision.HIGHEST,
                                 preferred_element_type=F32)
        qs += heads(r0, 0)
        ks += heads(r0, qk_w)
        vs += heads(r0, 2 * qk_w)
        betas += [bg[:, nh + h:nh + h + 1] for h in range(nh)]
        gccs += [gc_col[:, 2 * nh + h:2 * nh + h + 1] for h in range(nh)]
        gcrs += [gc_row[h:h + 1, :] for h in range(nh)]
        glasts += [gc_row[h:h + 1, c - 1:c] for h in range(nh)]
    q = jnp.stack(qs)
    k = jnp.stack(ks)
    v = jnp.stack(vs)
    beta = jnp.stack(betas)
    gcc = jnp.stack(gccs)
    gcr = jnp.stack(gcrs)
    g_last = jnp.stack(glasts)
    q = q * (lax.rsqrt(jnp.sum(q * q, axis=-1, keepdims=True) + EPS) * (dk ** -0.5))
    k = k * lax.rsqrt(jnp.sum(k * k, axis=-1, keepdims=True) + EPS)
    decay = jnp.where(tril, jnp.exp(jnp.minimum(gcc - gcr, 0.0)), 0.0)
    eg = jnp.exp(gcc)
    kb = k * beta
    vb = v * beta
    aq = _bdot_nt(jnp.concatenate([kb, q], axis=1).astype(BF16), k.astype(BF16))
    lmat = jnp.where(strict, aq[:, :c] * decay, 0.0)
    intra = (aq[:, c:] * decay).astype(BF16)
    tinv = _tri_inverse(lmat, masks, eye)
    uw = _bdot(tinv.astype(BF16), jnp.concatenate([vb, kb * eg], axis=2).astype(BF16))
    u = uw[:, :, :dk]
    wq = jnp.concatenate([uw[:, :, dk:], q * eg], axis=1).astype(BF16)
    kd = (k * jnp.exp(g_last - gcc)).astype(BF16)
    sdec = jnp.exp(g_last)
    for cc in range(cpi):
        sl = slice(cc * nh, (cc + 1) * nh)
        m1 = _bdot(wq[sl], s.astype(BF16))
        vnb = (u[sl] - m1[:, :c]).astype(BF16)
        o = _rms(m1[:, c:] + _bdot(intra[sl], vnb), onorm)
        s = s * sdec[sl] + _bdot_tn(kd[sl], vnb)
        for h in range(nh):
            rows = slice(rb + cc * c, rb + (cc + 1) * c)
            zz = in_ref[rows, 3 * qk_w + h * dk:3 * qk_w + (h + 1) * dk].astype(F32)
            o_ref[rows, h * dk:(h + 1) * dk] = (o[h] * _silu(zz)).astype(o_ref.dtype)
    return s


def _inproj_dn_kernel(x_ref, g_ref, wb_ref, ws_ref, qn_ref, kvn_ref, ikg_ref, ikb_ref, negA_ref, dtb_ref,
                      cw_ref, on_ref,
                      gates_ref, cq_ref, ckv_ref, ckvt_ref, ik_ref, misct_ref, odn_ref,
                      qkvz_s, misc_s, mt_s, dn_in, dn_misc, dn_mt, xpad_ref, act_ref, s_ref,
                      *, q_rank, kv_rank, idx_dim, nh, dk, w_scale, col_chunk, seq, ts):
    o_cq, o_ckv, o_ik, o_misc, _ = _small_layout(q_rank, kv_rank, idx_dim)
    i = pl.program_id(0)
    tm = x_ref.shape[0]
    qkvz_w = qkvz_s.shape[1]

    @pl.when(i == 0)
    def _():
        qkvz_s[...] = jnp.zeros_like(qkvz_s)
        misc_s[...] = jnp.zeros_like(misc_s)
        mt_s[...] = jnp.zeros_like(mt_s)
        xpad_ref[0:8, :] = jnp.zeros((8, xpad_ref.shape[1]), F32)
        s_ref[...] = jnp.zeros_like(s_ref)

    dn_in[...] = qkvz_s[...]
    dn_misc[...] = misc_s[...]
    dn_mt[...] = mt_s[...]
    keep = jnp.where(((i - 1) * tm) % seq == 0, 0.0, 1.0)
    consts = _dn_consts(DN_CHUNK)
    onorm = on_ref[...]

    h = _rms(x_ref[...], g_ref[...]).astype(BF16)
    cols = wb_ref.shape[1]

    def proj_chunk(c0):
        c1 = min(c0 + col_chunk, cols)
        p = _dot(h, wb_ref[:, c0:c1]).astype(BF16)
        if c1 <= qkvz_w:
            qkvz_s[:, c0:c1] = p
        else:
            assert c0 >= qkvz_w
            gates_ref[:, c0 - qkvz_w:c1 - qkvz_w] = p

    def proj_small():
        p = _dot(h, ws_ref[...])
        cq_ref[...] = _rms(p[:, o_cq:o_cq + q_rank], qn_ref[...]).astype(cq_ref.dtype)
        ckv = _rms(p[:, o_ckv:o_ckv + kv_rank], kvn_ref[...])
        ckv_ref[...] = ckv.astype(ckv_ref.dtype)
        ones = jnp.ones((SUM_ROWS, ckv.shape[0]), F32)
        ckvt_ref[...] = jnp.concatenate([ckv.T, ones], axis=0).astype(ckvt_ref.dtype)
        ik = p[:, o_ik:o_ik + idx_dim]
        mu = jnp.mean(ik, axis=-1, keepdims=True)
        var = jnp.mean(jnp.square(ik - mu), axis=-1, keepdims=True)
        ik_ref[...] = ((ik - mu) * lax.rsqrt(var + EPS) * ikg_ref[...] + ikb_ref[...]).astype(ik_ref.dtype)
        m = p[:, o_misc:o_misc + LANE]
        lane = lax.broadcasted_iota(jnp.int32, m.shape, 1)
        beta = jax.nn.sigmoid(m)
        sp_in = m + dtb_ref[...]
        softplus = jnp.maximum(sp_in, 0.0) + jnp.log(1.0 + jnp.exp(-jnp.abs(sp_in)))
        gdec = negA_ref[...] * softplus
        misc = jnp.where(lane < nh, m * w_scale,
                         jnp.where(lane < 2 * nh, beta, jnp.where(lane < 3 * nh, gdec, 0.0)))
        misc_s[...] = misc
        mt = misc.T[0:MISC_ROWS, :]
        mt_s[...] = mt
        misct_ref[...] = mt

    chunks = list(range(0, cols, col_chunk))
    n_sub = tm // ts
    s = s_ref[...] * keep
    ci = 0
    for st in range(n_sub):
        for k0 in range(0, cw_ref.shape[1], col_chunk):
            _dn_conv(dn_in, st * ts, ts, cw_ref, xpad_ref, act_ref, keep if st == 0 else None,
                     slice(k0, k0 + col_chunk))
            if ci < len(chunks):
                proj_chunk(chunks[ci])
                ci += 1
        s = _dn_stages(dn_in, dn_misc, dn_mt, st * ts, ts, act_ref, onorm, s, odn_ref, consts, nh=nh, dk=dk)
    for c0 in chunks[ci:]:
        proj_chunk(c0)
    proj_small()
    s_ref[...] = s


def _in_proj_deltanet(x2d, g, w_big, w_small, q_norm, kv_norm, ik_g, ik_b, neg_a_pad, dtb_pad, conv_w,
                      out_norm, *, seq, q_rank, kv_rank, idx_dim, nh, dk, w_scale, tm=512, ts=256,
                      col_chunk=512):
    n, d = x2d.shape
    cols = w_big.shape[1]
    qk_w = nh * dk
    qkvz_w = 4 * qk_w
    gate_w = cols - qkvz_w
    nt = n // tm
    assert seq % tm == 0 and tm % ts == 0 and qkvz_w % col_chunk == 0
    cur = lambda i: jnp.minimum(i, nt - 1)
    prev = lambda i: jnp.maximum(i - 1, 0)
    row = lambda c: pl.BlockSpec((tm, c), lambda i: (cur(i), 0))
    colblk = lambda r: pl.BlockSpec((r, tm), lambda i: (0, cur(i)))
    return pl.pallas_call(
        functools.partial(_inproj_dn_kernel, q_rank=q_rank, kv_rank=kv_rank, idx_dim=idx_dim, nh=nh,
                          dk=dk, w_scale=w_scale, col_chunk=col_chunk, seq=seq, ts=ts),
        grid=(nt + 1,),
        in_specs=[
            row(d), _resident((1, d)), _resident((d, cols)), _resident((d, w_small.shape[1])),
            _resident((1, q_rank)), _resident((1, kv_rank)), _resident((1, idx_dim)),
            _resident((1, idx_dim)), _resident((1, LANE)), _resident((1, LANE)),
            _resident(conv_w.shape), _resident((1, dk)),
        ],
        out_specs=[row(gate_w), row(q_rank), row(kv_rank), colblk(kv_rank + SUM_ROWS), row(idx_dim),
                   colblk(MISC_ROWS), pl.BlockSpec((tm, qk_w), lambda i: (prev(i), 0))],
        out_shape=[
            jax.ShapeDtypeStruct((n, gate_w), BF16),
            jax.ShapeDtypeStruct((n, q_rank), BF16),
            jax.ShapeDtypeStruct((n, kv_rank), BF16),
            jax.ShapeDtypeStruct((kv_rank + SUM_ROWS, n), BF16),
            jax.ShapeDtypeStruct((n, idx_dim), BF16),
            jax.ShapeDtypeStruct((MISC_ROWS, n), F32),
            jax.ShapeDtypeStruct((n, qk_w), BF16),
        ],
        scratch_shapes=[
            pltpu.VMEM((tm, qkvz_w), BF16), pltpu.VMEM((tm, LANE), F32), pltpu.VMEM((MISC_ROWS, tm), F32),
            pltpu.VMEM((tm, qkvz_w), BF16), pltpu.VMEM((tm, LANE), F32), pltpu.VMEM((MISC_ROWS, tm), F32),
            pltpu.VMEM((ts + 8, 3 * qk_w), F32),
            pltpu.VMEM((ts, 3 * qk_w), F32),
            pltpu.VMEM((nh, dk, dk), F32),
        ],
        compiler_params=_cparams(("arbitrary",)),
        name="in_proj_deltanet",
    )(x2d, g.reshape(1, d), w_big, w_small, q_norm.reshape(1, -1), kv_norm.reshape(1, -1),
      ik_g.reshape(1, -1), ik_b.reshape(1, -1), neg_a_pad, dtb_pad, conv_w, out_norm.reshape(1, dk))


def _sortable(x):
    i = pltpu.bitcast(x, jnp.int32)
    return i ^ ((i >> 31) & jnp.int32(0x7FFFFFFF))


def _bit_transpose32(words):
    a = list(words)
    assert len(a) == 32
    j, m = 16, 0x0000FFFF
    while j:
        for k in range(32):
            if not k & j:
                t = (lax.shift_right_logical(a[k], jnp.int32(j)) ^ a[k + j]) & jnp.int32(m)
                a[k + j] = a[k + j] ^ t
                a[k] = a[k] ^ (t << j)
        j >>= 1
        m = (m ^ (m << j)) & 0xFFFFFFFF if j else m
    return a


def _dsa_kernel(cq_ref, mtc_ref, mtn_ref, ki_ref, ckv_ref, ckvt_ref, wiq_ref, wuq_ref, wuk_ref, wuv_ref,
                bias_ref, o_ref, key_ref, keyn_ref, planes_ref, thr_s, acc_ref, lg0_ref, lg1_ref,
                qidx_s, qlat_s, olat_s, *, nh_idx, idx_dim, nh, dh, topk, scale, tks, tka, grp):
    tq = LANE
    sub = LANE
    j = pl.program_id(1)
    nb = pl.num_programs(1)
    jg = j % grp
    rows = lax.broadcasted_iota(jnp.int32, (sub, tq), 0)
    cols = lax.broadcasted_iota(jnp.int32, (sub, tq), 1)
    rows_s = lax.broadcasted_iota(jnp.int32, (tks, tq), 0)
    nt = planes_ref.shape[1]
    assert nt % 2 == 0 and tks == tka

    def n_tiles(blk):
        return (blk * tq + tq + tks - 1) // tks

    def project_group(first_slot):
        cqg = cq_ref[...]
        qi = _dot_nt(wiq_ref[...], cqg).astype(BF16)
        qg = _dot_nt(wuq_ref[...], cqg).astype(BF16)
        for g in range(grp):
            qidx_s[first_slot + g] = jnp.concatenate(
                [qi[h * idx_dim:(h + 1) * idx_dim, g * tq:(g + 1) * tq] for h in range(nh_idx)], axis=1)
        for h in range(nh):
            ql = (_dot(wuk_ref[h], qg[h * dh:(h + 1) * dh, :]) * (scale * LOG2E)).astype(BF16)
            for g in range(grp):
                qlat_s[first_slot + g, :, h * tq:(h + 1) * tq] = ql[:, g * tq:(g + 1) * tq]

    @pl.when((pl.program_id(0) == 0) & (j == 0))
    def _():
        planes_ref[...] = jnp.zeros_like(planes_ref)
        keyn_ref[...] = jnp.zeros_like(keyn_ref)

    @pl.when(j == 0)
    def _():
        project_group(0)

    @pl.when((j > 0) & ((j + 1) % grp == 0) & (j + 1 < nb))
    def _():
        project_group((((j + 1) // grp) % 2) * grp)

    def head_weights(mt_ref):
        return jnp.concatenate([mt_ref[h:h + 1, :] for h in range(nh_idx)], axis=1)

    def score_tile(kt, blk, kref, wrow_all):
        slot = blk % (2 * grp)
        qpos = blk * tq + cols
        words = []
        t0 = pl.multiple_of(kt * tks, tks)
        d_all = _dot(ki_ref[pl.ds(t0, tks), :], qidx_s[slot])
        for r in range(tks // sub):
            s0 = pl.multiple_of(kt * tks + r * sub, sub)
            d = jnp.maximum(d_all[r * sub:(r + 1) * sub], 0.0) * wrow_all
            parts = [d[:, h * tq:(h + 1) * tq] for h in range(nh_idx)]
            while len(parts) > 1:
                parts = [parts[a] + parts[a + 1] for a in range(0, len(parts), 2)]
            causal = (s0 + rows) <= qpos
            key = jnp.where(causal, _sortable(parts[0]), INT_MIN)
            kref[pl.ds(s0, sub), :] = key
            ukey = key ^ INT_MIN
            words += [ukey[8 * w:8 * (w + 1), :] for w in range(sub // 8)]
        for b, plane in enumerate(_bit_transpose32(words)):
            planes_ref[b, kt] = plane

    def score_pair(t, blk, kref, wrow_all):
        score_tile(2 * t, blk, kref, wrow_all)
        score_tile(2 * t + 1, blk, kref, wrow_all)

    def select(blk, kref):
        n_s = n_tiles(blk)
        tile_ids = lax.broadcasted_iota(jnp.int32, (nt, 8, tq), 0)
        eq0 = jnp.where(tile_ids < n_s, -1, 0).astype(jnp.int32)

        ans_u = cnt_gt = jnp.zeros((1, tq), jnp.int32)
        eq = eq0
        for b in range(31, -1, -1):
            t = eq & planes_ref[b]
            c = jnp.sum(jnp.sum(lax.population_count(t), axis=0), axis=0, keepdims=True)
            accept = (cnt_gt + c) >= topk
            ans_u = jnp.where(accept, ans_u | jnp.int32(INT_MIN if b == 31 else 1 << b), ans_u)
            cnt_gt = jnp.where(accept, cnt_gt, cnt_gt + c)
            eq = jnp.where(accept, t, eq ^ t)
        thr = jnp.maximum(ans_u ^ INT_MIN, INT_MIN + 1)

        n_eq = jnp.sum(jnp.sum(lax.population_count(eq), axis=0), axis=0, keepdims=True)
        any_tie = jnp.max(jnp.where((cnt_gt + n_eq > topk) & (ans_u != 0), 1, 0)) > 0

        def count_tiles(hit_fn):
            def body(kt, acc):
                s0 = pl.multiple_of(kt * tks, tks)
                hit = hit_fn(kref[pl.ds(s0, tks), :], s0)
                return acc + jnp.sum(hit.reshape(tks // 32, 4, 8, tq), axis=0)
            acc = lax.fori_loop(0, n_s, body, jnp.zeros((4, 8, tq), jnp.int32))
            return jnp.sum(acc.reshape(32, tq), axis=0, keepdims=True)

        @pl.when(any_tie)
        def _():
            n_gt = count_tiles(lambda kv, s0: jnp.where(kv > thr, 1, 0).astype(jnp.int32))
            need = topk - n_gt

            def count_eq_below(lim):
                return count_tiles(lambda kv, s0: jnp.where((kv == thr) & ((s0 + rows_s) < lim), 1, 0)
                                   .astype(jnp.int32))

            nbits = int(math.ceil(math.log2(key_ref.shape[0]))) + 1

            def jbit_body(bi, lim):
                cand = lim | jnp.left_shift(jnp.int32(1), nbits - 1 - bi)
                return jnp.where(count_eq_below(cand) <= need, cand, lim)

            lim = lax.fori_loop(0, nbits, jbit_body, jnp.zeros((1, tq), jnp.int32))

            def demote_body(kt, carry):
                s0 = pl.multiple_of(kt * tks, tks)
                kv = kref[pl.ds(s0, tks), :]
                drop = (kv == thr) & ((s0 + rows_s) >= lim)
                kref[pl.ds(s0, tks), :] = jnp.where(drop, INT_MIN, kv)
                return carry

            lax.fori_loop(0, n_s, demote_body, 0)

        thr_s[0:1, :] = thr

    @pl.when(j == 0)
    def _():
        w0 = head_weights(mtc_ref)
        lax.fori_loop(0, (n_tiles(0) + 1) // 2, lambda t, c: (score_pair(t, 0, key_ref, w0), c)[1], 0)
        select(0, key_ref)

    @pl.when(j > 0)
    def _():
        key_ref[...] = keyn_ref[...]

    thr = thr_s[0:1, :]

    acc_ref[...] = jnp.zeros_like(acc_ref)
    nsub = tka // sub
    n_a = n_tiles(j)
    last_tile = n_a - 1
    int_max = jnp.int32(2 ** 31 - 1)
    qslot = j % (2 * grp)

    def logits_to(buf_ref, kt):
        s0 = pl.multiple_of(jnp.minimum(kt, last_tile) * tka, tka)
        buf_ref[...] = _dot(ckv_ref[pl.ds(s0, tka), :], qlat_s[qslot]).astype(buf_ref.dtype)

    def consume(buf_ref, kt, m_run, near):
        ktc = jnp.minimum(kt, last_tile)
        s0 = pl.multiple_of(ktc * tka, tka)
        thr_v = jnp.where(kt <= last_tile, thr, int_max)
        parts = []
        for r in range(nsub):
            kv = key_ref[pl.ds(pl.multiple_of(s0 + r * sub, sub), sub), :]
            madd = jnp.where(kv >= thr_v, 0.0, NEG_BIG).astype(BF16)
            madd = jnp.concatenate([madd] * nh, axis=1)
            if near:
                madd = madd + bias_ref[jnp.clip(j - (ktc * nsub + r), 0, 2)]
            parts.append(buf_ref[r * sub:(r + 1) * sub, :] + madd)
        lg = jnp.concatenate(parts, axis=0)
        tile_max = jnp.max(lg.reshape(tka // 16, 16, nh * tq), axis=0).astype(F32)
        m_new = jnp.maximum(m_run, jnp.max(tile_max, axis=0, keepdims=True))
        p = jnp.exp2(lg - m_new.astype(BF16))
        alpha = jnp.exp2(m_run - m_new)
        ct = ckvt_ref[:, pl.ds(s0, tka)]
        acc_ref[...] = acc_ref[...] * alpha + _dot(ct, p)
        return m_new

    def pair_body(t, m_run, near, with_scores):
        kt = 2 * t
        logits_to(lg1_ref, kt + 1)
        m_run = consume(lg0_ref, kt, m_run, near)
        if with_scores:
            score_tile(kt, j + 1, keyn_ref, wrow_n)
        logits_to(lg0_ref, kt + 2)
        m_run = consume(lg1_ref, kt + 1, m_run, near)
        if with_scores:
            score_tile(kt + 1, j + 1, keyn_ref, wrow_n)
        return m_run

    def quad_body(q, m_run, near, with_scores, base=0):
        m_run = pair_body(base + 2 * q, m_run, near, with_scores)
        return pair_body(base + 2 * q + 1, m_run, near, with_scores)

    m0 = jnp.full((1, nh * tq), NEG_BIG, F32).astype(BF16).astype(F32)
    n_pairs = (n_a + 1) // 2
    far_quads = jnp.maximum((j - 1) // (2 * nsub), 0) // 2
    rest0 = 2 * far_quads
    rest_quads = (n_pairs - rest0) // 2
    logits_to(lg0_ref, 0)
    wrow_n = head_weights(mtn_ref)

    def attention(with_scores):
        m_run = lax.fori_loop(0, far_quads,
                              functools.partial(quad_body, near=False, with_scores=with_scores), m0)
        m_run = lax.fori_loop(0, rest_quads,
                              functools.partial(quad_body, near=True, with_scores=with_scores, base=rest0),
                              m_run)
        lax.fori_loop(rest0 + 2 * rest_quads, n_pairs,
                      functools.partial(pair_body, near=True, with_scores=with_scores), m_run)

    @pl.when(j + 1 < nb)
    def _():
        attention(True)
        lax.fori_loop(n_pairs, (n_tiles(j + 1) + 1) // 2,
                      lambda t, c: (score_pair(t, j + 1, keyn_ref, wrow_n), c)[1], 0)

    @pl.when(j + 1 >= nb)
    def _():
        attention(False)

    kvr = acc_ref.shape[0] - SUM_ROWS
    l_fin = acc_ref[kvr:kvr + 1, :]
    olat_s[jg] = (acc_ref[0:kvr, :] * (1.0 / l_fin)).astype(BF16)

    @pl.when(jg == grp - 1)
    def _():
        for h in range(nh):
            ol = jnp.concatenate([olat_s[g, :, h * tq:(h + 1) * tq] for g in range(grp)], axis=1)
            o_ref[:, h * dh:(h + 1) * dh] = _dot(wuv_ref[h], ol).T.astype(o_ref.dtype)

    @pl.when(j + 1 < nb)
    def _():
        select(j + 1, keyn_ref)


def _dsa(cq, misc_t, kidx, ckv, ckv_t, wiq_t, wuq_t, wuk_h, wuv_h, bias_tab, *, batch, topk, scale):
    n, q_rank = cq.shape
    s = n // batch
    kv_rank = ckv.shape[-1]
    idx_dim = kidx.shape[-1]
    nh_idx = wiq_t.shape[0] // idx_dim
    nh, dh, _ = wuv_h.shape
    tq = LANE
    nb = s // tq
    tks = 256
    tka = 256
    grp = 4
    assert s % tks == 0 and s % tka == 0 and nb % grp == 0
    last_blk = batch * nb - 1
    prep = lambda bi, j: jnp.minimum(bi * nb + j + jnp.where(j > 0, 1, 0), last_blk)
    return pl.pallas_call(
        functools.partial(_dsa_kernel, nh_idx=nh_idx, idx_dim=idx_dim, nh=nh, dh=dh, topk=topk,
                          scale=scale, tks=tks, tka=tka, grp=grp),
        grid=(batch, nb),
        in_specs=[
            pl.BlockSpec((grp * tq, q_rank), lambda bi, j: (prep(bi, j) // grp, 0)),
            pl.BlockSpec((MISC_ROWS, tq), lambda bi, j: (0, bi * nb + j)),
            pl.BlockSpec((MISC_ROWS, tq), lambda bi, j: (0, jnp.minimum(bi * nb + j + 1, last_blk))),
            pl.BlockSpec((s, idx_dim), lambda bi, j: (bi, 0)),
            pl.BlockSpec((s, kv_rank), lambda bi, j: (bi, 0)),
            pl.BlockSpec((kv_rank + SUM_ROWS, s), lambda bi, j: (0, bi)),
            _resident(wiq_t.shape), _resident(wuq_t.shape), _resident(wuk_h.shape),
            _resident(wuv_h.shape), _resident(bias_tab.shape),
        ],
        out_specs=pl.BlockSpec((grp * tq, nh * dh), lambda bi, j: ((bi * nb + j) // grp, 0)),
        out_shape=jax.ShapeDtypeStruct((n, nh * dh), BF16),
        scratch_shapes=[
            pltpu.VMEM((s, tq), jnp.int32),
            pltpu.VMEM((s, tq), jnp.int32),
            pltpu.VMEM((32, s // tks, 8, tq), jnp.int32),
            pltpu.VMEM((8, tq), jnp.int32),
            pltpu.VMEM((kv_rank + SUM_ROWS, nh * tq), F32),
            pltpu.VMEM((tka, nh * tq), BF16),
            pltpu.VMEM((tka, nh * tq), BF16),
            pltpu.VMEM((2 * grp, idx_dim, nh_idx * tq), BF16),
            pltpu.VMEM((2 * grp, kv_rank, nh * tq), BF16),
            pltpu.VMEM((grp, kv_rank, nh * tq), BF16),
        ],
        compiler_params=_cparams(("arbitrary", "arbitrary")),
        name="dsa",
    )(cq, misc_t, misc_t, kidx, ckv, ckv_t, wiq_t, wuq_t, wuk_h, wuv_h, bias_tab)


def _t5_bucket_np(n, n_buckets):
    max_exact = n_buckets // 2
    nf = np.maximum(n, 1).astype(np.float32)
    large = max_exact + (np.log(nf / max_exact) / math.log(REL_MAX_DIST / max_exact)
                         * (n_buckets - max_exact)).astype(np.int32)
    large = np.minimum(large, n_buckets - 1)
    return np.where(n < max_exact, n, large)


def _bias_table(rel_bias, tq):
    n_buckets, nh = rel_bias.shape
    assert _t5_bucket_np(np.array([tq + 1]), n_buckets)[0] == n_buckets - 1
    s_loc = np.arange(tq)[:, None]
    t_loc = np.arange(tq)[None, :]
    dist = np.stack([np.maximum(d * tq + t_loc - s_loc, 0) for d in range(3)])
    onehot = np.eye(n_buckets, dtype=np.float32)[_t5_bucket_np(dist, n_buckets)]
    tb = jnp.einsum('dstb,bh->dsht', jnp.asarray(onehot), rel_bias.astype(F32),
                    precision=lax.Precision.HIGHEST)
    far = rel_bias[n_buckets - 1].astype(F32)[None, None, :, None]
    return ((tb - far) * LOG2E).reshape(3, tq, nh * tq).astype(BF16)


def kernel(x, ffn1_norm, ffn1_wg, ffn1_wu, ffn1_wd, mix_norm, w_in, conv_w, a_log, dt_bias, dn_out_norm, q_norm, kv_norm, w_uq, w_uk, w_uv, w_iq, idx_k_g, idx_k_b, rel_bias, w_o, ffn2_norm, ffn2_wg, ffn2_wu, ffn2_wd, final_norm):
    b, s, d = x.shape
    depth = ffn1_norm.shape[0]
    nh_dn = a_log.shape[1]
    dk = dn_out_norm.shape[1]
    qk_w = nh_dn * dk
    q_rank = q_norm.shape[1]
    kv_rank = kv_norm.shape[1]
    idx_dim = idx_k_g.shape[1]
    nh_idx = w_iq.shape[2] // idx_dim
    nh_sa, dh_sa = w_uk.shape[2], w_uk.shape[3]
    assert qk_w == d and nh_sa * dh_sa == d and nh_idx == nh_dn and 3 * nh_dn <= MISC_ROWS
    topk = min(TOPK_MAX, s // 4)
    n = b * s

    sizes = (3 * qk_w, qk_w, nh_dn, nh_dn, q_rank, kv_rank, idx_dim, nh_idx, d, d)
    offs = np.concatenate([[0], np.cumsum(sizes)])
    (o_qkv, o_z, o_b, o_a, o_cq, o_ckv, o_ik, o_iw, o_ga, o_gb) = [int(v) for v in offs[:-1]]
    l_cq, l_ckv, l_ik, l_misc, small_cols = _small_layout(q_rank, kv_rank, idx_dim)
    bias_tab = _bias_table(rel_bias, LANE)

    x2d = x.reshape(n, d)
    for l in range(depth):
        x1 = _ffn(x2d, ffn1_norm[l], ffn1_wg[l], ffn1_wu[l], ffn1_wd[l])

        wl = w_in[l]
        w_big = jnp.concatenate([wl[:, o_qkv:o_qkv + 4 * qk_w], wl[:, o_ga:o_ga + 2 * d]],
                                axis=1).astype(BF16)
        w_small = jnp.zeros((d, small_cols), F32)
        w_small = w_small.at[:, l_cq:l_cq + q_rank].set(wl[:, o_cq:o_cq + q_rank])
        w_small = w_small.at[:, l_ckv:l_ckv + kv_rank].set(wl[:, o_ckv:o_ckv + kv_rank])
        w_small = w_small.at[:, l_ik:l_ik + idx_dim].set(wl[:, o_ik:o_ik + idx_dim])
        w_small = w_small.at[:, l_misc:l_misc + nh_idx].set(wl[:, o_iw:o_iw + nh_idx])
        w_small = w_small.at[:, l_misc + nh_dn:l_misc + 2 * nh_dn].set(wl[:, o_b:o_b + nh_dn])
        w_small = w_small.at[:, l_misc + 2 * nh_dn:l_misc + 3 * nh_dn].set(wl[:, o_a:o_a + nh_dn])
        neg_a_pad = jnp.zeros((1, LANE), F32).at[0, 2 * nh_dn:3 * nh_dn].set(-jnp.exp(a_log[l].astype(F32)))
        dtb_pad = jnp.zeros((1, LANE), F32).at[0, 2 * nh_dn:3 * nh_dn].set(dt_bias[l].astype(F32))
        gates, cq, ckv, ckv_t, kidx, misc_t, o_dn = _in_proj_deltanet(
            x1, mix_norm[l], w_big, w_small.astype(BF16), q_norm[l], kv_norm[l], idx_k_g[l],
            idx_k_b[l], neg_a_pad, dtb_pad, conv_w[l].astype(F32), dn_out_norm[l], seq=s,
            q_rank=q_rank, kv_rank=kv_rank, idx_dim=idx_dim, nh=nh_dn, dk=dk,
            w_scale=(nh_idx ** -0.5) * (idx_dim ** -0.5))

        o_sa = _dsa(cq, misc_t, kidx, ckv, ckv_t,
                    w_iq[l].T.astype(BF16), w_uq[l].T.astype(BF16),
                    jnp.transpose(w_uk[l], (1, 0, 2)).astype(BF16),
                    jnp.transpose(w_uv[l], (1, 2, 0)).astype(BF16),
                    bias_tab, batch=b, topk=topk, scale=dh_sa ** -0.5)

        x2d = _ffn(x1, ffn2_norm[l], ffn2_wg[l], ffn2_wu[l], ffn2_wd[l],
                   final_g=final_norm if l == depth - 1 else None,
                   merge=(gates, 0, o_dn, o_sa, w_o[l]))
    return x2d.reshape(b, s, d)
```
